```python
import math
import functools
import jax
import jax.numpy as jnp
from jax import lax
import numpy as np

D_MODEL = 1024
BATCH = 4
SEQ = 4096
DEPTH = 2
DEC_BATCH = 16
DEC_SEQ = 32
PAST_LEN = 4096

CHUNK = 64
Q_BLOCK = 128
HEAD_DIM = 64
H_A = 8
H_B = 8
IDX_HEADS = 8
IDX_DIM = 64
IDX_TOPK_MAX = 256
H_C = 8
DK_C = 128
DV_C = 128
CONV_W = 4
D_FF = 4 * D_MODEL
REL_BUCKETS = 32
REL_MAX_DIST = 256
FFN_RES = 0.5
EPS = 1e-6

D_A = H_A * HEAD_DIM
D_B = H_B * HEAD_DIM
AB_SIZES = (D_A, D_A, D_A, IDX_HEADS * IDX_DIM, IDX_DIM, IDX_HEADS, D_B, D_B, D_B, H_B)
AB_IN = sum(AB_SIZES)
C_QKV = 2 * H_C * DK_C + H_C * DV_C
C_SIZES = (C_QKV, H_C * DV_C, H_C, H_C)
C_IN = sum(C_SIZES)

kernel_name = 'hybrid_dsa_fox_gdn_stream_step'


def split_cols(z, sizes):
    return jnp.split(z, [int(i) for i in np.cumsum(sizes)[:-1]], axis=-1)


def rms_norm(x, g):
    xf = x.astype(jnp.float32)
    y = xf * lax.rsqrt(jnp.mean(xf * xf, axis=-1, keepdims=True) + EPS)
    return (y * g.astype(jnp.float32)).astype(x.dtype)


def l2_normalize(x):
    return x * lax.rsqrt(jnp.sum(x * x, axis=-1, keepdims=True) + EPS)


def modulate(x, g, shift, scale):
    return rms_norm(x, g) * (1.0 + scale[:, None, :]) + shift[:, None, :]


def add_branch(x, y, g, gate, weight):
    return x + weight * gate[:, None, :] * rms_norm(y, g)


def swiglu(h, w_gate, w_up, w_down):
    return (jax.nn.silu(h @ w_gate) * (h @ w_up)) @ w_down


def t5_bucket(rel):
    nb = REL_BUCKETS // 2
    max_exact = nb // 2
    base = jnp.where(rel > 0, nb, 0)
    n = jnp.abs(rel)
    nf = jnp.maximum(n, 1).astype(jnp.float32)
    large = max_exact + (jnp.log(nf / max_exact) / math.log(REL_MAX_DIST / max_exact)
                         * (nb - max_exact)).astype(jnp.int32)
    large = jnp.minimum(large, nb - 1)
    return base + jnp.where(n < max_exact, n, large)


def over_query_blocks(fn, q_pos, *q_arrays):
    tq = q_pos.shape[0]
    if tq <= Q_BLOCK:
        return fn(q_pos, *q_arrays)
    nb = tq // Q_BLOCK
    pos_b = q_pos.reshape(nb, Q_BLOCK)
    arrs_b = [jnp.moveaxis(a.reshape(a.shape[0], nb, Q_BLOCK, *a.shape[2:]), 1, 0) for a in q_arrays]
    out = lax.map(lambda xs: fn(*xs), (pos_b, *arrs_b))
    out = jnp.moveaxis(out, 0, 1)
    return out.reshape(out.shape[0], tq, *out.shape[3:])


def dsa_block(k, v, ki, rel_table, top_k, q_pos, q, qi, wi):
    n_keys = k.shape[1]
    s = jnp.einsum('bqhd,bsd->bqhs', qi, ki) * IDX_DIM ** -0.5
    score = jnp.einsum('bqhs,bqh->bqs', jax.nn.relu(s), wi).astype(jnp.float32) * IDX_HEADS ** -0.5
    admissible = (jnp.arange(n_keys) // CHUNK)[None, :] <= (q_pos // CHUNK)[:, None]
    score = jnp.where(admissible[None], score, -jnp.inf)
    top_score, top_idx = lax.top_k(score, top_k)
    take = jax.vmap(lambda rows, idx: rows[idx])
    k_sel = take(k, top_idx)
    v_sel = take(v, top_idx)
    bias = rel_table[t5_bucket(top_idx - q_pos[None, :, None])]
    logits = (jnp.einsum('bqhd,bqkhd->bqhk', q, k_sel).astype(jnp.float32) * HEAD_DIM ** -0.5
              + jnp.moveaxis(bias, -1, 2).astype(jnp.float32))
    logits = jnp.where((top_score > -jnp.inf)[:, :, None, :], logits, -jnp.inf)
    p = jax.nn.softmax(logits, axis=-1).astype(v.dtype)
    return jnp.einsum('bqhk,bqkhd->bqhd', p, v_sel)


def fox_block(k, v, f_k, q_pos, q, f_q):
    n_keys = k.shape[1]
    logits = (jnp.einsum('bqhd,bshd->bhqs', q, k).astype(jnp.float32) * HEAD_DIM ** -0.5
              + f_q.transpose(0, 2, 1)[..., None] - f_k.transpose(0, 2, 1)[:, :, None, :])
    causal = jnp.arange(n_keys)[None, :] <= q_pos[:, None]
    logits = jnp.where(causal, logits, -jnp.inf)
    p = jax.nn.softmax(logits, axis=-1).astype(v.dtype)
    return jnp.einsum('bhqs,bshd->bqhd', p, v)


def mixer_ab(h, w, past):
    B, T, _ = h.shape
    z = h @ w['ab_w_in']
    qa, ka, va, qi, ki, wi, qb, kb, vb, fl = split_cols(z, AB_SIZES)
    qa = qa.reshape(B, T, H_A, HEAD_DIM)
    ka = ka.reshape(B, T, H_A, HEAD_DIM)
    va = va.reshape(B, T, H_A, HEAD_DIM)
    qi = qi.reshape(B, T, IDX_HEADS, IDX_DIM)
    qb = qb.reshape(B, T, H_B, HEAD_DIM)
    kb = kb.reshape(B, T, H_B, HEAD_DIM)
    vb = vb.reshape(B, T, H_B, HEAD_DIM)
    logf = jax.nn.log_sigmoid((fl + w['ab_b_f']).astype(jnp.float32))
    if past is None:
        offset = 0
        ka_all, va_all, ki_all, kb_all, vb_all, logf_all = ka, va, ki, kb, vb, logf
    else:
        cak, cav, caik, cbk, cbv, cblf = past
        offset = cak.shape[1]
        ka_all = jnp.concatenate([cak, ka], axis=1)
        va_all = jnp.concatenate([cav, va], axis=1)
        ki_all = jnp.concatenate([caik, ki], axis=1)
        kb_all = jnp.concatenate([cbk, kb], axis=1)
        vb_all = jnp.concatenate([cbv, vb], axis=1)
        logf_all = jnp.concatenate([cblf.astype(jnp.float32), logf], axis=1)
    n_keys = offset + T
    q_pos = offset + jnp.arange(T, dtype=jnp.int32)
    top_k = min(IDX_TOPK_MAX, n_keys // 4)
    o_a = over_query_blocks(functools.partial(dsa_block, ka_all, va_all, ki_all, w['rel_bias'], top_k),
                            q_pos, qa, qi, wi)
    f_all = jnp.cumsum(logf_all, axis=1)
    o_b = over_query_blocks(functools.partial(fox_block, kb_all, vb_all, f_all),
                            q_pos, qb, f_all[:, offset:])
    o = jnp.concatenate([o_a.reshape(B, T, D_A), o_b.reshape(B, T, D_B)], axis=-1)
    y = o @ w['ab_w_out']
    return y, (ka, va, ki, kb, vb, logf.astype(h.dtype))


def gated_delta_chunked(q, k, v, g, beta, s0):
    B, T, H, _ = q.shape
    dv = v.shape[-1]
    c = min(T, CHUNK)
    n = T // c

    def chunks4(a):
        return a.reshape(B, n, c, H, a.shape[-1]).transpose(1, 0, 3, 2, 4)

    def chunks3(a):
        return a.reshape(B, n, c, H).transpose(1, 0, 3, 2)

    q, k, v = chunks4(q), chunks4(k), chunks4(v)
    g, beta = chunks3(g), chunks3(beta)
    gc = jnp.cumsum(g, axis=-1)
    causal = jnp.tril(jnp.ones((c, c), dtype=bool))
    strict = jnp.tril(jnp.ones((c, c), dtype=bool), k=-1)
    decay = jnp.exp(jnp.where(causal, gc[..., :, None] - gc[..., None, :], -jnp.inf))
    kk = jnp.einsum('nbhid,nbhjd->nbhij', k, k)
    a_low = jnp.where(strict, beta[..., :, None] * kk * decay, 0.0)
    lhs = a_low + jnp.eye(c, dtype=a_low.dtype)
    rhs = jnp.concatenate([v * beta[..., None], k * (beta * jnp.exp(gc))[..., None]], axis=-1)
    sol = lax.linalg.triangular_solve(lhs, rhs, left_side=True, lower=True)
    u, wk = sol[..., :dv], sol[..., dv:]
    qk = jnp.where(causal, jnp.einsum('nbhid,nbhjd->nbhij', q, k) * decay, 0.0)
    q_dec = q * jnp.exp(gc)[..., None]
    k_dec = k * jnp.exp(gc[..., -1:] - gc)[..., None]
    g_last = jnp.exp(gc[..., -1])

    def step(s, xs):
        u_c, w_c, qk_c, qd_c, kd_c, gl_c = xs
        v_new = u_c - jnp.einsum('bhcd,bhde->bhce', w_c, s)
        o_c = jnp.einsum('bhcd,bhde->bhce', qd_c, s) + jnp.einsum('bhij,bhje->bhie', qk_c, v_new)
        s = s * gl_c[..., None, None] + jnp.einsum('bhcd,bhce->bhde', kd_c, v_new)
        return s, o_c

    s_fin, o = lax.scan(step, s0, (u, wk, qk, q_dec, k_dec, g_last))
    o = o.transpose(1, 0, 3, 2, 4).reshape(B, T, H, dv)
    return o, s_fin


def mixer_gdn(h, w, past):
    B, T, _ = h.shape
    z = h @ w['gdn_w_in']
    qkv, gate, b_logit, a_logit = split_cols(z, C_SIZES)
    if past is None:
        conv_prev = jnp.zeros((B, CONV_W - 1, C_QKV), qkv.dtype)
        s0 = jnp.zeros((B, H_C, DK_C, DV_C), jnp.float32)
    else:
        conv_prev, s_prev = past
        s0 = s_prev.astype(jnp.float32)
    xp = jnp.concatenate([conv_prev, qkv], axis=1)
    cw = w['gdn_conv_w']
    conv = xp[:, 0:T] * cw[0]
    for j in range(1, CONV_W):
        conv = conv + xp[:, j:j + T] * cw[j]
    conv = jax.nn.silu(conv)
    new_conv = xp[:, xp.shape[1] - (CONV_W - 1):]
    q, k, v = split_cols(conv, (H_C * DK_C, H_C * DK_C, H_C * DV_C))
    q = l2_normalize(q.reshape(B, T, H_C, DK_C).astype(jnp.float32)) * DK_C ** -0.5
    k = l2_normalize(k.reshape(B, T, H_C, DK_C).astype(jnp.float32))
    v = v.reshape(B, T, H_C, DV_C).astype(jnp.float32)
    beta = jax.nn.sigmoid(b_logit.astype(jnp.float32))
    g = -jnp.exp(w['gdn_A_log'].astype(jnp.float32)) * jax.nn.softplus(
        a_logit.astype(jnp.float32) + w['gdn_dt_bias'].astype(jnp.float32))
    o, s_new = gated_delta_chunked(q, k, v, g, beta, s0)
    o = rms_norm(o, w['gdn_norm_w']) * jax.nn.silu(gate.reshape(B, T, H_C, DV_C).astype(jnp.float32))
    y = o.reshape(B, T, H_C * DV_C).astype(h.dtype) @ w['gdn_w_out']
    return y, (new_conv, s_new.astype(h.dtype))


def ffn_sublayer(x, w, mod, l, slot, j):
    h = modulate(x, w['norm_pre'][l, slot], mod[:, slot, 0], mod[:, slot, 1])
    y = swiglu(h, w['ffn_w_gate'][l, j], w['ffn_w_up'][l, j], w['ffn_w_down'][l, j])
    return add_branch(x, y, w['norm_post'][l, slot], mod[:, slot, 2], FFN_RES)


def forward(x, c, w, past):
    B = x.shape[0]
    new_state = []
    for l in range(DEPTH):
        mod = (jax.nn.silu(c) @ w['ada_w'][l] + w['ada_b'][l]).reshape(B, 3, 3, D_MODEL)
        x = ffn_sublayer(x, w, mod, l, 0, 0)
        h = modulate(x, w['norm_pre'][l, 1], mod[:, 1, 0], mod[:, 1, 1])
        if l % 2 == 0:
            y, st = mixer_ab(h, w, None if past is None else past[:6])
        else:
            y, st = mixer_gdn(h, w, None if past is None else past[6:])
        x = add_branch(x, y, w['norm_post'][l, 1], mod[:, 1, 2], 1.0)
        new_state.extend(st)
        x = ffn_sublayer(x, w, mod, l, 2, 1)
    return x, new_state


def setup_inputs(seed: int = 0) -> dict:
    key = jax.random.key(seed)
    ks = jax.random.split(key, 32)

    def nrm(k, shape, s):
        return s * jax.random.normal(k, shape, jnp.float32)

    dt = jnp.exp(jax.random.uniform(ks[20], (H_C,), jnp.float32, math.log(1e-3), math.log(1e-1)))
    return {
        'x_prompt': nrm(ks[0], (BATCH, SEQ, D_MODEL), 1.0),
        'x_sample': nrm(ks[1], (DEC_BATCH, DEC_SEQ, D_MODEL), 1.0),
        'cache_a_k': nrm(ks[2], (DEC_BATCH, PAST_LEN, H_A, HEAD_DIM), 1.0),
        'cache_a_v': nrm(ks[3], (DEC_BATCH, PAST_LEN, H_A, HEAD_DIM), 1.0),
        'cache_a_idx_k': nrm(ks[4], (DEC_BATCH, PAST_LEN, IDX_DIM), 1.0),
        'cache_b_k': nrm(ks[5], (DEC_BATCH, PAST_LEN, H_B, HEAD_DIM), 1.0),
        'cache_b_v': nrm(ks[6], (DEC_BATCH, PAST_LEN, H_B, HEAD_DIM), 1.0),
        'cache_b_logf': jax.nn.log_sigmoid(2.0 + nrm(ks[7], (DEC_BATCH, PAST_LEN, H_B), 1.0)),
        'state_c_conv': nrm(ks[8], (DEC_BATCH, CONV_W - 1, C_QKV), 1.0),
        'state_c_rec': nrm(ks[9], (DEC_BATCH, H_C, DK_C, DV_C), 0.2),
        'c_prompt': nrm(ks[10], (BATCH, D_MODEL), 1.0),
        'c_sample': nrm(ks[11], (DEC_BATCH, D_MODEL), 1.0),
        'rel_bias': nrm(ks[12], (REL_BUCKETS, H_A), 0.5),
        'ab_w_in': nrm(ks[13], (D_MODEL, AB_IN), D_MODEL ** -0.5),
        'ab_b_f': 2.0 + nrm(ks[14], (H_B,), 0.1),
        'ab_w_out': nrm(ks[15], (D_A + D_B, D_MODEL), (D_A + D_B) ** -0.5),
        'gdn_w_in': nrm(ks[16], (D_MODEL, C_IN), D_MODEL ** -0.5),
        'gdn_conv_w': nrm(ks[17], (CONV_W, C_QKV), CONV_W ** -0.5),
        'gdn_A_log': jnp.log(jax.random.uniform(ks[18], (H_C,), jnp.float32, 1.0, 16.0)),
        'gdn_dt_bias': dt + jnp.log(-jnp.expm1(-dt)),
        'gdn_norm_w': 1.0 + nrm(ks[19], (DV_C,), 0.1),
        'gdn_w_out': nrm(ks[21], (H_C * DV_C, D_MODEL), (H_C * DV_C) ** -0.5),
        'norm_pre': 1.0 + nrm(ks[22], (DEPTH, 3, D_MODEL), 0.1),
        'norm_post': 1.0 + nrm(ks[23], (DEPTH, 3, D_MODEL), 0.1),
        'ada_w': nrm(ks[24], (DEPTH, D_MODEL, 9 * D_MODEL), 0.5 * D_MODEL ** -0.5),
        'ada_b': nrm(ks[25], (DEPTH, 9 * D_MODEL), 0.01),
        'ffn_w_gate': nrm(ks[26], (DEPTH, 2, D_MODEL, D_FF), D_MODEL ** -0.5),
        'ffn_w_up': nrm(ks[27], (DEPTH, 2, D_MODEL, D_FF), D_MODEL ** -0.5),
        'ffn_w_down': nrm(ks[28], (DEPTH, 2, D_FF, D_MODEL), D_FF ** -0.5),
    }


def reference(x_prompt, x_sample, cache_a_k, cache_a_v, cache_a_idx_k, cache_b_k, cache_b_v, cache_b_logf,
              state_c_conv, state_c_rec, c_prompt, c_sample, rel_bias, ab_w_in, ab_b_f, ab_w_out,
              gdn_w_in, gdn_conv_w, gdn_A_log, gdn_dt_bias, gdn_norm_w, gdn_w_out, norm_pre, norm_post,
              ada_w, ada_b, ffn_w_gate, ffn_w_up, ffn_w_down):
    w = dict(rel_bias=rel_bias, ab_w_in=ab_w_in, ab_b_f=ab_b_f, ab_w_out=ab_w_out,
             gdn_w_in=gdn_w_in, gdn_conv_w=gdn_conv_w, gdn_A_log=gdn_A_log, gdn_dt_bias=gdn_dt_bias,
             gdn_norm_w=gdn_norm_w, gdn_w_out=gdn_w_out, norm_pre=norm_pre, norm_post=norm_post,
             ada_w=ada_w, ada_b=ada_b, ffn_w_gate=ffn_w_gate, ffn_w_up=ffn_w_up, ffn_w_down=ffn_w_down)
    y_prompt, st_p = forward(x_prompt, c_prompt, w, None)
    past = (cache_a_k, cache_a_v, cache_a_idx_k, cache_b_k, cache_b_v, cache_b_logf, state_c_conv, state_c_rec)
    y_sample, st_s = forward(x_sample, c_sample, w, past)
    p_a_k, p_a_v, p_a_idx_k, p_b_k, p_b_v, p_b_logf, p_c_conv, p_c_rec = st_p
    s_a_k, s_a_v, s_a_idx_k, s_b_k, s_b_v, s_b_logf, s_c_conv, s_c_rec = st_s
    return (y_prompt, y_sample,
            p_a_k, p_a_v, p_a_idx_k, p_b_k, p_b_v, p_b_logf, p_c_conv, p_c_rec,
            s_a_k, s_a_v, s_a_idx_k, s_b_k, s_b_v, s_b_logf, s_c_conv, s_c_rec)
```

```python
import functools
import math

import jax
import jax.numpy as jnp
from jax import lax
from jax.experimental import pallas as pl
from jax.experimental.pallas import tpu as pltpu

D_MODEL = 1024
D_FF = 4 * D_MODEL
CHUNK = 64
HEAD_DIM = 64
N_HEADS = 8
D_HEADS = N_HEADS * HEAD_DIM
IDX_DIM = 64
IDX_TOPK_MAX = 256
DK_C = 128
CONV_W = 4
C_QKV = 3 * N_HEADS * DK_C
REL_BUCKETS = 32
REL_MAX_DIST = 256
FFN_RES = 0.5
EPS = 1e-6

GROUP = 32
LANES = 128
NEG = -1e30
INT_MIN = -(2 ** 31)
VMEM_LIMIT = 56 * 1024 * 1024

F32 = jnp.float32
BF16 = jnp.bfloat16


def _params(*sem):
    return pltpu.CompilerParams(dimension_semantics=sem, vmem_limit_bytes=VMEM_LIMIT)


def _dot(a, b):
    return jnp.dot(a, b, preferred_element_type=F32)


def _dot_nt(a, b):
    return lax.dot_general(a, b, (((1,), (1,)), ((), ())), preferred_element_type=F32)


def _split(x):
    hi = x.astype(BF16)
    lo = (x - hi.astype(F32)).astype(BF16)
    return hi, lo


def _dot3(a, b):
    ah, al = _split(a)
    bh, bl = _split(b)
    return _dot(ah, bh) + _dot(al, bh) + _dot(ah, bl)


def _silu(x):
    return x * jax.nn.sigmoid(x)


def _modulated(x, gpre, sh, sc):
    r = lax.rsqrt(jnp.mean(x * x, axis=-1, keepdims=True) + EPS)
    y = (x * r * gpre).reshape(sh.shape[0], GROUP, x.shape[-1])
    return (y * (1.0 + sc) + sh).reshape(x.shape)


def _residual(x, y, gpost, gate, weight):
    r = lax.rsqrt(jnp.mean(y * y, axis=-1, keepdims=True) + EPS)
    yn = (y * r * gpost).reshape(gate.shape[0], GROUP, x.shape[-1])
    return x + (weight * gate * yn).reshape(x.shape)


def _ada_kernel(c_ref, w_ref, b_ref, o_ref):
    chi, clo = _split(_silu(c_ref[...]))
    whi, wlo = _split(w_ref[...])
    o_ref[...] = _dot(chi, whi) + _dot(clo, whi) + _dot(chi, wlo) + b_ref[...]


def _ada(c_all, ada_w, ada_b):
    depth, _, n_out = ada_w.shape
    nb = c_all.shape[0]
    tn = 1024
    return pl.pallas_call(
        _ada_kernel,
        out_shape=jax.ShapeDtypeStruct((depth, nb, n_out), F32),
        grid=(depth, n_out // tn),
        in_specs=[pl.BlockSpec((nb, D_MODEL), lambda l, j: (0, 0)),
                  pl.BlockSpec((None, D_MODEL, tn), lambda l, j: (l, 0, j)),
                  pl.BlockSpec((None, 1, tn), lambda l, j: (l, 0, j))],
        out_specs=pl.BlockSpec((None, nb, tn), lambda l, j: (l, 0, j)),
        compiler_params=_params("arbitrary", "arbitrary"),
        name="ada_mod",
    )(c_all, ada_w, ada_b.reshape(depth, 1, n_out))


def _ffn_kernel(x_ref, sh_ref, sc_ref, gt_ref, gpre_ref, gpost_ref, wg_ref, wu_ref, wd_ref, o_ref,
                h_scr, acc_scr):
    j = pl.program_id(1)

    @pl.when(j == 0)
    def _():
        h_scr[...] = _modulated(x_ref[...], gpre_ref[...], sh_ref[...], sc_ref[...]).astype(BF16)
        acc_scr[...] = jnp.zeros_like(acc_scr)

    h = h_scr[...]
    a = _dot(h, wg_ref[...])
    b = _dot(h, wu_ref[...])
    acc_scr[...] += _dot((_silu(a) * b).astype(BF16), wd_ref[...])

    @pl.when(j == pl.num_programs(1) - 1)
    def _():
        o_ref[...] = _residual(x_ref[...], acc_scr[...], gpost_ref[...], gt_ref[...], FFN_RES)


def _ffn(x, mod, gpre, gpost, wg, wu, wd, tm=512, tf=512):
    n = x.shape[0]
    tm = min(tm, n)
    sh, sc, gt = mod
    row = pl.BlockSpec((tm, D_MODEL), lambda i, j: (i, 0))
    grp = pl.BlockSpec((tm // GROUP, 1, D_MODEL), lambda i, j: (i, 0, 0))
    vec = pl.BlockSpec((1, D_MODEL), lambda i, j: (0, 0))
    return pl.pallas_call(
        _ffn_kernel,
        out_shape=jax.ShapeDtypeStruct((n, D_MODEL), F32),
        grid=(n // tm, D_FF // tf),
        in_specs=[row, grp, grp, grp, vec, vec,
                  pl.BlockSpec((D_MODEL, tf), lambda i, j: (0, j)),
                  pl.BlockSpec((D_MODEL, tf), lambda i, j: (0, j)),
                  pl.BlockSpec((tf, D_MODEL), lambda i, j: (j, 0))],
        out_specs=row,
        scratch_shapes=[pltpu.VMEM((tm, D_MODEL), BF16), pltpu.VMEM((tm, D_MODEL), F32)],
        compiler_params=_params("arbitrary", "arbitrary"),
        name="ffn_sublayer",
    )(x, sh, sc, gt, gpre, gpost, wg, wu, wd)


AB_MAIN = ("qa", "qb", "ka", "va", "kb", "vb")
IDX_COLS = D_HEADS + LANES


def _proj_ab_kernel(x_ref, sh_ref, sc_ref, gpre_ref, wm_ref, wih_ref, wil_ref,
                    qa_ref, qb_ref, ka_ref, va_ref, kb_ref, vb_ref,
                    ka16_ref, va16_ref, kb16_ref, vb16_ref, qi_ref, misc_ref):
    h = _modulated(x_ref[...], gpre_ref[...], sh_ref[...], sc_ref[...])
    hi, lo = _split(h)
    for n, ref in enumerate((qa_ref, qb_ref)):
        ref[...] = _dot(hi, wm_ref[:, n * D_HEADS:(n + 1) * D_HEADS]).astype(BF16)
    for n, (ref, ref16) in enumerate(((ka_ref, ka16_ref), (va_ref, va16_ref),
                                      (kb_ref, kb16_ref), (vb_ref, vb16_ref))):
        z = _dot(hi, wm_ref[:, (n + 2) * D_HEADS:(n + 3) * D_HEADS])
        ref[...] = z
        ref16[...] = z.astype(BF16)
    z = _dot(hi, wih_ref[...]) + _dot(lo, wih_ref[...]) + _dot(hi, wil_ref[...])
    qi_ref[...] = z[:, :D_HEADS]
    misc_ref[...] = z[:, D_HEADS:]


def _proj_ab(x, sh, sc, gpre, wm, wih, wil, tm=256):
    n = x.shape[0]
    tm = min(tm, n)
    row = lambda w: pl.BlockSpec((tm, w), lambda i: (i, 0))
    full = lambda a: pl.BlockSpec(a.shape, lambda i: (0, 0))
    grp = pl.BlockSpec((tm // GROUP, 1, D_MODEL), lambda i: (i, 0, 0))
    sds = lambda w, dt: jax.ShapeDtypeStruct((n, w), dt)
    out_shape = ([sds(D_HEADS, BF16)] * 2 + [sds(D_HEADS, F32)] * 4 + [sds(D_HEADS, BF16)] * 4
                 + [sds(D_HEADS, F32), sds(LANES, F32)])
    out_specs = [row(D_HEADS)] * 11 + [row(LANES)]
    return pl.pallas_call(
        _proj_ab_kernel,
        out_shape=out_shape,
        grid=(n // tm,),
        in_specs=[row(D_MODEL), grp, grp, pl.BlockSpec((1, D_MODEL), lambda i: (0, 0)),
                  full(wm), full(wih), full(wil)],
        out_specs=out_specs,
        compiler_params=_params("arbitrary"),
        name="proj_ab",
    )(x, sh, sc, gpre, wm, wih, wil)


def _lane_cumsum(x):
    lane = lax.broadcasted_iota(jnp.int32, x.shape, 1)
    s = 1
    while s < LANES:
        x = x + jnp.where(lane >= s, pltpu.roll(x, s, 1), 0.0)
        s *= 2
    return x


def _logf_kernel(*refs, n_cache, n_new):
    if n_cache:
        fl_ref, bf_ref, cache_ref, logf_ref, fnew_ref, fcache_ref = refs
    else:
        fl_ref, bf_ref, logf_ref, fnew_ref = refs
    carry = jnp.zeros((N_HEADS, 1), F32)
    for blk in range(n_cache):
        c = _lane_cumsum(cache_ref[blk]) + carry
        fcache_ref[blk] = c
        carry = c[:, LANES - 1:LANES]
    for blk in range(n_new):
        lf = jax.nn.log_sigmoid(fl_ref[blk] + bf_ref[...])
        logf_ref[blk] = lf
        c = _lane_cumsum(lf) + carry
        fnew_ref[blk] = c
        carry = c[:, LANES - 1:LANES]


def _blocked_rows(a):
    b, t, h = a.shape
    return a.reshape(b, t // LANES, LANES, h).transpose(0, 1, 3, 2)


def _unblocked_rows(a):
    b, nb, h, _ = a.shape
    return a.transpose(0, 1, 3, 2).reshape(b, nb * LANES, h)


def _logf(fl, b_f, cache_logf):
    b, t, _ = fl.shape
    n_new = t // LANES
    blk = lambda nb: pl.BlockSpec((None, nb, N_HEADS, LANES), lambda i: (i, 0, 0, 0))
    sds = lambda nb: jax.ShapeDtypeStruct((b, nb, N_HEADS, LANES), F32)
    ins = [_blocked_rows(fl), b_f.reshape(N_HEADS, 1)]
    in_specs = [blk(n_new), pl.BlockSpec((N_HEADS, 1), lambda i: (0, 0))]
    out_shape = [sds(n_new), sds(n_new)]
    out_specs = [blk(n_new), blk(n_new)]
    n_cache = 0
    if cache_logf is not None:
        n_cache = cache_logf.shape[1] // LANES
        ins.append(_blocked_rows(cache_logf))
        in_specs.append(blk(n_cache))
        out_shape.append(sds(n_cache))
        out_specs.append(blk(n_cache))
    outs = pl.pallas_call(
        functools.partial(_logf_kernel, n_cache=n_cache, n_new=n_new),
        out_shape=out_shape, grid=(b,), in_specs=in_specs, out_specs=out_specs,
        compiler_params=_params("arbitrary"), name="logf_cumsum",
    )(*ins)
    return outs


N_BIAS_TILES = 4


def _t5_bucket(rel):
    nb = REL_BUCKETS // 2
    max_exact = nb // 2
    base = jnp.where(rel > 0, nb, 0)
    n = jnp.abs(rel)
    nf = jnp.maximum(n, 1).astype(F32)
    large = max_exact + (jnp.log(nf / max_exact) / math.log(REL_MAX_DIST / max_exact)
                         * (nb - max_exact)).astype(jnp.int32)
    large = jnp.minimum(large, nb - 1)
    return base + jnp.where(n < max_exact, n, large)


def _bias_kernel(tab_ref, o_ref):
    tq = o_ref.shape[2]
    row = lax.broadcasted_iota(jnp.int32, (tq, LANES), 0)
    col = lax.broadcasted_iota(jnp.int32, (tq, LANES), 1)
    for t in range(N_BIAS_TILES):
        bucket = _t5_bucket(LANES * (t - (N_BIAS_TILES - 1)) + col - row)
        for h in range(N_HEADS):
            acc = jnp.zeros((tq, LANES), F32)
            for b in range(REL_BUCKETS):
                acc = jnp.where(bucket == b, tab_ref[b, h], acc)
            o_ref[h, t] = acc


def _bias_tiles(rel_bias, tq):
    assert LANES * (N_BIAS_TILES - 1) - (LANES - 1) >= REL_MAX_DIST
    return pl.pallas_call(
        _bias_kernel,
        out_shape=jax.ShapeDtypeStruct((N_HEADS, N_BIAS_TILES, tq, LANES), F32),
        in_specs=[pl.BlockSpec(memory_space=pltpu.SMEM)],
        name="rel_bias_tiles",
    )(rel_bias)


def _sortable(x):
    b = pltpu.bitcast(x, jnp.int32)
    return b ^ ((b >> 31) & 0x7FFFFFFF)


def _softmax_step(s, v16, carry):
    m, l, acc = carry
    m_new = jnp.maximum(m, jnp.max(s, axis=-1, keepdims=True))
    alpha = jnp.exp(m - m_new)
    p = jnp.exp(s - m_new)
    l = alpha * l + jnp.sum(p, axis=-1, keepdims=True)
    acc = alpha * acc + _dot(p.astype(BF16), v16)
    return m_new, l, acc


def _softmax_init(tq):
    return (jnp.full((tq, 1), NEG, F32), jnp.zeros((tq, 1), F32), jnp.zeros((tq, HEAD_DIM), F32))


def _head(ref, rows, h):
    return ref[rows, h * HEAD_DIM:(h + 1) * HEAD_DIM]


def _dsa_kernel(*refs, tq, n_cache, past, t_new, top_k, dyn_new):
    if n_cache:
        (q_ref, qi_ref, mq_ref, cki_ref, cka_ref, cva_ref, nki_ref, nka_ref, nva_ref, bias_ref,
         o_ref, keys_ref, madd_ref) = refs
    else:
        (q_ref, qi_ref, mq_ref, nki_ref, nka_ref, nva_ref, bias_ref, o_ref, keys_ref, madd_ref) = refs
        cki_ref = cka_ref = cva_ref = None
    i = pl.program_id(1)
    n_new = (i + 1) * (tq // LANES) if dyn_new else 1
    n_blk = n_cache + n_new
    qpos0 = past + i * tq
    qpos = qpos0 + lax.broadcasted_iota(jnp.int32, (tq, 1), 0)
    col = lax.broadcasted_iota(jnp.int32, (tq, LANES), 1)
    n_keys = past + t_new

    qi = qi_ref[...]
    qparts = [_split(qi[:, h * IDX_DIM:(h + 1) * IDX_DIM]) for h in range(N_HEADS)]
    wi = [mq_ref[:, IDX_DIM + h:IDX_DIM + h + 1] for h in range(N_HEADS)]
    score_scale = IDX_DIM ** -0.5 * N_HEADS ** -0.5

    def score_block(ki, kpos0, slot):
        khi, klo = _split(ki)
        acc = jnp.zeros((tq, LANES), F32)
        for h in range(N_HEADS):
            qhi, qlo = qparts[h]
            s = _dot_nt(qhi, khi) + _dot_nt(qlo, khi) + _dot_nt(qhi, klo)
            acc = acc + jnp.maximum(s, 0.0) * wi[h]
        kpos = kpos0 + col
        adm = jnp.logical_and((kpos // CHUNK) <= (qpos // CHUNK), kpos < n_keys)
        keys_ref[slot] = jnp.where(adm, _sortable(acc * score_scale), INT_MIN)

    if n_cache:
        def cache_scores(j, _):
            r0 = pl.multiple_of(j * LANES, LANES)
            score_block(cki_ref[pl.ds(r0, LANES), :], j * LANES, j)
            return 0
        lax.fori_loop(0, n_cache, cache_scores, 0)

    def new_scores(j, _):
        r0 = pl.multiple_of(j * LANES, LANES)
        score_block(nki_ref[pl.ds(r0, LANES), :IDX_DIM], past + j * LANES, n_cache + j)
        return 0
    lax.fori_loop(0, n_new, new_scores, 0)

    def count(pred):
        def body(j, c):
            return c + jnp.where(pred(keys_ref[j]), 1.0, 0.0)
        c = lax.fori_loop(0, n_blk, body, jnp.zeros((tq, LANES), F32))
        return jnp.sum(c, axis=-1, keepdims=True)

    def search(b, ans):
        cand = ans ^ lax.shift_left(jnp.int32(1), jnp.int32(31) - b.astype(jnp.int32))
        cand_b = jnp.broadcast_to(cand, (tq, LANES))
        cnt = count(lambda k: k >= cand_b)
        return jnp.where(cnt >= top_k, cand, ans)

    kth = lax.fori_loop(0, 32, search, jnp.full((tq, 1), INT_MIN, jnp.int32))
    kth_b = jnp.broadcast_to(kth, (tq, LANES))
    n_gt = count(lambda k: k > kth_b)
    quota = top_k - n_gt
    tri = (lax.broadcasted_iota(jnp.int32, (LANES, LANES), 0)
           <= lax.broadcasted_iota(jnp.int32, (LANES, LANES), 1)).astype(BF16)

    def build_mask(j, seen):
        k = keys_ref[j]
        eq = jnp.where(k == kth_b, jnp.where(k != INT_MIN, 1.0, 0.0), 0.0)
        rank = _dot(eq.astype(BF16), tri) + seen
        sel = jnp.where(k > kth_b, 1.0, jnp.where(rank <= quota, eq, 0.0))
        madd_ref[j] = (1.0 - sel) * NEG
        return seen + jnp.sum(eq, axis=-1, keepdims=True)
    lax.fori_loop(0, n_blk, build_mask, jnp.zeros((tq, 1), F32))

    for h in range(N_HEADS):
        q16 = (_head(q_ref, slice(None), h).astype(F32) * HEAD_DIM ** -0.5).astype(BF16)

        def step(k, v, slot, kblk, carry):
            tile = jnp.clip(kblk - qpos0 // LANES + (N_BIAS_TILES - 1), 0, N_BIAS_TILES - 1)
            s = _dot_nt(q16, k.astype(BF16)) + bias_ref[h, tile] + madd_ref[slot]
            return _softmax_step(s, v.astype(BF16), carry)

        carry = _softmax_init(tq)
        if n_cache:
            def cache_step(j, carry):
                rows = pl.ds(pl.multiple_of(j * LANES, LANES), LANES)
                return step(_head(cka_ref, rows, h), _head(cva_ref, rows, h), j, j, carry)
            carry = lax.fori_loop(0, n_cache, cache_step, carry)

        def new_step(j, carry):
            rows = pl.ds(pl.multiple_of(j * LANES, LANES), LANES)
            return step(_head(nka_ref, rows, h), _head(nva_ref, rows, h), n_cache + j,
                        past // LANES + j, carry)
        _, l, acc = lax.fori_loop(0, n_new, new_step, carry)
        o_ref[:, h * HEAD_DIM:(h + 1) * HEAD_DIM] = (acc / l).astype(BF16)


def _dsa(q, qi, misc, ka16, va16, bias, cache, *, batch, t_q, tq):
    t_kv = ka16.shape[0] // batch
    nq = t_q // tq
    past = 0 if cache is None else cache[0].shape[1]
    n_cache = past // LANES
    n_keys = past + t_q
    top_k = min(IDX_TOPK_MAX, n_keys // 4)
    assert tq == LANES or nq == 1
    qrow = lambda w: pl.BlockSpec((tq, w), lambda b, i: (b * (t_kv // tq) + i, 0))
    krow = lambda w: pl.BlockSpec((t_kv, w), lambda b, i: (b, 0))
    crow = lambda w: pl.BlockSpec((None, past, w), lambda b, i: (b, 0, 0))
    ins = [q, qi, misc]
    in_specs = [qrow(D_HEADS), qrow(D_HEADS), qrow(LANES)]
    if cache is not None:
        ins += list(cache)
        in_specs += [crow(IDX_DIM), crow(D_HEADS), crow(D_HEADS)]
    ins += [misc, ka16, va16, bias]
    in_specs += [krow(LANES), krow(D_HEADS), krow(D_HEADS),
                 pl.BlockSpec(bias.shape, lambda b, i: (0, 0, 0, 0))]
    n_blk_max = n_cache + t_kv // LANES
    return pl.pallas_call(
        functools.partial(_dsa_kernel, tq=tq, n_cache=n_cache, past=past, t_new=t_q, top_k=top_k,
                          dyn_new=cache is None),
        out_shape=jax.ShapeDtypeStruct((batch * t_kv, D_HEADS), BF16),
        grid=(batch, nq),
        in_specs=in_specs,
        out_specs=qrow(D_HEADS),
        scratch_shapes=[pltpu.VMEM((n_blk_max, tq, LANES), jnp.int32),
                        pltpu.VMEM((n_blk_max, tq, LANES), F32)],
        compiler_params=_params("arbitrary", "arbitrary"),
        name="dsa_attention",
    )(*ins)


def _fox_kernel(*refs, tq, n_cache, past, dyn_new):
    if n_cache:
        q_ref, fq_ref, ck_ref, cv_ref, cf_ref, nk_ref, nv_ref, nf_ref, o_ref = refs
    else:
        q_ref, fq_ref, nk_ref, nv_ref, nf_ref, o_ref = refs
        ck_ref = cv_ref = cf_ref = None
    i = pl.program_id(1)
    n_new = (i + 1) * (tq // LANES) if dyn_new else 1
    n_unmasked = i * (tq // LANES) if dyn_new else 0
    qpos = past + i * tq + lax.broadcasted_iota(jnp.int32, (tq, 1), 0)
    col = lax.broadcasted_iota(jnp.int32, (tq, LANES), 1)

    for h in range(N_HEADS):
        q16 = (_head(q_ref, slice(None), h).astype(F32) * HEAD_DIM ** -0.5).astype(BF16)
        fq = fq_ref[:, h:h + 1]

        def step(k, v, fk, carry, kpos0=None):
            s = _dot_nt(q16, k.astype(BF16)) + fq - fk
            if kpos0 is not None:
                s = jnp.where(kpos0 + col <= qpos, s, NEG)
            return _softmax_step(s, v.astype(BF16), carry)

        carry = _softmax_init(tq)
        if n_cache:
            def cache_step(j, carry):
                rows = pl.ds(pl.multiple_of(j * LANES, LANES), LANES)
                return step(_head(ck_ref, rows, h), _head(cv_ref, rows, h), cf_ref[j][h:h + 1, :], carry)
            carry = lax.fori_loop(0, n_cache, cache_step, carry)

        def new_step(j, carry, masked):
            rows = pl.ds(pl.multiple_of(j * LANES, LANES), LANES)
            return step(_head(nk_ref, rows, h), _head(nv_ref, rows, h), nf_ref[j][h:h + 1, :], carry,
                        past + j * LANES if masked else None)
        carry = lax.fori_loop(0, n_unmasked, functools.partial(new_step, masked=False), carry)
        _, l, acc = lax.fori_loop(n_unmasked, n_new, functools.partial(new_step, masked=True), carry)
        o_ref[:, h * HEAD_DIM:(h + 1) * HEAD_DIM] = (acc / l).astype(BF16)


def _fox(q, fq, kb16, vb16, f_new, cache, *, batch, t_q, tq):
    t_kv = kb16.shape[0] // batch
    nq = t_q // tq
    past = 0 if cache is None else cache[0].shape[1]
    n_cache = past // LANES
    qrow = lambda w: pl.BlockSpec((tq, w), lambda b, i: (b * (t_kv // tq) + i, 0))
    krow = lambda w: pl.BlockSpec((t_kv, w), lambda b, i: (b, 0))
    crow = lambda w: pl.BlockSpec((None, past, w), lambda b, i: (b, 0, 0))
    fblk = lambda nb: pl.BlockSpec((None, nb, N_HEADS, LANES), lambda b, i: (b, 0, 0, 0))
    ins = [q, fq]
    in_specs = [qrow(D_HEADS), qrow(N_HEADS)]
    if cache is not None:
        ins += list(cache)
        in_specs += [crow(D_HEADS), crow(D_HEADS), fblk(n_cache)]
    ins += [kb16, vb16, f_new]
    in_specs += [krow(D_HEADS), krow(D_HEADS), fblk(t_kv // LANES)]
    return pl.pallas_call(
        functools.partial(_fox_kernel, tq=tq, n_cache=n_cache, past=past, dyn_new=cache is None),
        out_shape=jax.ShapeDtypeStruct((batch * t_kv, D_HEADS), BF16),
        grid=(batch, nq),
        in_specs=in_specs,
        out_specs=qrow(D_HEADS),
        compiler_params=_params("arbitrary", "arbitrary"),
        name="fox_attention",
    )(*ins)


def _out_ab_kernel(oa_ref, ob_ref, w_ref, x_ref, gt_ref, gpost_ref, o_ref):
    y = _dot(oa_ref[...], w_ref[:D_HEADS, :]) + _dot(ob_ref[...], w_ref[D_HEADS:, :])
    o_ref[...] = _residual(x_ref[...], y, gpost_ref[...], gt_ref[...], 1.0)


def _out_ab(oa, ob, w, x, gt, gpost, *, batch, t_q, tm):
    t_kv = oa.shape[0] // batch
    nt = t_q // tm
    arow = pl.BlockSpec((tm, D_HEADS), lambda b, i: (b * (t_kv // tm) + i, 0))
    xrow = pl.BlockSpec((tm, D_MODEL), lambda b, i: (b * nt + i, 0))
    return pl.pallas_call(
        _out_ab_kernel,
        out_shape=jax.ShapeDtypeStruct(x.shape, F32),
        grid=(batch, nt),
        in_specs=[arow, arow, pl.BlockSpec(w.shape, lambda b, i: (0, 0)), xrow,
                  pl.BlockSpec((tm // GROUP, 1, D_MODEL), lambda b, i: (b * nt + i, 0, 0)),
                  pl.BlockSpec((1, D_MODEL), lambda b, i: (0, 0))],
        out_specs=xrow,
        compiler_params=_params("arbitrary", "arbitrary"),
        name="out_ab",
    )(oa, ob, w, x, gt, gpost)


GDN_MISC0 = C_QKV + N_HEADS * DK_C


def _proj_gdn_kernel(x_ref, sh_ref, sc_ref, gpre_ref, w_ref, cw_ref, prev_ref, alog_ref, dtb_ref,
                     q_ref, k_ref, v_ref, gz_ref, misc_ref, tail_ref, zbuf):
    tm = x_ref.shape[0]
    halo = CONV_W - 1
    base = 8

    @pl.when(pl.program_id(1) == 0)
    def _():
        zbuf[base - halo:base, :] = prev_ref[...]

    h = _modulated(x_ref[...], gpre_ref[...], sh_ref[...], sc_ref[...]).astype(BF16)
    zbuf[base:base + tm, :] = _dot(h, w_ref[:, :C_QKV])
    gz_ref[...] = _dot(h, w_ref[:, C_QKV:GDN_MISC0])
    zm = _dot(h, w_ref[:, GDN_MISC0:])
    lane = lax.broadcasted_iota(jnp.int32, zm.shape, 1)
    beta = jax.nn.sigmoid(zm)
    g = -jnp.exp(alog_ref[...]) * jax.nn.softplus(zm + dtb_ref[...])
    misc_ref[...] = jnp.where(lane < N_HEADS, beta, g)

    for part, ref in enumerate((q_ref, k_ref, v_ref)):
        for hd in range(N_HEADS):
            c0 = part * N_HEADS * DK_C + hd * DK_C
            cols = slice(c0, c0 + DK_C)
            conv = zbuf[base:base + tm, cols] * cw_ref[halo:halo + 1, cols]
            for j in range(halo):
                conv = conv + zbuf[base - halo + j:base - halo + j + tm, cols] * cw_ref[j:j + 1, cols]
            a = _silu(conv)
            if part < 2:
                a = a * lax.rsqrt(jnp.sum(a * a, axis=-1, keepdims=True) + EPS)
                if part == 0:
                    a = a * DK_C ** -0.5
            ref[:, hd * DK_C:(hd + 1) * DK_C] = a
    tail = zbuf[base + tm - halo:base + tm, :]
    tail_ref[...] = tail
    zbuf[base - halo:base, :] = tail


def _proj_gdn(x, sh, sc, gpre, w, conv_w, conv_prev, alog_row, dtb_row, *, batch, tm):
    n = x.shape[0]
    t = n // batch
    nt = t // tm
    row = lambda wd: pl.BlockSpec((tm, wd), lambda b, i: (b * nt + i, 0))
    full = lambda a: pl.BlockSpec(a.shape, lambda b, i: (0,) * a.ndim)
    sds = lambda wd: jax.ShapeDtypeStruct((n, wd), F32)
    hd = N_HEADS * DK_C
    tail = pl.BlockSpec((None, CONV_W - 1, C_QKV), lambda b, i: (b, 0, 0))
    return pl.pallas_call(
        _proj_gdn_kernel,
        out_shape=[sds(hd), sds(hd), sds(hd), sds(hd), sds(LANES),
                   jax.ShapeDtypeStruct((batch, CONV_W - 1, C_QKV), F32)],
        grid=(batch, nt),
        in_specs=[row(D_MODEL),
                  pl.BlockSpec((tm // GROUP, 1, D_MODEL), lambda b, i: (b * nt + i, 0, 0)),
                  pl.BlockSpec((tm // GROUP, 1, D_MODEL), lambda b, i: (b * nt + i, 0, 0)),
                  pl.BlockSpec((1, D_MODEL), lambda b, i: (0, 0)),
                  full(w), full(conv_w), tail, full(alog_row), full(dtb_row)],
        out_specs=[row(hd), row(hd), row(hd), row(hd), row(LANES), tail],
        scratch_shapes=[pltpu.VMEM((8 + tm, C_QKV), F32)],
        compiler_params=_params("arbitrary", "arbitrary"),
        name="proj_gdn",
    )(x, sh, sc, gpre, w, conv_w, conv_prev, alog_row, dtb_row)


def _chunk_cumsum_kernel(g_ref, o_ref, *, chunk):
    x = g_ref[...]
    lane = lax.broadcasted_iota(jnp.int32, x.shape, 1) % chunk
    s = 1
    while s < chunk:
        x = x + jnp.where(lane >= s, pltpu.roll(x, s, 1), 0.0)
        s *= 2
    o_ref[...] = x


def _chunk_cumsum(g_rows, chunk):
    assert LANES % chunk == 0
    tiles = g_rows.reshape(-1, LANES)
    return pl.pallas_call(
        functools.partial(_chunk_cumsum_kernel, chunk=chunk),
        out_shape=jax.ShapeDtypeStruct(tiles.shape, F32),
        name="gdn_chunk_cumsum",
    )(tiles).reshape(g_rows.shape)


def _gdn_kernel(q_ref, k_ref, v_ref, col_ref, row_ref, s0_ref, o_ref, s_ref):
    c = q_ref.shape[0]

    @pl.when(pl.program_id(1) == 0)
    def _():
        s_ref[...] = s0_ref[...]

    ri = lax.broadcasted_iota(jnp.int32, (c, c), 0)
    ci = lax.broadcasted_iota(jnp.int32, (c, c), 1)
    causal = ri >= ci
    strict = ri > ci
    eye = jnp.where(ri == ci, 1.0, 0.0)
    for h in range(N_HEADS):
        cols = slice(h * DK_C, (h + 1) * DK_C)
        q, k, v = q_ref[:, cols], k_ref[:, cols], v_ref[:, cols]
        beta = col_ref[:, h:h + 1]
        gc = col_ref[:, N_HEADS + h:N_HEADS + h + 1]
        gc_row = row_ref[h:h + 1, :]
        gc_last = gc[c - 1:c, :]
        decay = jnp.exp(jnp.where(causal, gc - gc_row, NEG))
        k16 = k.astype(BF16)
        a = jnp.where(strict, beta * _dot_nt(k16, k16) * decay, 0.0)
        inv = eye - a
        p = a
        n = 2
        while n < c:
            p = _dot3(p, p)
            inv = inv + _dot3(inv, p)
            n *= 2
        rhs = jnp.concatenate([v * beta, k * (beta * jnp.exp(gc))], axis=-1)
        sol = _dot3(inv, rhs)
        u, w = sol[:, :DK_C], sol[:, DK_C:]
        qk = jnp.where(causal, _dot_nt(q.astype(BF16), k16) * decay, 0.0)
        q_dec = (q * jnp.exp(gc)).astype(BF16)
        k_dec_t = (k * jnp.exp(gc_last - gc)).T.astype(BF16)
        s = s_ref[h]
        s16 = s.astype(BF16)
        v_new = u - _dot(w.astype(BF16), s16)
        v16 = v_new.astype(BF16)
        o_ref[:, cols] = _dot(q_dec, s16) + _dot(qk.astype(BF16), v16)
        s_ref[h] = s * jnp.exp(gc_last) + _dot(k_dec_t, v16)


def _gdn(q, k, v, col, row, s0, *, batch, chunk):
    n = q.shape[0]
    nc = n // batch // chunk
    hd = N_HEADS * DK_C
    blk = lambda w: pl.BlockSpec((chunk, w), lambda b, i: (b * nc + i, 0))
    state = pl.BlockSpec((None, N_HEADS, DK_C, DK_C), lambda b, i: (b, 0, 0, 0))
    return pl.pallas_call(
        _gdn_kernel,
        out_shape=[jax.ShapeDtypeStruct((n, hd), F32), jax.ShapeDtypeStruct(s0.shape, F32)],
        grid=(batch, nc),
        in_specs=[blk(hd), blk(hd), blk(hd), blk(2 * N_HEADS),
                  pl.BlockSpec((None, N_HEADS, chunk), lambda b, i: (b * nc + i, 0, 0)), state],
        out_specs=[blk(hd), state],
        compiler_params=_params("arbitrary", "arbitrary"),
        name="gated_delta_rule",
    )(q, k, v, col, row, s0)


def _out_gdn_kernel(o_ref, gz_ref, nw_ref, w_ref, x_ref, gt_ref, gpost_ref, out_ref):
    parts = []
    for h in range(N_HEADS):
        cols = slice(h * DK_C, (h + 1) * DK_C)
        o = o_ref[:, cols]
        o = o * lax.rsqrt(jnp.mean(o * o, axis=-1, keepdims=True) + EPS) * nw_ref[...]
        parts.append((o * _silu(gz_ref[:, cols])).astype(BF16))
    y = _dot(jnp.concatenate(parts, axis=-1), w_ref[...])
    out_ref[...] = _residual(x_ref[...], y, gpost_ref[...], gt_ref[...], 1.0)


def _out_gdn(o, gz, norm_w, w, x, gt, gpost, tm=256):
    n = x.shape[0]
    tm = min(tm, n)
    row = pl.BlockSpec((tm, D_MODEL), lambda i: (i, 0))
    return pl.pallas_call(
        _out_gdn_kernel,
        out_shape=jax.ShapeDtypeStruct(x.shape, F32),
        grid=(n // tm,),
        in_specs=[row, row, pl.BlockSpec((1, DK_C), lambda i: (0, 0)), pl.BlockSpec(w.shape, lambda i: (0, 0)),
                  row, pl.BlockSpec((tm // GROUP, 1, D_MODEL), lambda i: (i, 0, 0)),
                  pl.BlockSpec((1, D_MODEL), lambda i: (0, 0))],
        out_specs=row,
        compiler_params=_params("arbitrary"),
        name="out_gdn",
    )(o, gz, norm_w, w, x, gt, gpost)


def _mixer_ab(x, mod, w, past, *, batch, t, bias):
    sh, sc, gt = mod
    (qa, qb, ka, va, kb, vb, ka16, va16, kb16, vb16, qi, misc) = _proj_ab(
        x, sh, sc, w["pre"], w["ab_main"], w["ab_idx_hi"], w["ab_idx_lo"])
    ki = misc[:, :IDX_DIM]
    fl = misc[:, IDX_DIM + N_HEADS:IDX_DIM + 2 * N_HEADS].reshape(batch, t, N_HEADS)
    t_kv = -(-t // LANES) * LANES
    pad_rows = lambda a: a if t_kv == t else jnp.pad(
        a.reshape(batch, t, -1), ((0, 0), (0, t_kv - t), (0, 0))).reshape(batch * t_kv, -1)
    fl_p = fl if t_kv == t else jnp.pad(fl, ((0, 0), (0, t_kv - t), (0, 0)))
    if past is None:
        logf_b, f_new = _logf(fl_p, w["ab_b_f"], None)
        cache_a = cache_b = None
    else:
        cak, cav, caik, cbk, cbv, cblf = past
        p = cak.shape[1]
        logf_b, f_new, f_cache = _logf(fl_p, w["ab_b_f"], cblf)
        cache_a = (caik, cak.reshape(batch, p, D_HEADS), cav.reshape(batch, p, D_HEADS))
        cache_b = (cbk.reshape(batch, p, D_HEADS), cbv.reshape(batch, p, D_HEADS), f_cache)
    logf = _unblocked_rows(logf_b)[:, :t]
    fq = _unblocked_rows(f_new).reshape(batch * t_kv, N_HEADS)
    tq = min(LANES, t)
    o_a = _dsa(pad_rows(qa), pad_rows(qi), pad_rows(misc), pad_rows(ka16), pad_rows(va16), bias, cache_a,
               batch=batch, t_q=t, tq=tq)
    o_b = _fox(pad_rows(qb), fq, pad_rows(kb16), pad_rows(vb16), f_new, cache_b, batch=batch, t_q=t, tq=tq)
    x = _out_ab(o_a, o_b, w["ab_w_out"], x, gt, w["post"], batch=batch, t_q=t, tm=min(256, t))
    hd = (batch, t, N_HEADS, HEAD_DIM)
    state = (ka.reshape(hd), va.reshape(hd), ki.reshape(batch, t, IDX_DIM), kb.reshape(hd), vb.reshape(hd), logf)
    return x, state


def _mixer_gdn(x, mod, w, past, *, batch, t):
    sh, sc, gt = mod
    if past is None:
        conv_prev = jnp.zeros((batch, CONV_W - 1, C_QKV), F32)
        s0 = jnp.zeros((batch, N_HEADS, DK_C, DK_C), F32)
    else:
        conv_prev, s0 = past
    q, k, v, gz, misc, new_conv = _proj_gdn(x, sh, sc, w["pre"], w["gdn_w_in"], w["gdn_conv_w"], conv_prev,
                                            w["gdn_alog_row"], w["gdn_dtb_row"], batch=batch, tm=min(256, t))
    chunk = min(t, CHUNK)
    n = batch * t
    gc_rows = _chunk_cumsum(misc[:, N_HEADS:2 * N_HEADS].T, chunk)
    col = jnp.concatenate([misc[:, :N_HEADS], gc_rows.T], axis=-1)
    row = gc_rows.reshape(N_HEADS, n // chunk, chunk).transpose(1, 0, 2)
    o, s_new = _gdn(q, k, v, col, row, s0, batch=batch, chunk=chunk)
    x = _out_gdn(o, gz, w["gdn_norm_w"], w["gdn_w_out"], x, gt, w["post"])
    return x, (new_conv, s_new)


def _forward(x, mod_all, w, past, *, batch, t):
    new_state = []
    groups = lambda v: jnp.repeat(v, t // GROUP, axis=0)[:, None, :]
    for l in range(mod_all.shape[0]):
        m = mod_all[l].reshape(batch, 3, 3, D_MODEL)
        mod = [[groups(m[:, s, j]) for j in range(3)] for s in range(3)]
        pre = lambda s: w["norm_pre"][l, s][None, :]
        post = lambda s: w["norm_post"][l, s][None, :]
        x = _ffn(x, mod[0], pre(0), post(0), w["ffn_w_gate"][l, 0], w["ffn_w_up"][l, 0], w["ffn_w_down"][l, 0])
        wl = dict(w, pre=pre(1), post=post(1))
        if l % 2 == 0:
            x, st = _mixer_ab(x, mod[1], wl, None if past is None else past[:6], batch=batch, t=t,
                              bias=w["bias_tiles"][min(LANES, t)])
        else:
            x, st = _mixer_gdn(x, mod[1], wl, None if past is None else past[6:], batch=batch, t=t)
        new_state.extend(st)
        x = _ffn(x, mod[2], pre(2), post(2), w["ffn_w_gate"][l, 1], w["ffn_w_up"][l, 1], w["ffn_w_down"][l, 1])
    return x, new_state


def _prepare_weights(rel_bias, ab_w_in, ab_b_f, ab_w_out, gdn_w_in, gdn_conv_w, gdn_A_log, gdn_dt_bias,
                     gdn_norm_w, gdn_w_out, norm_pre, norm_post, ffn_w_gate, ffn_w_up, ffn_w_down, tqs):
    o = [0]
    for s in (D_HEADS, D_HEADS, D_HEADS, D_HEADS, IDX_DIM, N_HEADS, D_HEADS, D_HEADS, D_HEADS, N_HEADS):
        o.append(o[-1] + s)
    cols = lambda i: ab_w_in[:, o[i]:o[i + 1]]
    qa, ka, va, qi, ki, wi, qb, kb, vb, fl = (cols(i) for i in range(10))
    idx = jnp.concatenate([qi, ki, wi, fl, jnp.zeros((D_MODEL, IDX_COLS - D_HEADS - IDX_DIM - 2 * N_HEADS), F32)], 1)
    idx_hi = idx.astype(BF16)
    gdn_in = jnp.concatenate([gdn_w_in, jnp.zeros((D_MODEL, LANES - 2 * N_HEADS), F32)], 1)
    lane_row = lambda v: jnp.concatenate([jnp.zeros((N_HEADS,), F32), v.astype(F32),
                                          jnp.zeros((LANES - 2 * N_HEADS,), F32)])[None, :]
    return dict(
        ab_main=jnp.concatenate([qa, qb, ka, va, kb, vb], 1).astype(BF16),
        ab_idx_hi=idx_hi, ab_idx_lo=(idx - idx_hi.astype(F32)).astype(BF16),
        ab_b_f=ab_b_f, ab_w_out=ab_w_out.astype(BF16),
        gdn_w_in=gdn_in.astype(BF16), gdn_conv_w=gdn_conv_w,
        gdn_alog_row=lane_row(gdn_A_log), gdn_dtb_row=lane_row(gdn_dt_bias),
        gdn_norm_w=gdn_norm_w[None, :], gdn_w_out=gdn_w_out.astype(BF16),
        norm_pre=norm_pre, norm_post=norm_post,
        ffn_w_gate=ffn_w_gate.astype(BF16), ffn_w_up=ffn_w_up.astype(BF16), ffn_w_down=ffn_w_down.astype(BF16),
        bias_tiles={tq: _bias_tiles(rel_bias, tq) for tq in tqs},
    )


def kernel(x_prompt, x_sample, cache_a_k, cache_a_v, cache_a_idx_k, cache_b_k, cache_b_v, cache_b_logf,
           state_c_conv, state_c_rec, c_prompt, c_sample, rel_bias, ab_w_in, ab_b_f, ab_w_out,
           gdn_w_in, gdn_conv_w, gdn_A_log, gdn_dt_bias, gdn_norm_w, gdn_w_out, norm_pre, norm_post,
           ada_w, ada_b, ffn_w_gate, ffn_w_up, ffn_w_down):
    bp, tp, _ = x_prompt.shape
    bs, ts, _ = x_sample.shape
    w = _prepare_weights(rel_bias, ab_w_in, ab_b_f, ab_w_out, gdn_w_in, gdn_conv_w, gdn_A_log, gdn_dt_bias,
                         gdn_norm_w, gdn_w_out, norm_pre, norm_post, ffn_w_gate, ffn_w_up, ffn_w_down,
                         tqs={min(LANES, tp), min(LANES, ts)})
    mod = _ada(jnp.concatenate([c_prompt, c_sample], 0), ada_w, ada_b)
    y_p, st_p = _forward(x_prompt.reshape(bp * tp, D_MODEL), mod[:, :bp], w, None, batch=bp, t=tp)
    past = (cache_a_k, cache_a_v, cache_a_idx_k, cache_b_k, cache_b_v, cache_b_logf, state_c_conv, state_c_rec)
    y_s, st_s = _forward(x_sample.reshape(bs * ts, D_MODEL), mod[:, bp:], w, past, batch=bs, t=ts)
    return (y_p.reshape(bp, tp, D_MODEL), y_s.reshape(bs, ts, D_MODEL), *st_p, *st_s)
```

```python
import functools
import math

import jax
import jax.numpy as jnp
from jax import lax
from jax.experimental import pallas as pl
from jax.experimental.pallas import tpu as pltpu

D_MODEL = 1024
D_FF = 4 * D_MODEL
CHUNK = 64
HEAD_DIM = 64
N_HEADS = 8
D_HEADS = N_HEADS * HEAD_DIM
IDX_DIM = 64
IDX_TOPK_MAX = 256
DK_C = 128
CONV_W = 4
C_QKV = 3 * N_HEADS * DK_C
REL_BUCKETS = 32
REL_MAX_DIST = 256
FFN_RES = 0.5
EPS = 1e-6

GROUP = 32
LANES = 128
NEG = -1e30
INT_MIN = -(2 ** 31)
VMEM_LIMIT = 56 * 1024 * 1024

F32 = jnp.float32
BF16 = jnp.bfloat16


def _params(*sem):
    return pltpu.CompilerParams(dimension_semantics=sem, vmem_limit_bytes=VMEM_LIMIT)


def _dot(a, b):
    return jnp.dot(a, b, preferred_element_type=F32)


def _dot_nt(a, b):
    return lax.dot_general(a, b, (((1,), (1,)), ((), ())), preferred_element_type=F32)


def _bmm(a, b):
    return lax.dot_general(a, b, (((2,), (1,)), ((0,), (0,))), preferred_element_type=F32)


def _bmm_nt(a, b):
    return lax.dot_general(a, b, (((2,), (2,)), ((0,), (0,))), preferred_element_type=F32)


def _split(x):
    hi = x.astype(BF16)
    lo = (x - hi.astype(F32)).astype(BF16)
    return hi, lo


def _dot3(a, b, mm=_dot):
    ah, al = _split(a)
    bh, bl = _split(b)
    return mm(ah, bh) + mm(al, bh) + mm(ah, bl)


def _silu(x):
    return x * jax.nn.sigmoid(x)


def _modulated(x, gpre, sh, sc):
    r = lax.rsqrt(jnp.mean(x * x, axis=-1, keepdims=True) + EPS)
    y = (x * r * gpre).reshape(sh.shape[0], GROUP, x.shape[-1])
    return (y * (1.0 + sc) + sh).reshape(x.shape)


def _residual(x, y, gpost, gate, weight):
    r = lax.rsqrt(jnp.mean(y * y, axis=-1, keepdims=True) + EPS)
    yn = (y * r * gpost).reshape(gate.shape[0], GROUP, x.shape[-1])
    return x + (weight * gate * yn).reshape(x.shape)


def _ada_kernel(c_ref, w_ref, b_ref, o_ref):
    o_ref[...] = _dot3(_silu(c_ref[...]), w_ref[...]) + b_ref[...]


def _ada(c_all, ada_w, ada_b):
    depth, _, n_out = ada_w.shape
    nb = c_all.shape[0]
    tn = 1024
    return pl.pallas_call(
        _ada_kernel,
        out_shape=jax.ShapeDtypeStruct((depth, nb, n_out), F32),
        grid=(depth, n_out // tn),
        in_specs=[pl.BlockSpec((nb, D_MODEL), lambda l, j: (0, 0)),
                  pl.BlockSpec((None, D_MODEL, tn), lambda l, j: (l, 0, j)),
                  pl.BlockSpec((None, 1, tn), lambda l, j: (l, 0, j))],
        out_specs=pl.BlockSpec((None, nb, tn), lambda l, j: (l, 0, j)),
        compiler_params=_params("arbitrary", "arbitrary"),
        name="ada_mod",
    )(c_all, ada_w, ada_b.reshape(depth, 1, n_out))


def _ffn_kernel(x_ref, sh_ref, sc_ref, gt_ref, gpre_ref, gpost_ref, wg_ref, wu_ref, wd_ref, o_ref,
                h_scr, acc_scr):
    j = pl.program_id(1)

    @pl.when(j == 0)
    def _():
        h_scr[...] = _modulated(x_ref[...], gpre_ref[...], sh_ref[...], sc_ref[...]).astype(BF16)
        acc_scr[...] = jnp.zeros_like(acc_scr)

    h = h_scr[...]
    a = _dot(h, wg_ref[...])
    b = _dot(h, wu_ref[...])
    acc_scr[...] += _dot((_silu(a) * b).astype(BF16), wd_ref[...])

    @pl.when(j == pl.num_programs(1) - 1)
    def _():
        o_ref[...] = _residual(x_ref[...], acc_scr[...], gpost_ref[...], gt_ref[...], FFN_RES)


def _ffn(x, mod, gpre, gpost, wg, wu, wd, tm=512, tf=512):
    n = x.shape[0]
    tm = min(tm, n)
    sh, sc, gt = mod
    row = pl.BlockSpec((tm, D_MODEL), lambda i, j: (i, 0))
    grp = pl.BlockSpec((tm // GROUP, 1, D_MODEL), lambda i, j: (i, 0, 0))
    vec = pl.BlockSpec((1, D_MODEL), lambda i, j: (0, 0))
    return pl.pallas_call(
        _ffn_kernel,
        out_shape=jax.ShapeDtypeStruct((n, D_MODEL), F32),
        grid=(n // tm, D_FF // tf),
        in_specs=[row, grp, grp, grp, vec, vec,
                  pl.BlockSpec((D_MODEL, tf), lambda i, j: (0, j)),
                  pl.BlockSpec((D_MODEL, tf), lambda i, j: (0, j)),
                  pl.BlockSpec((tf, D_MODEL), lambda i, j: (j, 0))],
        out_specs=row,
        scratch_shapes=[pltpu.VMEM((tm, D_MODEL), BF16), pltpu.VMEM((tm, D_MODEL), F32)],
        compiler_params=_params("arbitrary", "arbitrary"),
        name="ffn_sublayer",
    )(x, sh, sc, gt, gpre, gpost, wg, wu, wd)


AB_MAIN = ("qa", "qb", "ka", "va", "kb", "vb")
IDX_COLS = D_HEADS + LANES


def _proj_ab_kernel(x_ref, sh_ref, sc_ref, gpre_ref, wm_ref, wih_ref, wil_ref,
                    qa_ref, qb_ref, ka_ref, va_ref, kb_ref, vb_ref,
                    ka16_ref, va16_ref, kb16_ref, vb16_ref, qi_ref, misc_ref):
    h = _modulated(x_ref[...], gpre_ref[...], sh_ref[...], sc_ref[...])
    hi, lo = _split(h)
    for n, ref in enumerate((qa_ref, qb_ref)):
        ref[...] = _dot(hi, wm_ref[:, n * D_HEADS:(n + 1) * D_HEADS]).astype(BF16)
    for n, (ref, ref16) in enumerate(((ka_ref, ka16_ref), (va_ref, va16_ref),
                                      (kb_ref, kb16_ref), (vb_ref, vb16_ref))):
        z = _dot(hi, wm_ref[:, (n + 2) * D_HEADS:(n + 3) * D_HEADS])
        ref[...] = z
        ref16[...] = z.astype(BF16)
    z = _dot(hi, wih_ref[...]) + _dot(lo, wih_ref[...]) + _dot(hi, wil_ref[...])
    qi_ref[...] = z[:, :D_HEADS]
    misc_ref[...] = z[:, D_HEADS:]


def _proj_ab(x, sh, sc, gpre, wm, wih, wil, tm=256):
    n = x.shape[0]
    tm = min(tm, n)
    row = lambda w: pl.BlockSpec((tm, w), lambda i: (i, 0))
    full = lambda a: pl.BlockSpec(a.shape, lambda i: (0, 0))
    grp = pl.BlockSpec((tm // GROUP, 1, D_MODEL), lambda i: (i, 0, 0))
    sds = lambda w, dt: jax.ShapeDtypeStruct((n, w), dt)
    out_shape = ([sds(D_HEADS, BF16)] * 2 + [sds(D_HEADS, F32)] * 4 + [sds(D_HEADS, BF16)] * 4
                 + [sds(D_HEADS, F32), sds(LANES, F32)])
    out_specs = [row(D_HEADS)] * 11 + [row(LANES)]
    return pl.pallas_call(
        _proj_ab_kernel,
        out_shape=out_shape,
        grid=(n // tm,),
        in_specs=[row(D_MODEL), grp, grp, pl.BlockSpec((1, D_MODEL), lambda i: (0, 0)),
                  full(wm), full(wih), full(wil)],
        out_specs=out_specs,
        compiler_params=_params("arbitrary"),
        name="proj_ab",
    )(x, sh, sc, gpre, wm, wih, wil)


def _lane_cumsum(x):
    lane = lax.broadcasted_iota(jnp.int32, x.shape, 1)
    s = 1
    while s < LANES:
        x = x + jnp.where(lane >= s, pltpu.roll(x, s, 1), 0.0)
        s *= 2
    return x


def _logf_kernel(*refs, n_cache, n_new):
    if n_cache:
        fl_ref, bf_ref, cache_ref, logf_ref, fnew_ref, fcache_ref = refs
    else:
        fl_ref, bf_ref, logf_ref, fnew_ref = refs
    carry = jnp.zeros((N_HEADS, 1), F32)
    for blk in range(n_cache):
        c = _lane_cumsum(cache_ref[blk]) + carry
        fcache_ref[blk] = c
        carry = c[:, LANES - 1:LANES]
    for blk in range(n_new):
        lf = jax.nn.log_sigmoid(fl_ref[blk] + bf_ref[...])
        logf_ref[blk] = lf
        c = _lane_cumsum(lf) + carry
        fnew_ref[blk] = c
        carry = c[:, LANES - 1:LANES]


def _blocked_rows(a):
    b, t, h = a.shape
    return a.reshape(b, t // LANES, LANES, h).transpose(0, 1, 3, 2)


def _unblocked_rows(a):
    b, nb, h, _ = a.shape
    return a.transpose(0, 1, 3, 2).reshape(b, nb * LANES, h)


def _logf(fl, b_f, cache_logf):
    b, t, _ = fl.shape
    n_new = t // LANES
    blk = lambda nb: pl.BlockSpec((None, nb, N_HEADS, LANES), lambda i: (i, 0, 0, 0))
    sds = lambda nb: jax.ShapeDtypeStruct((b, nb, N_HEADS, LANES), F32)
    ins = [_blocked_rows(fl), b_f.reshape(N_HEADS, 1)]
    in_specs = [blk(n_new), pl.BlockSpec((N_HEADS, 1), lambda i: (0, 0))]
    out_shape = [sds(n_new), sds(n_new)]
    out_specs = [blk(n_new), blk(n_new)]
    n_cache = 0
    if cache_logf is not None:
        n_cache = cache_logf.shape[1] // LANES
        ins.append(_blocked_rows(cache_logf))
        in_specs.append(blk(n_cache))
        out_shape.append(sds(n_cache))
        out_specs.append(blk(n_cache))
    outs = pl.pallas_call(
        functools.partial(_logf_kernel, n_cache=n_cache, n_new=n_new),
        out_shape=out_shape, grid=(b,), in_specs=in_specs, out_specs=out_specs,
        compiler_params=_params("arbitrary"), name="logf_cumsum",
    )(*ins)
    return outs


N_BIAS_TILES = 4


def _t5_bucket(rel):
    nb = REL_BUCKETS // 2
    max_exact = nb // 2
    base = jnp.where(rel > 0, nb, 0)
    n = jnp.abs(rel)
    nf = jnp.maximum(n, 1).astype(F32)
    large = max_exact + (jnp.log(nf / max_exact) / math.log(REL_MAX_DIST / max_exact)
                         * (nb - max_exact)).astype(jnp.int32)
    large = jnp.minimum(large, nb - 1)
    return base + jnp.where(n < max_exact, n, large)


def _bias_kernel(tab_ref, o_ref):
    tq = o_ref.shape[2]
    row = lax.broadcasted_iota(jnp.int32, (tq, LANES), 0)
    col = lax.broadcasted_iota(jnp.int32, (tq, LANES), 1)
    for t in range(N_BIAS_TILES):
        bucket = _t5_bucket(LANES * (t - (N_BIAS_TILES - 1)) + col - row)
        for h in range(N_HEADS):
            acc = jnp.zeros((tq, LANES), F32)
            for b in range(REL_BUCKETS):
                acc = jnp.where(bucket == b, tab_ref[b, h], acc)
            o_ref[t, h] = acc


def _bias_tiles(rel_bias, tq):
    assert LANES * (N_BIAS_TILES - 1) - (LANES - 1) >= REL_MAX_DIST
    return pl.pallas_call(
        _bias_kernel,
        out_shape=jax.ShapeDtypeStruct((N_BIAS_TILES, N_HEADS, tq, LANES), F32),
        in_specs=[pl.BlockSpec(memory_space=pltpu.SMEM)],
        name="rel_bias_tiles",
    )(rel_bias)


def _sortable(x):
    b = pltpu.bitcast(x, jnp.int32)
    return b ^ ((b >> 31) & 0x7FFFFFFF)


def _pair_queries(q16):
    lane = lax.broadcasted_iota(jnp.int32, (q16.shape[0], LANES), 1)
    out = []
    for h in range(N_HEADS):
        pair = q16[:, LANES * (h // 2):LANES * (h // 2 + 1)].astype(F32) * HEAD_DIM ** -0.5
        own = (lane < HEAD_DIM) if h % 2 == 0 else (lane >= HEAD_DIM)
        out.append(jnp.where(own, pair, 0.0).astype(BF16))
    return jnp.stack(out)


def _pair_keys(x):
    x = x.astype(BF16)
    return jnp.stack([x[:, LANES * (h // 2):LANES * (h // 2 + 1)] for h in range(N_HEADS)])


def _unpair(o):
    lane = lax.broadcasted_iota(jnp.int32, o.shape[1:], 1)
    return jnp.concatenate([jnp.where(lane < HEAD_DIM, o[2 * p], o[2 * p + 1]) for p in range(N_HEADS // 2)], axis=-1)


def _row_reduce(x, op, reduce):
    acc = x[..., :LANES]
    for t in range(1, x.shape[-1] // LANES):
        acc = op(acc, x[..., t * LANES:(t + 1) * LANES])
    return reduce(acc, axis=-1, keepdims=True)


def _softmax_init(m_scr, l_scr, acc_scr):
    m_scr[...] = jnp.full(m_scr.shape, NEG, F32)
    l_scr[...] = jnp.zeros(l_scr.shape, F32)
    acc_scr[...] = jnp.zeros(acc_scr.shape, F32)


def _softmax_step(s, vp16, m_scr, l_scr, acc_scr):
    m_prev = m_scr[...]
    m_new = jnp.maximum(m_prev, _row_reduce(s, jnp.maximum, jnp.max))
    alpha = jnp.exp(m_prev - m_new)
    p = jnp.exp(s - m_new)
    l_scr[...] = alpha * l_scr[...] + _row_reduce(p, jnp.add, jnp.sum)
    acc_scr[...] = alpha * acc_scr[...] + _bmm(p.astype(BF16), vp16)
    m_scr[...] = m_new


def _softmax_scratch(tq):
    return [pltpu.VMEM((N_HEADS, tq, 1), F32), pltpu.VMEM((N_HEADS, tq, 1), F32),
            pltpu.VMEM((N_HEADS, tq, LANES), F32)]


def _dsa_kernel(*refs, tq, n_cache, past, t_new, top_k, dyn_new):
    if n_cache:
        (q_ref, qi_ref, mq_ref, cki_ref, cka_ref, cva_ref, nki_ref, nka_ref, nva_ref, bias_ref,
         o_ref, keys_ref, madd_ref, qcat_scr, m_scr, l_scr, acc_scr) = refs
    else:
        (q_ref, qi_ref, mq_ref, nki_ref, nka_ref, nva_ref, bias_ref,
         o_ref, keys_ref, madd_ref, qcat_scr, m_scr, l_scr, acc_scr) = refs
        cki_ref = cka_ref = cva_ref = None
    i = pl.program_id(1)
    n_new = (i + 1) * (tq // LANES) if dyn_new else 1
    n_blk = n_cache + n_new
    qpos0 = past + i * tq
    qpos = qpos0 + lax.broadcasted_iota(jnp.int32, (tq, 1), 0)
    col = lax.broadcasted_iota(jnp.int32, (tq, LANES), 1)
    n_keys = past + t_new

    qi = qi_ref[...]
    for h in range(N_HEADS):
        x = qi[:, h * IDX_DIM:(h + 1) * IDX_DIM]
        hi = x.astype(BF16).astype(F32)
        qcat_scr[h * tq:(h + 1) * tq, :] = jnp.concatenate([hi, hi, x - hi, jnp.zeros_like(x)], axis=-1).astype(BF16)
    wi = jnp.stack([mq_ref[:, IDX_DIM + h:IDX_DIM + h + 1] for h in range(N_HEADS)])
    score_scale = IDX_DIM ** -0.5 * N_HEADS ** -0.5

    def score_block(ki, kpos0, slot):
        hi = ki.astype(BF16).astype(F32)
        kcat = jnp.concatenate([hi, ki - hi, hi, ki - hi], axis=-1).astype(BF16)
        s = _dot_nt(qcat_scr[...], kcat).reshape(N_HEADS, tq, LANES)
        acc = jnp.sum(jnp.maximum(s, 0.0) * wi, axis=0)
        kpos = kpos0 + col
        adm = jnp.logical_and((kpos // CHUNK) <= (qpos // CHUNK), kpos < n_keys)
        keys_ref[slot] = jnp.where(adm, _sortable(acc * score_scale), INT_MIN)

    if n_cache:
        def cache_scores(j, _):
            r0 = pl.multiple_of(j * LANES, LANES)
            score_block(cki_ref[pl.ds(r0, LANES), :], j * LANES, j)
            return 0
        lax.fori_loop(0, n_cache, cache_scores, 0)

    def new_scores(j, _):
        r0 = pl.multiple_of(j * LANES, LANES)
        score_block(nki_ref[pl.ds(r0, LANES), :IDX_DIM], past + j * LANES, n_cache + j)
        return 0
    lax.fori_loop(0, n_new, new_scores, 0)

    def count(pred):
        def body(j, c):
            return c + jnp.where(pred(keys_ref[j]), 1.0, 0.0)
        c = lax.fori_loop(0, n_blk, body, jnp.zeros((tq, LANES), F32))
        return jnp.sum(c, axis=-1, keepdims=True)

    def search(b, ans):
        cand = ans ^ lax.shift_left(jnp.int32(1), jnp.int32(31) - b.astype(jnp.int32))
        cand_b = jnp.broadcast_to(cand, (tq, LANES))
        cnt = count(lambda k: k >= cand_b)
        return jnp.where(cnt >= top_k, cand, ans)

    kth = lax.fori_loop(0, 32, search, jnp.full((tq, 1), INT_MIN, jnp.int32))
    kth_b = jnp.broadcast_to(kth, (tq, LANES))
    n_gt = count(lambda k: k > kth_b)
    quota = top_k - n_gt
    tri = (lax.broadcasted_iota(jnp.int32, (LANES, LANES), 0)
           <= lax.broadcasted_iota(jnp.int32, (LANES, LANES), 1)).astype(BF16)

    def build_mask(j, seen):
        k = keys_ref[j]
        eq = jnp.where(k == kth_b, jnp.where(k != INT_MIN, 1.0, 0.0), 0.0)
        rank = _dot(eq.astype(BF16), tri) + seen
        sel = jnp.where(k > kth_b, 1.0, jnp.where(rank <= quota, eq, 0.0))
        madd_ref[j] = (1.0 - sel) * NEG
        return seen + jnp.sum(eq, axis=-1, keepdims=True)
    lax.fori_loop(0, n_blk, build_mask, jnp.zeros((tq, 1), F32))

    qm = _pair_queries(q_ref[...])
    _softmax_init(m_scr, l_scr, acc_scr)

    def step(k, v, slot, kblk):
        tile = jnp.clip(kblk - qpos0 // LANES + (N_BIAS_TILES - 1), 0, N_BIAS_TILES - 1)
        s = _bmm_nt(qm, _pair_keys(k)) + bias_ref[tile] + madd_ref[slot][None]
        _softmax_step(s, _pair_keys(v), m_scr, l_scr, acc_scr)

    if n_cache:
        def cache_step(j, _):
            rows = pl.ds(pl.multiple_of(j * LANES, LANES), LANES)
            step(cka_ref[rows, :], cva_ref[rows, :], j, j)
            return 0
        lax.fori_loop(0, n_cache, cache_step, 0)

    def new_step(j, _):
        rows = pl.ds(pl.multiple_of(j * LANES, LANES), LANES)
        step(nka_ref[rows, :], nva_ref[rows, :], n_cache + j, past // LANES + j)
        return 0
    lax.fori_loop(0, n_new, new_step, 0)
    o_ref[...] = _unpair(acc_scr[...] / l_scr[...]).astype(BF16)


def _dsa(q, qi, misc, ka16, va16, bias, cache, *, batch, t_q, tq):
    t_kv = ka16.shape[0] // batch
    nq = t_q // tq
    past = 0 if cache is None else cache[0].shape[1]
    n_cache = past // LANES
    n_keys = past + t_q
    top_k = min(IDX_TOPK_MAX, n_keys // 4)
    assert tq == LANES or nq == 1
    qrow = lambda w: pl.BlockSpec((tq, w), lambda b, i: (b * (t_kv // tq) + i, 0))
    krow = lambda w: pl.BlockSpec((t_kv, w), lambda b, i: (b, 0))
    crow = lambda w: pl.BlockSpec((None, past, w), lambda b, i: (b, 0, 0))
    ins = [q, qi, misc]
    in_specs = [qrow(D_HEADS), qrow(D_HEADS), qrow(LANES)]
    if cache is not None:
        ins += list(cache)
        in_specs += [crow(IDX_DIM), crow(D_HEADS), crow(D_HEADS)]
    ins += [misc, ka16, va16, bias]
    in_specs += [krow(LANES), krow(D_HEADS), krow(D_HEADS),
                 pl.BlockSpec(bias.shape, lambda b, i: (0, 0, 0, 0))]
    n_blk_max = n_cache + t_kv // LANES
    return pl.pallas_call(
        functools.partial(_dsa_kernel, tq=tq, n_cache=n_cache, past=past, t_new=t_q, top_k=top_k,
                          dyn_new=cache is None),
        out_shape=jax.ShapeDtypeStruct((batch * t_kv, D_HEADS), BF16),
        grid=(batch, nq),
        in_specs=in_specs,
        out_specs=qrow(D_HEADS),
        scratch_shapes=[pltpu.VMEM((n_blk_max, tq, LANES), jnp.int32),
                        pltpu.VMEM((n_blk_max, tq, LANES), F32),
                        pltpu.VMEM((N_HEADS * tq, 4 * IDX_DIM), BF16)] + _softmax_scratch(tq),
        compiler_params=_params("arbitrary", "arbitrary"),
        name="dsa_attention",
    )(*ins)


def _fox_kernel(*refs, tq, tk, tkc, n_cache, past, dyn_new):
    if n_cache:
        q_ref, fq_ref, ck_ref, cv_ref, cf_ref, nk_ref, nv_ref, nf_ref, o_ref, m_scr, l_scr, acc_scr = refs
    else:
        q_ref, fq_ref, nk_ref, nv_ref, nf_ref, o_ref, m_scr, l_scr, acc_scr = refs
        ck_ref = cv_ref = cf_ref = None
    i = pl.program_id(1)
    n_new = (i + 1) * (tq // tk) if dyn_new else 1
    n_unmasked = i * (tq // tk) if dyn_new else 0
    qpos = past + i * tq + lax.broadcasted_iota(jnp.int32, (1, tq, 1), 1)
    col = lax.broadcasted_iota(jnp.int32, (1, 1, tk), 2)

    qm = _pair_queries(q_ref[...])
    fq = jnp.stack([fq_ref[:, h:h + 1] for h in range(N_HEADS)])
    _softmax_init(m_scr, l_scr, acc_scr)

    def step(k, v, fk, kpos0=None):
        s = _bmm_nt(qm, _pair_keys(k)) + fq - fk
        if kpos0 is not None:
            s = jnp.where(kpos0 + col <= qpos, s, NEG)
        _softmax_step(s, _pair_keys(v), m_scr, l_scr, acc_scr)

    if n_cache:
        def cache_step(j, _):
            rows = pl.ds(pl.multiple_of(j * tkc, tkc), tkc)
            step(ck_ref[rows, :], cv_ref[rows, :], cf_ref[j])
            return 0
        lax.fori_loop(0, n_cache, cache_step, 0)

    def new_step(j, _, masked):
        rows = pl.ds(pl.multiple_of(j * tk, tk), tk)
        step(nk_ref[rows, :], nv_ref[rows, :], nf_ref[j], past + j * tk if masked else None)
        return 0
    lax.fori_loop(0, n_unmasked, functools.partial(new_step, masked=False), 0)
    lax.fori_loop(n_unmasked, n_new, functools.partial(new_step, masked=True), 0)
    o_ref[...] = _unpair(acc_scr[...] / l_scr[...]).astype(BF16)


def _key_rows(f, tk):
    b, t, h = f.shape
    return f.reshape(b, t // tk, tk, h).transpose(0, 1, 3, 2)[:, :, :, None, :]


def _fox(q, fq, kb16, vb16, f_new, cache, *, batch, t_q, tq, tk, tkc):
    t_kv = kb16.shape[0] // batch
    nq = t_q // tq
    past = 0 if cache is None else cache[0].shape[1]
    n_cache = past // tkc
    qrow = lambda w: pl.BlockSpec((tq, w), lambda b, i: (b * (t_kv // tq) + i, 0))
    krow = lambda w: pl.BlockSpec((t_kv, w), lambda b, i: (b, 0))
    crow = lambda w: pl.BlockSpec((None, past, w), lambda b, i: (b, 0, 0))
    fblk = lambda nb, w: pl.BlockSpec((None, nb, N_HEADS, 1, w), lambda b, i: (b, 0, 0, 0, 0))
    ins = [q, fq]
    in_specs = [qrow(D_HEADS), qrow(N_HEADS)]
    if cache is not None:
        ins += [cache[0], cache[1], _key_rows(cache[2], tkc)]
        in_specs += [crow(D_HEADS), crow(D_HEADS), fblk(n_cache, tkc)]
    ins += [kb16, vb16, _key_rows(f_new, tk)]
    in_specs += [krow(D_HEADS), krow(D_HEADS), fblk(t_kv // tk, tk)]
    return pl.pallas_call(
        functools.partial(_fox_kernel, tq=tq, tk=tk, tkc=tkc, n_cache=n_cache, past=past, dyn_new=cache is None),
        out_shape=jax.ShapeDtypeStruct((batch * t_kv, D_HEADS), BF16),
        grid=(batch, nq),
        in_specs=in_specs,
        out_specs=qrow(D_HEADS),
        scratch_shapes=_softmax_scratch(tq),
        compiler_params=_params("arbitrary", "arbitrary"),
        name="fox_attention",
    )(*ins)


def _out_ab_kernel(oa_ref, ob_ref, w_ref, x_ref, gt_ref, gpost_ref, o_ref):
    y = _dot(oa_ref[...], w_ref[:D_HEADS, :]) + _dot(ob_ref[...], w_ref[D_HEADS:, :])
    o_ref[...] = _residual(x_ref[...], y, gpost_ref[...], gt_ref[...], 1.0)


def _out_ab(oa, ob, w, x, gt, gpost, *, batch, t_q, tm):
    t_kv = oa.shape[0] // batch
    nt = t_q // tm
    arow = pl.BlockSpec((tm, D_HEADS), lambda b, i: (b * (t_kv // tm) + i, 0))
    xrow = pl.BlockSpec((tm, D_MODEL), lambda b, i: (b * nt + i, 0))
    return pl.pallas_call(
        _out_ab_kernel,
        out_shape=jax.ShapeDtypeStruct(x.shape, F32),
        grid=(batch, nt),
        in_specs=[arow, arow, pl.BlockSpec(w.shape, lambda b, i: (0, 0)), xrow,
                  pl.BlockSpec((tm // GROUP, 1, D_MODEL), lambda b, i: (b * nt + i, 0, 0)),
                  pl.BlockSpec((1, D_MODEL), lambda b, i: (0, 0))],
        out_specs=xrow,
        compiler_params=_params("arbitrary", "arbitrary"),
        name="out_ab",
    )(oa, ob, w, x, gt, gpost)


GDN_MISC0 = C_QKV + N_HEADS * DK_C


def _proj_gdn_kernel(x_ref, sh_ref, sc_ref, gpre_ref, w_ref, cw_ref, prev_ref, alog_ref, dtb_ref,
                     q_ref, k_ref, v_ref, gz_ref, misc_ref, tail_ref, zbuf):
    tm = x_ref.shape[0]
    halo = CONV_W - 1
    base = 8

    @pl.when(pl.program_id(1) == 0)
    def _():
        zbuf[base - halo:base, :] = prev_ref[...]

    h = _modulated(x_ref[...], gpre_ref[...], sh_ref[...], sc_ref[...]).astype(BF16)
    zbuf[base:base + tm, :] = _dot(h, w_ref[:, :C_QKV])
    gz_ref[...] = _dot(h, w_ref[:, C_QKV:GDN_MISC0])
    zm = _dot(h, w_ref[:, GDN_MISC0:])
    lane = lax.broadcasted_iota(jnp.int32, zm.shape, 1)
    beta = jax.nn.sigmoid(zm)
    g = -jnp.exp(alog_ref[...]) * jax.nn.softplus(zm + dtb_ref[...])
    misc_ref[...] = jnp.where(lane < N_HEADS, beta, g)

    for part, ref in enumerate((q_ref, k_ref, v_ref)):
        for hd in range(N_HEADS):
            c0 = part * N_HEADS * DK_C + hd * DK_C
            cols = slice(c0, c0 + DK_C)
            conv = zbuf[base:base + tm, cols] * cw_ref[halo:halo + 1, cols]
            for j in range(halo):
                conv = conv + zbuf[base - halo + j:base - halo + j + tm, cols] * cw_ref[j:j + 1, cols]
            a = _silu(conv)
            if part < 2:
                a = a * lax.rsqrt(jnp.sum(a * a, axis=-1, keepdims=True) + EPS)
                if part == 0:
                    a = a * DK_C ** -0.5
            ref[:, hd * DK_C:(hd + 1) * DK_C] = a
    tail = zbuf[base + tm - halo:base + tm, :]
    tail_ref[...] = tail
    zbuf[base - halo:base, :] = tail


def _proj_gdn(x, sh, sc, gpre, w, conv_w, conv_prev, alog_row, dtb_row, *, batch, tm):
    n = x.shape[0]
    t = n // batch
    nt = t // tm
    row = lambda wd: pl.BlockSpec((tm, wd), lambda b, i: (b * nt + i, 0))
    full = lambda a: pl.BlockSpec(a.shape, lambda b, i: (0,) * a.ndim)
    sds = lambda wd: jax.ShapeDtypeStruct((n, wd), F32)
    hd = N_HEADS * DK_C
    tail = pl.BlockSpec((None, CONV_W - 1, C_QKV), lambda b, i: (b, 0, 0))
    return pl.pallas_call(
        _proj_gdn_kernel,
        out_shape=[sds(hd), sds(hd), sds(hd), sds(hd), sds(LANES),
                   jax.ShapeDtypeStruct((batch, CONV_W - 1, C_QKV), F32)],
        grid=(batch, nt),
        in_specs=[row(D_MODEL),
                  pl.BlockSpec((tm // GROUP, 1, D_MODEL), lambda b, i: (b * nt + i, 0, 0)),
                  pl.BlockSpec((tm // GROUP, 1, D_MODEL), lambda b, i: (b * nt + i, 0, 0)),
                  pl.BlockSpec((1, D_MODEL), lambda b, i: (0, 0)),
                  full(w), full(conv_w), tail, full(alog_row), full(dtb_row)],
        out_specs=[row(hd), row(hd), row(hd), row(hd), row(LANES), tail],
        scratch_shapes=[pltpu.VMEM((8 + tm, C_QKV), F32)],
        compiler_params=_params("arbitrary", "arbitrary"),
        name="proj_gdn",
    )(x, sh, sc, gpre, w, conv_w, conv_prev, alog_row, dtb_row)


def _chunk_cumsum_kernel(g_ref, o_ref, *, chunk):
    x = g_ref[...]
    lane = lax.broadcasted_iota(jnp.int32, x.shape, 1) % chunk
    s = 1
    while s < chunk:
        x = x + jnp.where(lane >= s, pltpu.roll(x, s, 1), 0.0)
        s *= 2
    o_ref[...] = x


def _chunk_cumsum(g_rows, chunk):
    assert LANES % chunk == 0
    tiles = g_rows.reshape(-1, LANES)
    return pl.pallas_call(
        functools.partial(_chunk_cumsum_kernel, chunk=chunk),
        out_shape=jax.ShapeDtypeStruct(tiles.shape, F32),
        name="gdn_chunk_cumsum",
    )(tiles).reshape(g_rows.shape)


def _gdn_kernel(q_ref, k_ref, v_ref, col_ref, row_ref, s0_ref, o_ref, s_ref):
    c = q_ref.shape[0]

    @pl.when(pl.program_id(1) == 0)
    def _():
        s_ref[...] = s0_ref[...]

    heads = lambda ref: jnp.stack([ref[:, h * DK_C:(h + 1) * DK_C] for h in range(N_HEADS)])
    q, k, v = heads(q_ref), heads(k_ref), heads(v_ref)
    beta = jnp.stack([col_ref[:, h:h + 1] for h in range(N_HEADS)])
    gc = jnp.stack([col_ref[:, N_HEADS + h:N_HEADS + h + 1] for h in range(N_HEADS)])
    gc_row = row_ref[...]
    gc_last = gc[:, c - 1:c, :]
    ri = lax.broadcasted_iota(jnp.int32, (1, c, c), 1)
    ci = lax.broadcasted_iota(jnp.int32, (1, c, c), 2)
    causal = ri >= ci
    strict = ri > ci
    eye = jnp.where(ri == ci, 1.0, 0.0)
    decay = jnp.exp(jnp.where(causal, gc - gc_row, NEG))
    k16 = k.astype(BF16)
    a = jnp.where(strict, beta * _bmm_nt(k16, k16) * decay, 0.0)
    inv = eye - a
    p = a
    n = 2
    while n < c:
        p = _dot3(p, p, _bmm)
        inv = inv + _dot3(inv, p, _bmm)
        n *= 2
    sol = _dot3(inv, jnp.concatenate([v * beta, k * (beta * jnp.exp(gc))], axis=-1), _bmm)
    u, w = sol[:, :, :DK_C], sol[:, :, DK_C:]
    qk = jnp.where(causal, _bmm_nt(q.astype(BF16), k16) * decay, 0.0)
    q_dec = (q * jnp.exp(gc)).astype(BF16)
    k_dec = k * jnp.exp(gc_last - gc)
    k_dec_t = jnp.stack([k_dec[h].T for h in range(N_HEADS)]).astype(BF16)
    s = s_ref[...]
    s16 = s.astype(BF16)
    v16 = (u - _bmm(w.astype(BF16), s16)).astype(BF16)
    o = _bmm(q_dec, s16) + _bmm(qk.astype(BF16), v16)
    for h in range(N_HEADS):
        o_ref[:, h * DK_C:(h + 1) * DK_C] = o[h]
    s_ref[...] = s * jnp.exp(gc_last) + _bmm(k_dec_t, v16)


def _gdn(q, k, v, col, row, s0, *, batch, chunk):
    n = q.shape[0]
    nc = n // batch // chunk
    hd = N_HEADS * DK_C
    blk = lambda w: pl.BlockSpec((chunk, w), lambda b, i: (b * nc + i, 0))
    state = pl.BlockSpec((None, N_HEADS, DK_C, DK_C), lambda b, i: (b, 0, 0, 0))
    return pl.pallas_call(
        _gdn_kernel,
        out_shape=[jax.ShapeDtypeStruct((n, hd), F32), jax.ShapeDtypeStruct(s0.shape, F32)],
        grid=(batch, nc),
        in_specs=[blk(hd), blk(hd), blk(hd), blk(2 * N_HEADS),
                  pl.BlockSpec((None, N_HEADS, 1, chunk), lambda b, i: (b * nc + i, 0, 0, 0)), state],
        out_specs=[blk(hd), state],
        compiler_params=_params("arbitrary", "arbitrary"),
        name="gated_delta_rule",
    )(q, k, v, col, row, s0)


def _out_gdn_kernel(o_ref, gz_ref, nw_ref, w_ref, x_ref, gt_ref, gpost_ref, out_ref):
    parts = []
    for h in range(N_HEADS):
        cols = slice(h * DK_C, (h + 1) * DK_C)
        o = o_ref[:, cols]
        o = o * lax.rsqrt(jnp.mean(o * o, axis=-1, keepdims=True) + EPS) * nw_ref[...]
        parts.append((o * _silu(gz_ref[:, cols])).astype(BF16))
    y = _dot(jnp.concatenate(parts, axis=-1), w_ref[...])
    out_ref[...] = _residual(x_ref[...], y, gpost_ref[...], gt_ref[...], 1.0)


def _out_gdn(o, gz, norm_w, w, x, gt, gpost, tm=256):
    n = x.shape[0]
    tm = min(tm, n)
    row = pl.BlockSpec((tm, D_MODEL), lambda i: (i, 0))
    return pl.pallas_call(
        _out_gdn_kernel,
        out_shape=jax.ShapeDtypeStruct(x.shape, F32),
        grid=(n // tm,),
        in_specs=[row, row, pl.BlockSpec((1, DK_C), lambda i: (0, 0)), pl.BlockSpec(w.shape, lambda i: (0, 0)),
                  row, pl.BlockSpec((tm // GROUP, 1, D_MODEL), lambda i: (i, 0, 0)),
                  pl.BlockSpec((1, D_MODEL), lambda i: (0, 0))],
        out_specs=row,
        compiler_params=_params("arbitrary"),
        name="out_gdn",
    )(o, gz, norm_w, w, x, gt, gpost)


def _mixer_ab(x, mod, w, past, *, batch, t, bias):
    sh, sc, gt = mod
    (qa, qb, ka, va, kb, vb, ka16, va16, kb16, vb16, qi, misc) = _proj_ab(
        x, sh, sc, w["pre"], w["ab_main"], w["ab_idx_hi"], w["ab_idx_lo"])
    ki = misc[:, :IDX_DIM]
    fl = misc[:, IDX_DIM + N_HEADS:IDX_DIM + 2 * N_HEADS].reshape(batch, t, N_HEADS)
    t_kv = -(-t // LANES) * LANES
    pad_rows = lambda a: a if t_kv == t else jnp.pad(
        a.reshape(batch, t, -1), ((0, 0), (0, t_kv - t), (0, 0))).reshape(batch * t_kv, -1)
    fl_p = fl if t_kv == t else jnp.pad(fl, ((0, 0), (0, t_kv - t), (0, 0)))
    if past is None:
        logf_b, f_new = _logf(fl_p, w["ab_b_f"], None)
        cache_a = cache_b = None
    else:
        cak, cav, caik, cbk, cbv, cblf = past
        p = cak.shape[1]
        logf_b, f_new, f_cache = _logf(fl_p, w["ab_b_f"], cblf)
        cache_a = (caik, cak.reshape(batch, p, D_HEADS), cav.reshape(batch, p, D_HEADS))
        cache_b = (cbk.reshape(batch, p, D_HEADS), cbv.reshape(batch, p, D_HEADS), _unblocked_rows(f_cache))
    logf = _unblocked_rows(logf_b)[:, :t]
    f_new = _unblocked_rows(f_new)
    o_a = _dsa(pad_rows(qa), pad_rows(qi), pad_rows(misc), pad_rows(ka16), pad_rows(va16), bias, cache_a,
               batch=batch, t_q=t, tq=min(LANES, t))
    o_b = _fox(pad_rows(qb), f_new.reshape(batch * t_kv, N_HEADS), pad_rows(kb16), pad_rows(vb16), f_new, cache_b,
               batch=batch, t_q=t, tq=min(4 * LANES, t), tk=LANES, tkc=LANES)
    x = _out_ab(o_a, o_b, w["ab_w_out"], x, gt, w["post"], batch=batch, t_q=t, tm=min(256, t))
    hd = (batch, t, N_HEADS, HEAD_DIM)
    state = (ka.reshape(hd), va.reshape(hd), ki.reshape(batch, t, IDX_DIM), kb.reshape(hd), vb.reshape(hd), logf)
    return x, state


def _mixer_gdn(x, mod, w, past, *, batch, t):
    sh, sc, gt = mod
    if past is None:
        conv_prev = jnp.zeros((batch, CONV_W - 1, C_QKV), F32)
        s0 = jnp.zeros((batch, N_HEADS, DK_C, DK_C), F32)
    else:
        conv_prev, s0 = past
    q, k, v, gz, misc, new_conv = _proj_gdn(x, sh, sc, w["pre"], w["gdn_w_in"], w["gdn_conv_w"], conv_prev,
                                            w["gdn_alog_row"], w["gdn_dtb_row"], batch=batch, tm=min(256, t))
    chunk = min(t, CHUNK)
    n = batch * t
    gc_rows = _chunk_cumsum(misc[:, N_HEADS:2 * N_HEADS].T, chunk)
    col = jnp.concatenate([misc[:, :N_HEADS], gc_rows.T], axis=-1)
    row = gc_rows.reshape(N_HEADS, n // chunk, chunk).transpose(1, 0, 2)[:, :, None, :]
    o, s_new = _gdn(q, k, v, col, row, s0, batch=batch, chunk=chunk)
    x = _out_gdn(o, gz, w["gdn_norm_w"], w["gdn_w_out"], x, gt, w["post"])
    return x, (new_conv, s_new)


def _forward(x, mod_all, w, past, *, batch, t):
    new_state = []
    groups = lambda v: jnp.repeat(v, t // GROUP, axis=0)[:, None, :]
    for l in range(mod_all.shape[0]):
        m = mod_all[l].reshape(batch, 3, 3, D_MODEL)
        mod = [[groups(m[:, s, j]) for j in range(3)] for s in range(3)]
        pre = lambda s: w["norm_pre"][l, s][None, :]
        post = lambda s: w["norm_post"][l, s][None, :]
        x = _ffn(x, mod[0], pre(0), post(0), w["ffn_w_gate"][l, 0], w["ffn_w_up"][l, 0], w["ffn_w_down"][l, 0])
        wl = dict(w, pre=pre(1), post=post(1))
        if l % 2 == 0:
            x, st = _mixer_ab(x, mod[1], wl, None if past is None else past[:6], batch=batch, t=t,
                              bias=w["bias_tiles"][min(LANES, t)])
        else:
            x, st = _mixer_gdn(x, mod[1], wl, None if past is None else past[6:], batch=batch, t=t)
        new_state.extend(st)
        x = _ffn(x, mod[2], pre(2), post(2), w["ffn_w_gate"][l, 1], w["ffn_w_up"][l, 1], w["ffn_w_down"][l, 1])
    return x, new_state


def _prepare_weights(rel_bias, ab_w_in, ab_b_f, ab_w_out, gdn_w_in, gdn_conv_w, gdn_A_log, gdn_dt_bias,
                     gdn_norm_w, gdn_w_out, norm_pre, norm_post, ffn_w_gate, ffn_w_up, ffn_w_down, tqs):
    o = [0]
    for s in (D_HEADS, D_HEADS, D_HEADS, D_HEADS, IDX_DIM, N_HEADS, D_HEADS, D_HEADS, D_HEADS, N_HEADS):
        o.append(o[-1] + s)
    cols = lambda i: ab_w_in[:, o[i]:o[i + 1]]
    qa, ka, va, qi, ki, wi, qb, kb, vb, fl = (cols(i) for i in range(10))
    idx = jnp.concatenate([qi, ki, wi, fl, jnp.zeros((D_MODEL, IDX_COLS - D_HEADS - IDX_DIM - 2 * N_HEADS), F32)], 1)
    idx_hi = idx.astype(BF16)
    gdn_in = jnp.concatenate([gdn_w_in, jnp.zeros((D_MODEL, LANES - 2 * N_HEADS), F32)], 1)
    lane_row = lambda v: jnp.concatenate([jnp.zeros((N_HEADS,), F32), v.astype(F32),
                                          jnp.zeros((LANES - 2 * N_HEADS,), F32)])[None, :]
    return dict(
        ab_main=jnp.concatenate([qa, qb, ka, va, kb, vb], 1).astype(BF16),
        ab_idx_hi=idx_hi, ab_idx_lo=(idx - idx_hi.astype(F32)).astype(BF16),
        ab_b_f=ab_b_f, ab_w_out=ab_w_out.astype(BF16),
        gdn_w_in=gdn_in.astype(BF16), gdn_conv_w=gdn_conv_w,
        gdn_alog_row=lane_row(gdn_A_log), gdn_dtb_row=lane_row(gdn_dt_bias),
        gdn_norm_w=gdn_norm_w[None, :], gdn_w_out=gdn_w_out.astype(BF16),
        norm_pre=norm_pre, norm_post=norm_post,
        ffn_w_gate=ffn_w_gate.astype(BF16), ffn_w_up=ffn_w_up.astype(BF16), ffn_w_down=ffn_w_down.astype(BF16),
        bias_tiles={tq: _bias_tiles(rel_bias, tq) for tq in tqs},
    )


def kernel(x_prompt, x_sample, cache_a_k, cache_a_v, cache_a_idx_k, cache_b_k, cache_b_v, cache_b_logf,
           state_c_conv, state_c_rec, c_prompt, c_sample, rel_bias, ab_w_in, ab_b_f, ab_w_out,
           gdn_w_in, gdn_conv_w, gdn_A_log, gdn_dt_bias, gdn_norm_w, gdn_w_out, norm_pre, norm_post,
           ada_w, ada_b, ffn_w_gate, ffn_w_up, ffn_w_down):
    bp, tp, _ = x_prompt.shape
    bs, ts, _ = x_sample.shape
    w = _prepare_weights(rel_bias, ab_w_in, ab_b_f, ab_w_out, gdn_w_in, gdn_conv_w, gdn_A_log, gdn_dt_bias,
                         gdn_norm_w, gdn_w_out, norm_pre, norm_post, ffn_w_gate, ffn_w_up, ffn_w_down,
                         tqs={min(LANES, tp), min(LANES, ts)})
    mod = _ada(jnp.concatenate([c_prompt, c_sample], 0), ada_w, ada_b)
    y_p, st_p = _forward(x_prompt.reshape(bp * tp, D_MODEL), mod[:, :bp], w, None, batch=bp, t=tp)
    past = (cache_a_k, cache_a_v, cache_a_idx_k, cache_b_k, cache_b_v, cache_b_logf, state_c_conv, state_c_rec)
    y_s, st_s = _forward(x_sample.reshape(bs * ts, D_MODEL), mod[:, bp:], w, past, batch=bs, t=ts)
    return (y_p.reshape(bp, tp, D_MODEL), y_s.reshape(bs, ts, D_MODEL), *st_p, *st_s)
```

```python
import functools
import math

import jax
import jax.numpy as jnp
from jax import lax
from jax.experimental import pallas as pl
from jax.experimental.pallas import tpu as pltpu

D_MODEL = 1024
D_FF = 4 * D_MODEL
CHUNK = 64
HEAD_DIM = 64
N_HEADS = 8
D_HEADS = N_HEADS * HEAD_DIM
IDX_DIM = 64
IDX_TOPK_MAX = 256
DK_C = 128
CONV_W = 4
C_QKV = 3 * N_HEADS * DK_C
REL_BUCKETS = 32
REL_MAX_DIST = 256
FFN_RES = 0.5
EPS = 1e-6

GROUP = 32
LANES = 128
NEG = -1e30
INT_MIN = -(2 ** 31)
VMEM_LIMIT = 56 * 1024 * 1024

F32 = jnp.float32
BF16 = jnp.bfloat16


def _params(*sem):
    return pltpu.CompilerParams(dimension_semantics=sem, vmem_limit_bytes=VMEM_LIMIT)


def _dot(a, b):
    return jnp.dot(a, b, preferred_element_type=F32)


def _dot_nt(a, b):
    return lax.dot_general(a, b, (((1,), (1,)), ((), ())), preferred_element_type=F32)


def _bmm(a, b):
    return lax.dot_general(a, b, (((2,), (1,)), ((0,), (0,))), preferred_element_type=F32)


def _bmm_nt(a, b):
    return lax.dot_general(a, b, (((2,), (2,)), ((0,), (0,))), preferred_element_type=F32)


def _split(x):
    hi = x.astype(BF16)
    lo = (x - hi.astype(F32)).astype(BF16)
    return hi, lo


def _dot3(a, b, mm=_dot):
    ah, al = _split(a)
    bh, bl = _split(b)
    return mm(ah, bh) + mm(al, bh) + mm(ah, bl)


def _silu(x):
    return x * jax.nn.sigmoid(x)


def _modulated(x, gpre, sh, sc):
    r = lax.rsqrt(jnp.mean(x * x, axis=-1, keepdims=True) + EPS)
    y = (x * r * gpre).reshape(sh.shape[0], GROUP, x.shape[-1])
    return (y * (1.0 + sc) + sh).reshape(x.shape)


def _residual(x, y, gpost, gate, weight):
    r = lax.rsqrt(jnp.mean(y * y, axis=-1, keepdims=True) + EPS)
    yn = (y * r * gpost).reshape(gate.shape[0], GROUP, x.shape[-1])
    return x + (weight * gate * yn).reshape(x.shape)


def _ada_kernel(c_ref, w_ref, b_ref, o_ref):
    o_ref[...] = _dot3(_silu(c_ref[...]), w_ref[...]) + b_ref[...]


def _ada(c_all, ada_w, ada_b):
    depth, _, n_out = ada_w.shape
    nb = c_all.shape[0]
    tn = 1024
    return pl.pallas_call(
        _ada_kernel,
        out_shape=jax.ShapeDtypeStruct((depth, nb, n_out), F32),
        grid=(depth, n_out // tn),
        in_specs=[pl.BlockSpec((nb, D_MODEL), lambda l, j: (0, 0)),
                  pl.BlockSpec((None, D_MODEL, tn), lambda l, j: (l, 0, j)),
                  pl.BlockSpec((None, 1, tn), lambda l, j: (l, 0, j))],
        out_specs=pl.BlockSpec((None, nb, tn), lambda l, j: (l, 0, j)),
        compiler_params=_params("arbitrary", "arbitrary"),
        name="ada_mod",
    )(c_all, ada_w, ada_b.reshape(depth, 1, n_out))


def _ffn_kernel(x_ref, sh_ref, sc_ref, gt_ref, gpre_ref, gpost_ref, wg_ref, wu_ref, wd_ref, o_ref,
                h_scr, acc_scr):
    j = pl.program_id(1)

    @pl.when(j == 0)
    def _():
        h_scr[...] = _modulated(x_ref[...], gpre_ref[...], sh_ref[...], sc_ref[...]).astype(BF16)
        acc_scr[...] = jnp.zeros_like(acc_scr)

    h = h_scr[...]
    a = _dot(h, wg_ref[...])
    b = _dot(h, wu_ref[...])
    acc_scr[...] += _dot((_silu(a) * b).astype(BF16), wd_ref[...])

    @pl.when(j == pl.num_programs(1) - 1)
    def _():
        o_ref[...] = _residual(x_ref[...], acc_scr[...], gpost_ref[...], gt_ref[...], FFN_RES)


def _ffn(x, mod, gpre, gpost, wg, wu, wd, tm=512, tf=512):
    n = x.shape[0]
    tm = min(tm, n)
    sh, sc, gt = mod
    row = pl.BlockSpec((tm, D_MODEL), lambda i, j: (i, 0))
    grp = pl.BlockSpec((tm // GROUP, 1, D_MODEL), lambda i, j: (i, 0, 0))
    vec = pl.BlockSpec((1, D_MODEL), lambda i, j: (0, 0))
    return pl.pallas_call(
        _ffn_kernel,
        out_shape=jax.ShapeDtypeStruct((n, D_MODEL), F32),
        grid=(n // tm, D_FF // tf),
        in_specs=[row, grp, grp, grp, vec, vec,
                  pl.BlockSpec((D_MODEL, tf), lambda i, j: (0, j)),
                  pl.BlockSpec((D_MODEL, tf), lambda i, j: (0, j)),
                  pl.BlockSpec((tf, D_MODEL), lambda i, j: (j, 0))],
        out_specs=row,
        scratch_shapes=[pltpu.VMEM((tm, D_MODEL), BF16), pltpu.VMEM((tm, D_MODEL), F32)],
        compiler_params=_params("arbitrary", "arbitrary"),
        name="ffn_sublayer",
    )(x, sh, sc, gt, gpre, gpost, wg, wu, wd)


AB_MAIN = ("qa", "qb", "ka", "va", "kb", "vb")
IDX_COLS = D_HEADS + LANES


def _proj_ab_kernel(x_ref, sh_ref, sc_ref, gpre_ref, wm_ref, wih_ref, wil_ref,
                    qa_ref, qb_ref, ka_ref, va_ref, kb_ref, vb_ref,
                    ka16_ref, va16_ref, kb16_ref, vb16_ref, qi_ref, misc_ref, kcat_ref):
    h = _modulated(x_ref[...], gpre_ref[...], sh_ref[...], sc_ref[...])
    hi, lo = _split(h)
    for n, ref in enumerate((qa_ref, qb_ref)):
        ref[...] = _dot(hi, wm_ref[:, n * D_HEADS:(n + 1) * D_HEADS]).astype(BF16)
    for n, (ref, ref16) in enumerate(((ka_ref, ka16_ref), (va_ref, va16_ref),
                                      (kb_ref, kb16_ref), (vb_ref, vb16_ref))):
        z = _dot(hi, wm_ref[:, (n + 2) * D_HEADS:(n + 3) * D_HEADS])
        ref[...] = z
        ref16[...] = z.astype(BF16)
    z = _dot(hi, wih_ref[...]) + _dot(lo, wih_ref[...]) + _dot(hi, wil_ref[...])
    qi_ref[...] = z[:, :D_HEADS]
    misc_ref[...] = z[:, D_HEADS:]
    kcat_ref[...] = _split_keys(z[:, D_HEADS:D_HEADS + IDX_DIM])


def _proj_ab(x, sh, sc, gpre, wm, wih, wil, tm=256):
    n = x.shape[0]
    tm = min(tm, n)
    row = lambda w: pl.BlockSpec((tm, w), lambda i: (i, 0))
    full = lambda a: pl.BlockSpec(a.shape, lambda i: (0, 0))
    grp = pl.BlockSpec((tm // GROUP, 1, D_MODEL), lambda i: (i, 0, 0))
    sds = lambda w, dt: jax.ShapeDtypeStruct((n, w), dt)
    out_shape = ([sds(D_HEADS, BF16)] * 2 + [sds(D_HEADS, F32)] * 4 + [sds(D_HEADS, BF16)] * 4
                 + [sds(D_HEADS, F32), sds(LANES, F32), sds(4 * IDX_DIM, BF16)])
    out_specs = [row(D_HEADS)] * 11 + [row(LANES), row(4 * IDX_DIM)]
    return pl.pallas_call(
        _proj_ab_kernel,
        out_shape=out_shape,
        grid=(n // tm,),
        in_specs=[row(D_MODEL), grp, grp, pl.BlockSpec((1, D_MODEL), lambda i: (0, 0)),
                  full(wm), full(wih), full(wil)],
        out_specs=out_specs,
        compiler_params=_params("arbitrary"),
        name="proj_ab",
    )(x, sh, sc, gpre, wm, wih, wil)


def _lane_cumsum(x):
    lane = lax.broadcasted_iota(jnp.int32, x.shape, 1)
    s = 1
    while s < LANES:
        x = x + jnp.where(lane >= s, pltpu.roll(x, s, 1), 0.0)
        s *= 2
    return x


def _logf_kernel(*refs, n_cache, n_new):
    if n_cache:
        fl_ref, bf_ref, cache_ref, logf_ref, fnew_ref, fcache_ref = refs
    else:
        fl_ref, bf_ref, logf_ref, fnew_ref = refs
    carry = jnp.zeros((N_HEADS, 1), F32)
    for blk in range(n_cache):
        c = _lane_cumsum(cache_ref[blk]) + carry
        fcache_ref[blk] = c
        carry = c[:, LANES - 1:LANES]
    for blk in range(n_new):
        lf = jax.nn.log_sigmoid(fl_ref[blk] + bf_ref[...])
        logf_ref[blk] = lf
        c = _lane_cumsum(lf) + carry
        fnew_ref[blk] = c
        carry = c[:, LANES - 1:LANES]


def _blocked_rows(a):
    b, t, h = a.shape
    return a.reshape(b, t // LANES, LANES, h).transpose(0, 1, 3, 2)


def _unblocked_rows(a):
    b, nb, h, _ = a.shape
    return a.transpose(0, 1, 3, 2).reshape(b, nb * LANES, h)


def _logf(fl, b_f, cache_logf):
    b, t, _ = fl.shape
    n_new = t // LANES
    blk = lambda nb: pl.BlockSpec((None, nb, N_HEADS, LANES), lambda i: (i, 0, 0, 0))
    sds = lambda nb: jax.ShapeDtypeStruct((b, nb, N_HEADS, LANES), F32)
    ins = [_blocked_rows(fl), b_f.reshape(N_HEADS, 1)]
    in_specs = [blk(n_new), pl.BlockSpec((N_HEADS, 1), lambda i: (0, 0))]
    out_shape = [sds(n_new), sds(n_new)]
    out_specs = [blk(n_new), blk(n_new)]
    n_cache = 0
    if cache_logf is not None:
        n_cache = cache_logf.shape[1] // LANES
        ins.append(_blocked_rows(cache_logf))
        in_specs.append(blk(n_cache))
        out_shape.append(sds(n_cache))
        out_specs.append(blk(n_cache))
    outs = pl.pallas_call(
        functools.partial(_logf_kernel, n_cache=n_cache, n_new=n_new),
        out_shape=out_shape, grid=(b,), in_specs=in_specs, out_specs=out_specs,
        compiler_params=_params("arbitrary"), name="logf_cumsum",
    )(*ins)
    return outs


N_BIAS_TILES = 4


def _t5_bucket(rel):
    nb = REL_BUCKETS // 2
    max_exact = nb // 2
    base = jnp.where(rel > 0, nb, 0)
    n = jnp.abs(rel)
    nf = jnp.maximum(n, 1).astype(F32)
    large = max_exact + (jnp.log(nf / max_exact) / math.log(REL_MAX_DIST / max_exact)
                         * (nb - max_exact)).astype(jnp.int32)
    large = jnp.minimum(large, nb - 1)
    return base + jnp.where(n < max_exact, n, large)


def _bias_kernel(tab_ref, o_ref):
    tq = o_ref.shape[2]
    row = lax.broadcasted_iota(jnp.int32, (tq, LANES), 0)
    col = lax.broadcasted_iota(jnp.int32, (tq, LANES), 1)
    for t in range(N_BIAS_TILES):
        bucket = _t5_bucket(LANES * (t - (N_BIAS_TILES - 1)) + col - row)
        for h in range(N_HEADS):
            acc = jnp.zeros((tq, LANES), F32)
            for b in range(REL_BUCKETS):
                acc = jnp.where(bucket == b, tab_ref[b, h], acc)
            o_ref[t, h] = acc


def _bias_tiles(rel_bias, tq):
    assert LANES * (N_BIAS_TILES - 1) - (LANES - 1) >= REL_MAX_DIST
    return pl.pallas_call(
        _bias_kernel,
        out_shape=jax.ShapeDtypeStruct((N_BIAS_TILES, N_HEADS, tq, LANES), F32),
        in_specs=[pl.BlockSpec(memory_space=pltpu.SMEM)],
        name="rel_bias_tiles",
    )(rel_bias)


def _sortable(x):
    b = pltpu.bitcast(x, jnp.int32)
    return b ^ ((b >> 31) & 0x7FFFFFFF)


def _split_keys(ki):
    hi = ki.astype(BF16).astype(F32)
    return jnp.concatenate([hi, ki - hi, hi, ki - hi], axis=-1).astype(BF16)


def _block_rows(blk, size):
    start = blk * size
    return pl.ds(start if isinstance(start, int) else pl.multiple_of(start, size), size)


def _pair_queries(q16):
    lane = lax.broadcasted_iota(jnp.int32, (q16.shape[0], LANES), 1)
    out = []
    for h in range(N_HEADS):
        pair = q16[:, LANES * (h // 2):LANES * (h // 2 + 1)].astype(F32) * HEAD_DIM ** -0.5
        own = (lane < HEAD_DIM) if h % 2 == 0 else (lane >= HEAD_DIM)
        out.append(jnp.where(own, pair, 0.0).astype(BF16))
    return jnp.stack(out)


def _pair_keys(x):
    x = x.astype(BF16)
    return jnp.stack([x[:, LANES * (h // 2):LANES * (h // 2 + 1)] for h in range(N_HEADS)])


def _pair_values(x):
    x = x.astype(BF16)
    lane = lax.broadcasted_iota(jnp.int32, (x.shape[0], LANES), 1)
    out = []
    for h in range(N_HEADS):
        pair = x[:, LANES * (h // 2):LANES * (h // 2 + 1)]
        own = (lane < HEAD_DIM) if h % 2 == 0 else (lane >= HEAD_DIM)
        out.append(jnp.where(own, pair, jnp.ones_like(pair)))
    return jnp.stack(out)


def _unpair(o):
    lane = lax.broadcasted_iota(jnp.int32, o.shape[1:], 1)
    return jnp.concatenate([jnp.where(lane < HEAD_DIM, o[2 * p], o[2 * p + 1]) for p in range(N_HEADS // 2)], axis=-1)


def _softmax_init(m_scr, acc_scr):
    m_scr[...] = jnp.full(m_scr.shape, NEG, F32)
    acc_scr[...] = jnp.zeros(acc_scr.shape, F32)


def _softmax_step(tiles, m_scr, acc_scr):
    m_prev = m_scr[...]
    top = tiles[0][0]
    for s, _ in tiles[1:]:
        top = jnp.maximum(top, s)
    m_new = jnp.maximum(m_prev, jnp.max(top, axis=-1, keepdims=True))
    acc = jnp.exp(m_prev - m_new) * acc_scr[...]
    for s, v in tiles:
        acc = acc + _bmm(jnp.exp(s - m_new).astype(BF16), v)
    acc_scr[...] = acc
    m_scr[...] = m_new


def _softmax_out(acc_scr):
    acc = acc_scr[...]
    den = jnp.stack([pltpu.roll(acc[h], HEAD_DIM, 1) for h in range(N_HEADS)])
    return _unpair(acc / den)


def _softmax_scratch(tq):
    return [pltpu.VMEM((N_HEADS, tq, LANES), F32), pltpu.VMEM((N_HEADS, tq, LANES), F32)]


def _dsa_kernel(*refs, tq, n_cache, past, t_new, top_k, dyn_new):
    if n_cache:
        (q_ref, qi_ref, mq_ref, cki_ref, cka_ref, cva_ref, nki_ref, nka_ref, nva_ref, bias_ref,
         o_ref, keys_ref, madd_ref, qcat_scr, wi_scr, m_scr, acc_scr) = refs
    else:
        (q_ref, qi_ref, mq_ref, nki_ref, nka_ref, nva_ref, bias_ref,
         o_ref, keys_ref, madd_ref, qcat_scr, wi_scr, m_scr, acc_scr) = refs
        cki_ref = cka_ref = cva_ref = None
    i = pl.program_id(1)
    n_new = (i + 1) * (tq // LANES) if dyn_new else 1
    n_blk = n_cache + n_new
    qpos0 = past + i * tq
    qpos = qpos0 + lax.broadcasted_iota(jnp.int32, (tq, 1), 0)
    col = lax.broadcasted_iota(jnp.int32, (tq, LANES), 1)
    n_keys = past + t_new

    qi = qi_ref[...]
    for h in range(N_HEADS):
        x = qi[:, h * IDX_DIM:(h + 1) * IDX_DIM]
        hi = x.astype(BF16).astype(F32)
        qcat_scr[h * tq:(h + 1) * tq, :] = jnp.concatenate([hi, hi, x - hi, jnp.zeros_like(x)], axis=-1).astype(BF16)
        wi_scr[h] = jnp.broadcast_to(mq_ref[:, IDX_DIM + h:IDX_DIM + h + 1], (tq, LANES))
    score_scale = IDX_DIM ** -0.5 * N_HEADS ** -0.5

    def score_block(kcat, kpos0, slot):
        s = _dot_nt(qcat_scr[...], kcat).reshape(N_HEADS, tq, LANES)
        acc = jnp.sum(jnp.maximum(s, 0.0) * wi_scr[...], axis=0)
        kpos = kpos0 + col
        adm = jnp.logical_and((kpos // CHUNK) <= (qpos // CHUNK), kpos < n_keys)
        keys_ref[slot] = jnp.where(adm, _sortable(acc * score_scale), INT_MIN)

    if n_cache:
        def cache_scores(j, _):
            r0 = pl.multiple_of(j * LANES, LANES)
            score_block(_split_keys(cki_ref[pl.ds(r0, LANES), :]), j * LANES, j)
            return 0
        lax.fori_loop(0, n_cache, cache_scores, 0)

    def new_scores(j, _):
        r0 = pl.multiple_of(j * LANES, LANES)
        score_block(nki_ref[pl.ds(r0, LANES), :], past + j * LANES, n_cache + j)
        return 0
    lax.fori_loop(0, n_new, new_scores, 0)

    def count(pred):
        def body(j, c):
            return c + jnp.where(pred(keys_ref[j]), 1.0, 0.0)
        c = lax.fori_loop(0, n_blk, body, jnp.zeros((tq, LANES), F32))
        return jnp.sum(c, axis=-1, keepdims=True)

    def search(b, ans):
        cand = ans ^ lax.shift_left(jnp.int32(1), jnp.int32(31) - jnp.asarray(b, jnp.int32))
        cand_b = jnp.broadcast_to(cand, (tq, LANES))
        cnt = count(lambda k: k >= cand_b)
        return jnp.where(cnt >= top_k, cand, ans)

    kth = lax.fori_loop(0, 32, search, jnp.full((tq, 1), INT_MIN, jnp.int32))
    kth_b = jnp.broadcast_to(kth, (tq, LANES))
    n_gt = count(lambda k: k > kth_b)
    quota = top_k - n_gt
    tri = (lax.broadcasted_iota(jnp.int32, (LANES, LANES), 0)
           <= lax.broadcasted_iota(jnp.int32, (LANES, LANES), 1)).astype(BF16)

    def build_mask(j, seen):
        k = keys_ref[j]
        eq = jnp.where(k == kth_b, jnp.where(k != INT_MIN, 1.0, 0.0), 0.0)
        rank = _dot(eq.astype(BF16), tri) + seen
        sel = jnp.where(k > kth_b, 1.0, jnp.where(rank <= quota, eq, 0.0))
        madd_ref[j] = (1.0 - sel) * NEG
        return seen + jnp.sum(eq, axis=-1, keepdims=True)
    lax.fori_loop(0, n_blk, build_mask, jnp.zeros((tq, 1), F32))

    qm = _pair_queries(q_ref[...])
    _softmax_init(m_scr, acc_scr)

    def step(k_ref, v_ref, blocks, slot0, kblk0):
        tiles = []
        for blk in blocks:
            rows = _block_rows(blk, LANES)
            tile = jnp.clip(kblk0 + blk - qpos0 // LANES + (N_BIAS_TILES - 1), 0, N_BIAS_TILES - 1)
            s = _bmm_nt(qm, _pair_keys(k_ref[rows, :])) + bias_ref[tile] + madd_ref[slot0 + blk][None]
            tiles.append((s, _pair_values(v_ref[rows, :])))
        _softmax_step(tiles, m_scr, acc_scr)

    if n_cache:
        def cache_step(j, _):
            step(cka_ref, cva_ref, (2 * j, 2 * j + 1), 0, 0)
            return 0
        lax.fori_loop(0, n_cache // 2, cache_step, 0)

    if dyn_new:
        @pl.when(n_new % 2 == 1)
        def _():
            madd_ref[n_blk] = jnp.full((tq, LANES), NEG, F32)

        def new_step(j, _):
            step(nka_ref, nva_ref, (2 * j, 2 * j + 1), n_cache, past // LANES)
            return 0
        lax.fori_loop(0, (n_new + 1) // 2, new_step, 0)
    else:
        step(nka_ref, nva_ref, (0,), n_cache, past // LANES)
    o_ref[...] = _softmax_out(acc_scr).astype(BF16)


def _dsa(q, qi, misc, kcat, ka16, va16, bias, cache, *, batch, t_q, tq):
    t_kv = ka16.shape[0] // batch
    nq = t_q // tq
    past = 0 if cache is None else cache[0].shape[1]
    n_cache = past // LANES
    n_keys = past + t_q
    top_k = min(IDX_TOPK_MAX, n_keys // 4)
    assert tq == LANES or nq == 1
    assert n_cache % 2 == 0 and (cache is not None or (t_kv // LANES) % 2 == 0)
    qrow = lambda w: pl.BlockSpec((tq, w), lambda b, i: (b * (t_kv // tq) + i, 0))
    krow = lambda w: pl.BlockSpec((t_kv, w), lambda b, i: (b, 0))
    crow = lambda w: pl.BlockSpec((None, past, w), lambda b, i: (b, 0, 0))
    ins = [q, qi, misc]
    in_specs = [qrow(D_HEADS), qrow(D_HEADS), qrow(LANES)]
    if cache is not None:
        ins += list(cache)
        in_specs += [crow(IDX_DIM), crow(D_HEADS), crow(D_HEADS)]
    ins += [kcat, ka16, va16, bias]
    in_specs += [krow(4 * IDX_DIM), krow(D_HEADS), krow(D_HEADS),
                 pl.BlockSpec(bias.shape, lambda b, i: (0, 0, 0, 0))]
    n_blk_max = n_cache + t_kv // LANES
    return pl.pallas_call(
        functools.partial(_dsa_kernel, tq=tq, n_cache=n_cache, past=past, t_new=t_q, top_k=top_k,
                          dyn_new=cache is None),
        out_shape=jax.ShapeDtypeStruct((batch * t_kv, D_HEADS), BF16),
        grid=(batch, nq),
        in_specs=in_specs,
        out_specs=qrow(D_HEADS),
        scratch_shapes=[pltpu.VMEM((n_blk_max, tq, LANES), jnp.int32),
                        pltpu.VMEM((n_blk_max, tq, LANES), F32),
                        pltpu.VMEM((N_HEADS * tq, 4 * IDX_DIM), BF16),
                        pltpu.VMEM((N_HEADS, tq, LANES), F32)] + _softmax_scratch(tq),
        compiler_params=_params("arbitrary", "arbitrary"),
        name="dsa_attention",
    )(*ins)


FOX_GROUP = 4


def _fox_kernel(*refs, tq, tk, tkc, n_cache, past, dyn_new):
    if n_cache:
        q_ref, fq_ref, ck_ref, cv_ref, cf_ref, nk_ref, nv_ref, nf_ref, o_ref, m_scr, acc_scr = refs
    else:
        q_ref, fq_ref, nk_ref, nv_ref, nf_ref, o_ref, m_scr, acc_scr = refs
        ck_ref = cv_ref = cf_ref = None
    i = pl.program_id(1)
    group = tq // tk if dyn_new else 1
    qpos = past + i * tq + lax.broadcasted_iota(jnp.int32, (1, tq, 1), 1)
    col = lax.broadcasted_iota(jnp.int32, (1, 1, LANES), 2)

    qm = _pair_queries(q_ref[...])
    fq = jnp.stack([fq_ref[:, h:h + 1] for h in range(N_HEADS)])
    _softmax_init(m_scr, acc_scr)

    def step(k_ref, v_ref, f_ref, blocks, size, kpos_base=None):
        tiles = []
        for blk in blocks:
            rows = _block_rows(blk, size)
            s = _bmm_nt(qm, _pair_keys(k_ref[rows, :])) + fq - f_ref[blk]
            if kpos_base is not None:
                s = jnp.where(kpos_base + blk * size + col <= qpos, s, NEG)
            tiles.append((s, _pair_values(v_ref[rows, :])))
        _softmax_step(tiles, m_scr, acc_scr)

    if n_cache:
        def cache_step(j, _):
            step(ck_ref, cv_ref, cf_ref, [FOX_GROUP * j + d for d in range(FOX_GROUP)], tkc)
            return 0
        lax.fori_loop(0, n_cache // FOX_GROUP, cache_step, 0)

    if dyn_new:
        def new_step(j, _):
            step(nk_ref, nv_ref, nf_ref, [group * j + d for d in range(group)], tk)
            return 0
        lax.fori_loop(0, i, new_step, 0)
        step(nk_ref, nv_ref, nf_ref, [group * i + d for d in range(group)], tk, past)
    else:
        step(nk_ref, nv_ref, nf_ref, [0], tk, past)
    o_ref[...] = _softmax_out(acc_scr).astype(BF16)


def _key_rows(f, tk):
    b, t, h = f.shape
    return f.reshape(b, t // tk, tk, h).transpose(0, 1, 3, 2)[:, :, :, None, :]


def _fox(q, fq, kb16, vb16, f_new, cache, *, batch, t_q, tq, tk, tkc):
    t_kv = kb16.shape[0] // batch
    nq = t_q // tq
    past = 0 if cache is None else cache[0].shape[1]
    n_cache = past // tkc
    assert tk == LANES and tkc == LANES and n_cache % FOX_GROUP == 0
    qrow = lambda w: pl.BlockSpec((tq, w), lambda b, i: (b * (t_kv // tq) + i, 0))
    krow = lambda w: pl.BlockSpec((t_kv, w), lambda b, i: (b, 0))
    crow = lambda w: pl.BlockSpec((None, past, w), lambda b, i: (b, 0, 0))
    fblk = lambda nb, w: pl.BlockSpec((None, nb, N_HEADS, 1, w), lambda b, i: (b, 0, 0, 0, 0))
    ins = [q, fq]
    in_specs = [qrow(D_HEADS), qrow(N_HEADS)]
    if cache is not None:
        ins += [cache[0], cache[1], _key_rows(cache[2], tkc)]
        in_specs += [crow(D_HEADS), crow(D_HEADS), fblk(n_cache, tkc)]
    ins += [kb16, vb16, _key_rows(f_new, tk)]
    in_specs += [krow(D_HEADS), krow(D_HEADS), fblk(t_kv // tk, tk)]
    return pl.pallas_call(
        functools.partial(_fox_kernel, tq=tq, tk=tk, tkc=tkc, n_cache=n_cache, past=past, dyn_new=cache is None),
        out_shape=jax.ShapeDtypeStruct((batch * t_kv, D_HEADS), BF16),
        grid=(batch, nq),
        in_specs=in_specs,
        out_specs=qrow(D_HEADS),
        scratch_shapes=_softmax_scratch(tq),
        compiler_params=_params("arbitrary", "arbitrary"),
        name="fox_attention",
    )(*ins)


def _out_ab_kernel(oa_ref, ob_ref, w_ref, x_ref, gt_ref, gpost_ref, o_ref):
    y = _dot(oa_ref[...], w_ref[:D_HEADS, :]) + _dot(ob_ref[...], w_ref[D_HEADS:, :])
    o_ref[...] = _residual(x_ref[...], y, gpost_ref[...], gt_ref[...], 1.0)


def _out_ab(oa, ob, w, x, gt, gpost, *, batch, t_q, tm):
    t_kv = oa.shape[0] // batch
    nt = t_q // tm
    arow = pl.BlockSpec((tm, D_HEADS), lambda b, i: (b * (t_kv // tm) + i, 0))
    xrow = pl.BlockSpec((tm, D_MODEL), lambda b, i: (b * nt + i, 0))
    return pl.pallas_call(
        _out_ab_kernel,
        out_shape=jax.ShapeDtypeStruct(x.shape, F32),
        grid=(batch, nt),
        in_specs=[arow, arow, pl.BlockSpec(w.shape, lambda b, i: (0, 0)), xrow,
                  pl.BlockSpec((tm // GROUP, 1, D_MODEL), lambda b, i: (b * nt + i, 0, 0)),
                  pl.BlockSpec((1, D_MODEL), lambda b, i: (0, 0))],
        out_specs=xrow,
        compiler_params=_params("arbitrary", "arbitrary"),
        name="out_ab",
    )(oa, ob, w, x, gt, gpost)


GDN_MISC0 = C_QKV + N_HEADS * DK_C


def _proj_gdn_kernel(x_ref, sh_ref, sc_ref, gpre_ref, w_ref, cw_ref, prev_ref, alog_ref, dtb_ref,
                     q_ref, k_ref, v_ref, gz_ref, misc_ref, tail_ref, zbuf):
    tm = x_ref.shape[0]
    halo = CONV_W - 1
    base = 8

    @pl.when(pl.program_id(1) == 0)
    def _():
        zbuf[base - halo:base, :] = prev_ref[...]

    h = _modulated(x_ref[...], gpre_ref[...], sh_ref[...], sc_ref[...]).astype(BF16)
    zbuf[base:base + tm, :] = _dot(h, w_ref[:, :C_QKV])
    gz_ref[...] = _dot(h, w_ref[:, C_QKV:GDN_MISC0])
    zm = _dot(h, w_ref[:, GDN_MISC0:])
    lane = lax.broadcasted_iota(jnp.int32, zm.shape, 1)
    beta = jax.nn.sigmoid(zm)
    g = -jnp.exp(alog_ref[...]) * jax.nn.softplus(zm + dtb_ref[...])
    misc_ref[...] = jnp.where(lane < N_HEADS, beta, g)

    for part, ref in enumerate((q_ref, k_ref, v_ref)):
        for hd in range(N_HEADS):
            c0 = part * N_HEADS * DK_C + hd * DK_C
            cols = slice(c0, c0 + DK_C)
            conv = zbuf[base:base + tm, cols] * cw_ref[halo:halo + 1, cols]
            for j in range(halo):
                conv = conv + zbuf[base - halo + j:base - halo + j + tm, cols] * cw_ref[j:j + 1, cols]
            a = _silu(conv)
            if part < 2:
                a = a * lax.rsqrt(jnp.sum(a * a, axis=-1, keepdims=True) + EPS)
                if part == 0:
                    a = a * DK_C ** -0.5
            ref[:, hd * DK_C:(hd + 1) * DK_C] = a
    tail = zbuf[base + tm - halo:base + tm, :]
    tail_ref[...] = tail
    zbuf[base - halo:base, :] = tail


def _proj_gdn(x, sh, sc, gpre, w, conv_w, conv_prev, alog_row, dtb_row, *, batch, tm):
    n = x.shape[0]
    t = n // batch
    nt = t // tm
    row = lambda wd: pl.BlockSpec((tm, wd), lambda b, i: (b * nt + i, 0))
    full = lambda a: pl.BlockSpec(a.shape, lambda b, i: (0,) * a.ndim)
    sds = lambda wd: jax.ShapeDtypeStruct((n, wd), F32)
    hd = N_HEADS * DK_C
    tail = pl.BlockSpec((None, CONV_W - 1, C_QKV), lambda b, i: (b, 0, 0))
    return pl.pallas_call(
        _proj_gdn_kernel,
        out_shape=[sds(hd), sds(hd), sds(hd), sds(hd), sds(LANES),
                   jax.ShapeDtypeStruct((batch, CONV_W - 1, C_QKV), F32)],
        grid=(batch, nt),
        in_specs=[row(D_MODEL),
                  pl.BlockSpec((tm // GROUP, 1, D_MODEL), lambda b, i: (b * nt + i, 0, 0)),
                  pl.BlockSpec((tm // GROUP, 1, D_MODEL), lambda b, i: (b * nt + i, 0, 0)),
                  pl.BlockSpec((1, D_MODEL), lambda b, i: (0, 0)),
                  full(w), full(conv_w), tail, full(alog_row), full(dtb_row)],
        out_specs=[row(hd), row(hd), row(hd), row(hd), row(LANES), tail],
        scratch_shapes=[pltpu.VMEM((8 + tm, C_QKV), F32)],
        compiler_params=_params("arbitrary", "arbitrary"),
        name="proj_gdn",
    )(x, sh, sc, gpre, w, conv_w, conv_prev, alog_row, dtb_row)


def _chunk_cumsum_kernel(g_ref, o_ref, *, chunk):
    x = g_ref[...]
    lane = lax.broadcasted_iota(jnp.int32, x.shape, 1) % chunk
    s = 1
    while s < chunk:
        x = x + jnp.where(lane >= s, pltpu.roll(x, s, 1), 0.0)
        s *= 2
    o_ref[...] = x


def _chunk_cumsum(g_rows, chunk):
    assert LANES % chunk == 0
    tiles = g_rows.reshape(-1, LANES)
    return pl.pallas_call(
        functools.partial(_chunk_cumsum_kernel, chunk=chunk),
        out_shape=jax.ShapeDtypeStruct(tiles.shape, F32),
        name="gdn_chunk_cumsum",
    )(tiles).reshape(g_rows.shape)


def _gdn_kernel(q_ref, k_ref, v_ref, col_ref, row_ref, s0_ref, o_ref, s_ref):
    c = q_ref.shape[0]

    @pl.when(pl.program_id(1) == 0)
    def _():
        s_ref[...] = s0_ref[...]

    heads = lambda ref: jnp.stack([ref[:, h * DK_C:(h + 1) * DK_C] for h in range(N_HEADS)])
    q, k, v = heads(q_ref), heads(k_ref), heads(v_ref)
    beta = jnp.stack([col_ref[:, h:h + 1] for h in range(N_HEADS)])
    gc = jnp.stack([col_ref[:, N_HEADS + h:N_HEADS + h + 1] for h in range(N_HEADS)])
    gc_row = row_ref[...]
    gc_last = gc[:, c - 1:c, :]
    ri = lax.broadcasted_iota(jnp.int32, (1, c, c), 1)
    ci = lax.broadcasted_iota(jnp.int32, (1, c, c), 2)
    causal = ri >= ci
    strict = ri > ci
    eye = jnp.where(ri == ci, 1.0, 0.0)
    decay = jnp.exp(jnp.where(causal, gc - gc_row, NEG))
    k16 = k.astype(BF16)
    a = jnp.where(strict, beta * _bmm_nt(k16, k16) * decay, 0.0)
    inv = eye - a
    p = a
    n = 2
    while n < c:
        p = _dot3(p, p, _bmm)
        inv = inv + _dot3(inv, p, _bmm)
        n *= 2
    sol = _dot3(inv, jnp.concatenate([v * beta, k * (beta * jnp.exp(gc))], axis=-1), _bmm)
    u, w = sol[:, :, :DK_C], sol[:, :, DK_C:]
    qk = jnp.where(causal, _bmm_nt(q.astype(BF16), k16) * decay, 0.0)
    q_dec = (q * jnp.exp(gc)).astype(BF16)
    k_dec = k * jnp.exp(gc_last - gc)
    k_dec_t = jnp.stack([k_dec[h].T for h in range(N_HEADS)]).astype(BF16)
    s = s_ref[...]
    s16 = s.astype(BF16)
    v16 = (u - _bmm(w.astype(BF16), s16)).astype(BF16)
    o = _bmm(q_dec, s16) + _bmm(qk.astype(BF16), v16)
    for h in range(N_HEADS):
        o_ref[:, h * DK_C:(h + 1) * DK_C] = o[h]
    s_ref[...] = s * jnp.exp(gc_last) + _bmm(k_dec_t, v16)


def _gdn(q, k, v, col, row, s0, *, batch, chunk):
    n = q.shape[0]
    nc = n // batch // chunk
    hd = N_HEADS * DK_C
    blk = lambda w: pl.BlockSpec((chunk, w), lambda b, i: (b * nc + i, 0))
    state = pl.BlockSpec((None, N_HEADS, DK_C, DK_C), lambda b, i: (b, 0, 0, 0))
    return pl.pallas_call(
        _gdn_kernel,
        out_shape=[jax.ShapeDtypeStruct((n, hd), F32), jax.ShapeDtypeStruct(s0.shape, F32)],
        grid=(batch, nc),
        in_specs=[blk(hd), blk(hd), blk(hd), blk(2 * N_HEADS),
                  pl.BlockSpec((None, N_HEADS, 1, chunk), lambda b, i: (b * nc + i, 0, 0, 0)), state],
        out_specs=[blk(hd), state],
        compiler_params=_params("arbitrary", "arbitrary"),
        name="gated_delta_rule",
    )(q, k, v, col, row, s0)


def _out_gdn_kernel(o_ref, gz_ref, nw_ref, w_ref, x_ref, gt_ref, gpost_ref, out_ref):
    parts = []
    for h in range(N_HEADS):
        cols = slice(h * DK_C, (h + 1) * DK_C)
        o = o_ref[:, cols]
        o = o * lax.rsqrt(jnp.mean(o * o, axis=-1, keepdims=True) + EPS) * nw_ref[...]
        parts.append((o * _silu(gz_ref[:, cols])).astype(BF16))
    y = _dot(jnp.concatenate(parts, axis=-1), w_ref[...])
    out_ref[...] = _residual(x_ref[...], y, gpost_ref[...], gt_ref[...], 1.0)


def _out_gdn(o, gz, norm_w, w, x, gt, gpost, tm=256):
    n = x.shape[0]
    tm = min(tm, n)
    row = pl.BlockSpec((tm, D_MODEL), lambda i: (i, 0))
    return pl.pallas_call(
        _out_gdn_kernel,
        out_shape=jax.ShapeDtypeStruct(x.shape, F32),
        grid=(n // tm,),
        in_specs=[row, row, pl.BlockSpec((1, DK_C), lambda i: (0, 0)), pl.BlockSpec(w.shape, lambda i: (0, 0)),
                  row, pl.BlockSpec((tm // GROUP, 1, D_MODEL), lambda i: (i, 0, 0)),
                  pl.BlockSpec((1, D_MODEL), lambda i: (0, 0))],
        out_specs=row,
        compiler_params=_params("arbitrary"),
        name="out_gdn",
    )(o, gz, norm_w, w, x, gt, gpost)


def _mixer_ab(x, mod, w, past, *, batch, t, bias):
    sh, sc, gt = mod
    (qa, qb, ka, va, kb, vb, ka16, va16, kb16, vb16, qi, misc, kcat) = _proj_ab(
        x, sh, sc, w["pre"], w["ab_main"], w["ab_idx_hi"], w["ab_idx_lo"])
    ki = misc[:, :IDX_DIM]
    fl = misc[:, IDX_DIM + N_HEADS:IDX_DIM + 2 * N_HEADS].reshape(batch, t, N_HEADS)
    t_kv = -(-t // LANES) * LANES
    pad_rows = lambda a: a if t_kv == t else jnp.pad(
        a.reshape(batch, t, -1), ((0, 0), (0, t_kv - t), (0, 0))).reshape(batch * t_kv, -1)
    fl_p = fl if t_kv == t else jnp.pad(fl, ((0, 0), (0, t_kv - t), (0, 0)))
    if past is None:
        logf_b, f_new = _logf(fl_p, w["ab_b_f"], None)
        cache_a = cache_b = None
    else:
        cak, cav, caik, cbk, cbv, cblf = past
        p = cak.shape[1]
        logf_b, f_new, f_cache = _logf(fl_p, w["ab_b_f"], cblf)
        cache_a = (caik, cak.reshape(batch, p, D_HEADS), cav.reshape(batch, p, D_HEADS))
        cache_b = (cbk.reshape(batch, p, D_HEADS), cbv.reshape(batch, p, D_HEADS), _unblocked_rows(f_cache))
    logf = _unblocked_rows(logf_b)[:, :t]
    f_new = _unblocked_rows(f_new)
    o_a = _dsa(pad_rows(qa), pad_rows(qi), pad_rows(misc), pad_rows(kcat), pad_rows(ka16), pad_rows(va16), bias, cache_a,
               batch=batch, t_q=t, tq=min(LANES, t))
    o_b = _fox(pad_rows(qb), f_new.reshape(batch * t_kv, N_HEADS), pad_rows(kb16), pad_rows(vb16), f_new, cache_b,
               batch=batch, t_q=t, tq=min(4 * LANES, t), tk=LANES, tkc=LANES)
    x = _out_ab(o_a, o_b, w["ab_w_out"], x, gt, w["post"], batch=batch, t_q=t, tm=min(256, t))
    hd = (batch, t, N_HEADS, HEAD_DIM)
    state = (ka.reshape(hd), va.reshape(hd), ki.reshape(batch, t, IDX_DIM), kb.reshape(hd), vb.reshape(hd), logf)
    return x, state


def _mixer_gdn(x, mod, w, past, *, batch, t):
    sh, sc, gt = mod
    if past is None:
        conv_prev = jnp.zeros((batch, CONV_W - 1, C_QKV), F32)
        s0 = jnp.zeros((batch, N_HEADS, DK_C, DK_C), F32)
    else:
        conv_prev, s0 = past
    q, k, v, gz, misc, new_conv = _proj_gdn(x, sh, sc, w["pre"], w["gdn_w_in"], w["gdn_conv_w"], conv_prev,
                                            w["gdn_alog_row"], w["gdn_dtb_row"], batch=batch, tm=min(256, t))
    chunk = min(t, CHUNK)
    n = batch * t
    gc_rows = _chunk_cumsum(misc[:, N_HEADS:2 * N_HEADS].T, chunk)
    col = jnp.concatenate([misc[:, :N_HEADS], gc_rows.T], axis=-1)
    row = gc_rows.reshape(N_HEADS, n // chunk, chunk).transpose(1, 0, 2)[:, :, None, :]
    o, s_new = _gdn(q, k, v, col, row, s0, batch=batch, chunk=chunk)
    x = _out_gdn(o, gz, w["gdn_norm_w"], w["gdn_w_out"], x, gt, w["post"])
    return x, (new_conv, s_new)


def _forward(x, mod_all, w, past, *, batch, t):
    new_state = []
    groups = lambda v: jnp.repeat(v, t // GROUP, axis=0)[:, None, :]
    for l in range(mod_all.shape[0]):
        m = mod_all[l].reshape(batch, 3, 3, D_MODEL)
        mod = [[groups(m[:, s, j]) for j in range(3)] for s in range(3)]
        pre = lambda s: w["norm_pre"][l, s][None, :]
        post = lambda s: w["norm_post"][l, s][None, :]
        x = _ffn(x, mod[0], pre(0), post(0), w["ffn_w_gate"][l, 0], w["ffn_w_up"][l, 0], w["ffn_w_down"][l, 0])
        wl = dict(w, pre=pre(1), post=post(1))
        if l % 2 == 0:
            x, st = _mixer_ab(x, mod[1], wl, None if past is None else past[:6], batch=batch, t=t,
                              bias=w["bias_tiles"][min(LANES, t)])
        else:
            x, st = _mixer_gdn(x, mod[1], wl, None if past is None else past[6:], batch=batch, t=t)
        new_state.extend(st)
        x = _ffn(x, mod[2], pre(2), post(2), w["ffn_w_gate"][l, 1], w["ffn_w_up"][l, 1], w["ffn_w_down"][l, 1])
    return x, new_state


def _prepare_weights(rel_bias, ab_w_in, ab_b_f, ab_w_out, gdn_w_in, gdn_conv_w, gdn_A_log, gdn_dt_bias,
                     gdn_norm_w, gdn_w_out, norm_pre, norm_post, ffn_w_gate, ffn_w_up, ffn_w_down, tqs):
    o = [0]
    for s in (D_HEADS, D_HEADS, D_HEADS, D_HEADS, IDX_DIM, N_HEADS, D_HEADS, D_HEADS, D_HEADS, N_HEADS):
        o.append(o[-1] + s)
    cols = lambda i: ab_w_in[:, o[i]:o[i + 1]]
    qa, ka, va, qi, ki, wi, qb, kb, vb, fl = (cols(i) for i in range(10))
    idx = jnp.concatenate([qi, ki, wi, fl, jnp.zeros((D_MODEL, IDX_COLS - D_HEADS - IDX_DIM - 2 * N_HEADS), F32)], 1)
    idx_hi = idx.astype(BF16)
    gdn_in = jnp.concatenate([gdn_w_in, jnp.zeros((D_MODEL, LANES - 2 * N_HEADS), F32)], 1)
    lane_row = lambda v: jnp.concatenate([jnp.zeros((N_HEADS,), F32), v.astype(F32),
                                          jnp.zeros((LANES - 2 * N_HEADS,), F32)])[None, :]
    return dict(
        ab_main=jnp.concatenate([qa, qb, ka, va, kb, vb], 1).astype(BF16),
        ab_idx_hi=idx_hi, ab_idx_lo=(idx - idx_hi.astype(F32)).astype(BF16),
        ab_b_f=ab_b_f, ab_w_out=ab_w_out.astype(BF16),
        gdn_w_in=gdn_in.astype(BF16), gdn_conv_w=gdn_conv_w,
        gdn_alog_row=lane_row(gdn_A_log), gdn_dtb_row=lane_row(gdn_dt_bias),
        gdn_norm_w=gdn_norm_w[None, :], gdn_w_out=gdn_w_out.astype(BF16),
        norm_pre=norm_pre, norm_post=norm_post,
        ffn_w_gate=ffn_w_gate.astype(BF16), ffn_w_up=ffn_w_up.astype(BF16), ffn_w_down=ffn_w_down.astype(BF16),
        bias_tiles={tq: _bias_tiles(rel_bias, tq) for tq in tqs},
    )


def kernel(x_prompt, x_sample, cache_a_k, cache_a_v, cache_a_idx_k, cache_b_k, cache_b_v, cache_b_logf,
           state_c_conv, state_c_rec, c_prompt, c_sample, rel_bias, ab_w_in, ab_b_f, ab_w_out,
           gdn_w_in, gdn_conv_w, gdn_A_log, gdn_dt_bias, gdn_norm_w, gdn_w_out, norm_pre, norm_post,
           ada_w, ada_b, ffn_w_gate, ffn_w_up, ffn_w_down):
    bp, tp, _ = x_prompt.shape
    bs, ts, _ = x_sample.shape
    w = _prepare_weights(rel_bias, ab_w_in, ab_b_f, ab_w_out, gdn_w_in, gdn_conv_w, gdn_A_log, gdn_dt_bias,
                         gdn_norm_w, gdn_w_out, norm_pre, norm_post, ffn_w_gate, ffn_w_up, ffn_w_down,
                         tqs={min(LANES, tp), min(LANES, ts)})
    mod = _ada(jnp.concatenate([c_prompt, c_sample], 0), ada_w, ada_b)
    y_p, st_p = _forward(x_prompt.reshape(bp * tp, D_MODEL), mod[:, :bp], w, None, batch=bp, t=tp)
    past = (cache_a_k, cache_a_v, cache_a_idx_k, cache_b_k, cache_b_v, cache_b_logf, state_c_conv, state_c_rec)
    y_s, st_s = _forward(x_sample.reshape(bs * ts, D_MODEL), mod[:, bp:], w, past, batch=bs, t=ts)
    return (y_p.reshape(bp, tp, D_MODEL), y_s.reshape(bs, ts, D_MODEL), *st_p, *st_s)
```

```python
import functools
import math

import jax
import jax.numpy as jnp
from jax import lax
from jax.experimental import pallas as pl
from jax.experimental.pallas import tpu as pltpu

D_MODEL = 1024
D_FF = 4 * D_MODEL
CHUNK = 64
HEAD_DIM = 64
N_HEADS = 8
D_HEADS = N_HEADS * HEAD_DIM
IDX_DIM = 64
IDX_TOPK_MAX = 256
DK_C = 128
CONV_W = 4
C_QKV = 3 * N_HEADS * DK_C
REL_BUCKETS = 32
REL_MAX_DIST = 256
FFN_RES = 0.5
EPS = 1e-6

GROUP = 32
LANES = 128
NEG = -1e30
INT_MIN = -(2 ** 31)
VMEM_LIMIT = 56 * 1024 * 1024

F32 = jnp.float32
BF16 = jnp.bfloat16


def _params(*sem):
    return pltpu.CompilerParams(dimension_semantics=sem, vmem_limit_bytes=VMEM_LIMIT)


def _dot(a, b):
    return jnp.dot(a, b, preferred_element_type=F32)


def _dot_nt(a, b):
    return lax.dot_general(a, b, (((1,), (1,)), ((), ())), preferred_element_type=F32)


def _bmm(a, b):
    return lax.dot_general(a, b, (((2,), (1,)), ((0,), (0,))), preferred_element_type=F32)


def _bmm_nt(a, b):
    return lax.dot_general(a, b, (((2,), (2,)), ((0,), (0,))), preferred_element_type=F32)


def _split(x):
    hi = x.astype(BF16)
    lo = (x - hi.astype(F32)).astype(BF16)
    return hi, lo


def _dot3(a, b, mm=_dot):
    ah, al = _split(a)
    bh, bl = _split(b)
    return mm(ah, bh) + mm(al, bh) + mm(ah, bl)


def _silu(x):
    return x * jax.nn.sigmoid(x)


def _modulated(x, gpre, sh, sc):
    r = lax.rsqrt(jnp.mean(x * x, axis=-1, keepdims=True) + EPS)
    y = (x * r * gpre).reshape(sh.shape[0], GROUP, x.shape[-1])
    return (y * (1.0 + sc) + sh).reshape(x.shape)


def _residual(x, y, gpost, gate, weight):
    r = lax.rsqrt(jnp.mean(y * y, axis=-1, keepdims=True) + EPS)
    yn = (y * r * gpost).reshape(gate.shape[0], GROUP, x.shape[-1])
    return x + (weight * gate * yn).reshape(x.shape)


def _ada_kernel(c_ref, w_ref, b_ref, o_ref):
    o_ref[...] = _dot3(_silu(c_ref[...]), w_ref[...]) + b_ref[...]


def _ada(c_all, ada_w, ada_b):
    depth, _, n_out = ada_w.shape
    nb = c_all.shape[0]
    tn = 1024
    return pl.pallas_call(
        _ada_kernel,
        out_shape=jax.ShapeDtypeStruct((depth, nb, n_out), F32),
        grid=(depth, n_out // tn),
        in_specs=[pl.BlockSpec((nb, D_MODEL), lambda l, j: (0, 0)),
                  pl.BlockSpec((None, D_MODEL, tn), lambda l, j: (l, 0, j)),
                  pl.BlockSpec((None, 1, tn), lambda l, j: (l, 0, j))],
        out_specs=pl.BlockSpec((None, nb, tn), lambda l, j: (l, 0, j)),
        compiler_params=_params("arbitrary", "arbitrary"),
        name="ada_mod",
    )(c_all, ada_w, ada_b.reshape(depth, 1, n_out))


def _ffn_kernel(x_ref, sh_ref, sc_ref, gt_ref, gpre_ref, gpost_ref, wg_ref, wu_ref, wd_ref, o_ref,
                h_scr, acc_scr):
    j = pl.program_id(1)

    @pl.when(j == 0)
    def _():
        h_scr[...] = _modulated(x_ref[...], gpre_ref[...], sh_ref[...], sc_ref[...]).astype(BF16)
        acc_scr[...] = jnp.zeros_like(acc_scr)

    h = h_scr[...]
    a = _dot(h, wg_ref[...])
    b = _dot(h, wu_ref[...])
    acc_scr[...] += _dot((_silu(a) * b).astype(BF16), wd_ref[...])

    @pl.when(j == pl.num_programs(1) - 1)
    def _():
        o_ref[...] = _residual(x_ref[...], acc_scr[...], gpost_ref[...], gt_ref[...], FFN_RES)


def _ffn(x, mod, gpre, gpost, wg, wu, wd, tm=512, tf=512):
    n = x.shape[0]
    tm = min(tm, n)
    sh, sc, gt = mod
    row = pl.BlockSpec((tm, D_MODEL), lambda i, j: (i, 0))
    grp = pl.BlockSpec((tm // GROUP, 1, D_MODEL), lambda i, j: (i, 0, 0))
    vec = pl.BlockSpec((1, D_MODEL), lambda i, j: (0, 0))
    return pl.pallas_call(
        _ffn_kernel,
        out_shape=jax.ShapeDtypeStruct((n, D_MODEL), F32),
        grid=(n // tm, D_FF // tf),
        in_specs=[row, grp, grp, grp, vec, vec,
                  pl.BlockSpec((D_MODEL, tf), lambda i, j: (0, j)),
                  pl.BlockSpec((D_MODEL, tf), lambda i, j: (0, j)),
                  pl.BlockSpec((tf, D_MODEL), lambda i, j: (j, 0))],
        out_specs=row,
        scratch_shapes=[pltpu.VMEM((tm, D_MODEL), BF16), pltpu.VMEM((tm, D_MODEL), F32)],
        compiler_params=_params("arbitrary", "arbitrary"),
        name="ffn_sublayer",
    )(x, sh, sc, gt, gpre, gpost, wg, wu, wd)


AB_MAIN = ("qa", "qb", "ka", "va", "kb", "vb")
IDX_COLS = D_HEADS + LANES


def _proj_ab_kernel(x_ref, sh_ref, sc_ref, gpre_ref, wm_ref, wih_ref, wil_ref,
                    qa_ref, qb_ref, ka_ref, va_ref, kb_ref, vb_ref,
                    ka16_ref, va16_ref, kb16_ref, vb16_ref, qi_ref, misc_ref, kcat_ref):
    h = _modulated(x_ref[...], gpre_ref[...], sh_ref[...], sc_ref[...])
    hi, lo = _split(h)
    for n, ref in enumerate((qa_ref, qb_ref)):
        ref[...] = _dot(hi, wm_ref[:, n * D_HEADS:(n + 1) * D_HEADS]).astype(BF16)
    for n, (ref, ref16) in enumerate(((ka_ref, ka16_ref), (va_ref, va16_ref),
                                      (kb_ref, kb16_ref), (vb_ref, vb16_ref))):
        z = _dot(hi, wm_ref[:, (n + 2) * D_HEADS:(n + 3) * D_HEADS])
        ref[...] = z
        ref16[...] = z.astype(BF16)
    z = _dot(hi, wih_ref[...]) + _dot(lo, wih_ref[...]) + _dot(hi, wil_ref[...])
    qi_ref[...] = z[:, :D_HEADS]
    misc_ref[...] = z[:, D_HEADS:]
    kcat_ref[...] = _split_keys(z[:, D_HEADS:D_HEADS + IDX_DIM])


def _proj_ab(x, sh, sc, gpre, wm, wih, wil, tm=256):
    n = x.shape[0]
    tm = min(tm, n)
    row = lambda w: pl.BlockSpec((tm, w), lambda i: (i, 0))
    full = lambda a: pl.BlockSpec(a.shape, lambda i: (0, 0))
    grp = pl.BlockSpec((tm // GROUP, 1, D_MODEL), lambda i: (i, 0, 0))
    sds = lambda w, dt: jax.ShapeDtypeStruct((n, w), dt)
    out_shape = ([sds(D_HEADS, BF16)] * 2 + [sds(D_HEADS, F32)] * 4 + [sds(D_HEADS, BF16)] * 4
                 + [sds(D_HEADS, F32), sds(LANES, F32), sds(4 * IDX_DIM, BF16)])
    out_specs = [row(D_HEADS)] * 11 + [row(LANES), row(4 * IDX_DIM)]
    return pl.pallas_call(
        _proj_ab_kernel,
        out_shape=out_shape,
        grid=(n // tm,),
        in_specs=[row(D_MODEL), grp, grp, pl.BlockSpec((1, D_MODEL), lambda i: (0, 0)),
                  full(wm), full(wih), full(wil)],
        out_specs=out_specs,
        compiler_params=_params("arbitrary"),
        name="proj_ab",
    )(x, sh, sc, gpre, wm, wih, wil)


def _lane_cumsum(x):
    lane = lax.broadcasted_iota(jnp.int32, x.shape, 1)
    s = 1
    while s < LANES:
        x = x + jnp.where(lane >= s, pltpu.roll(x, s, 1), 0.0)
        s *= 2
    return x


def _logf_kernel(*refs, n_cache, n_new):
    if n_cache:
        fl_ref, bf_ref, cache_ref, logf_ref, fnew_ref, fcache_ref = refs
    else:
        fl_ref, bf_ref, logf_ref, fnew_ref = refs
    carry = jnp.zeros((N_HEADS, 1), F32)
    for blk in range(n_cache):
        c = _lane_cumsum(cache_ref[blk]) + carry
        fcache_ref[blk] = c
        carry = c[:, LANES - 1:LANES]
    for blk in range(n_new):
        lf = jax.nn.log_sigmoid(fl_ref[blk] + bf_ref[...])
        logf_ref[blk] = lf
        c = _lane_cumsum(lf) + carry
        fnew_ref[blk] = c
        carry = c[:, LANES - 1:LANES]


def _blocked_rows(a):
    b, t, h = a.shape
    return a.reshape(b, t // LANES, LANES, h).transpose(0, 1, 3, 2)


def _unblocked_rows(a):
    b, nb, h, _ = a.shape
    return a.transpose(0, 1, 3, 2).reshape(b, nb * LANES, h)


def _logf(fl, b_f, cache_logf):
    b, t, _ = fl.shape
    n_new = t // LANES
    blk = lambda nb: pl.BlockSpec((None, nb, N_HEADS, LANES), lambda i: (i, 0, 0, 0))
    sds = lambda nb: jax.ShapeDtypeStruct((b, nb, N_HEADS, LANES), F32)
    ins = [_blocked_rows(fl), b_f.reshape(N_HEADS, 1)]
    in_specs = [blk(n_new), pl.BlockSpec((N_HEADS, 1), lambda i: (0, 0))]
    out_shape = [sds(n_new), sds(n_new)]
    out_specs = [blk(n_new), blk(n_new)]
    n_cache = 0
    if cache_logf is not None:
        n_cache = cache_logf.shape[1] // LANES
        ins.append(_blocked_rows(cache_logf))
        in_specs.append(blk(n_cache))
        out_shape.append(sds(n_cache))
        out_specs.append(blk(n_cache))
    outs = pl.pallas_call(
        functools.partial(_logf_kernel, n_cache=n_cache, n_new=n_new),
        out_shape=out_shape, grid=(b,), in_specs=in_specs, out_specs=out_specs,
        compiler_params=_params("arbitrary"), name="logf_cumsum",
    )(*ins)
    return outs


BIAS_FAR = 3


def _t5_bucket(rel):
    nb = REL_BUCKETS // 2
    max_exact = nb // 2
    base = jnp.where(rel > 0, nb, 0)
    n = jnp.abs(rel)
    nf = jnp.maximum(n, 1).astype(F32)
    large = max_exact + (jnp.log(nf / max_exact) / math.log(REL_MAX_DIST / max_exact)
                         * (nb - max_exact)).astype(jnp.int32)
    large = jnp.minimum(large, nb - 1)
    return base + jnp.where(n < max_exact, n, large)


def _bias_kernel(tab_ref, o_ref):
    tq = o_ref.shape[2]
    row = lax.broadcasted_iota(jnp.int32, (tq, LANES), 0)
    col = lax.broadcasted_iota(jnp.int32, (tq, LANES), 1)
    for t in range(o_ref.shape[0]):
        bucket = _t5_bucket(LANES * (t - BIAS_FAR) + col - row)
        for h in range(N_HEADS):
            acc = jnp.zeros((tq, LANES), F32)
            for b in range(REL_BUCKETS):
                acc = jnp.where(bucket == b, tab_ref[b, h], acc)
            o_ref[t, h] = acc


def _bias_tiles(rel_bias, tq):
    assert LANES * BIAS_FAR - (LANES - 1) >= REL_MAX_DIST
    n_tiles = BIAS_FAR + max(1, tq // LANES)
    return pl.pallas_call(
        _bias_kernel,
        out_shape=jax.ShapeDtypeStruct((n_tiles, N_HEADS, tq, LANES), F32),
        in_specs=[pl.BlockSpec(memory_space=pltpu.SMEM)],
        name="rel_bias_tiles",
    )(rel_bias)


def _sortable(x):
    b = pltpu.bitcast(x, jnp.int32)
    return b ^ ((b >> 31) & 0x7FFFFFFF)


def _split_keys(ki):
    hi = ki.astype(BF16).astype(F32)
    return jnp.concatenate([hi, ki - hi, hi, ki - hi], axis=-1).astype(BF16)


def _block_rows(blk, size):
    start = blk * size
    return pl.ds(start if isinstance(start, int) else pl.multiple_of(start, size), size)


def _pair_queries(q16):
    lane = lax.broadcasted_iota(jnp.int32, (q16.shape[0], LANES), 1)
    out = []
    for h in range(N_HEADS):
        pair = q16[:, LANES * (h // 2):LANES * (h // 2 + 1)].astype(F32) * HEAD_DIM ** -0.5
        own = (lane < HEAD_DIM) if h % 2 == 0 else (lane >= HEAD_DIM)
        out.append(jnp.where(own, pair, 0.0).astype(BF16))
    return jnp.stack(out)


def _pair_keys(x):
    x = x.astype(BF16)
    return jnp.stack([x[:, LANES * (h // 2):LANES * (h // 2 + 1)] for h in range(N_HEADS)])


def _pair_values(x):
    x = x.astype(BF16)
    lane = lax.broadcasted_iota(jnp.int32, (x.shape[0], LANES), 1)
    out = []
    for h in range(N_HEADS):
        pair = x[:, LANES * (h // 2):LANES * (h // 2 + 1)]
        own = (lane < HEAD_DIM) if h % 2 == 0 else (lane >= HEAD_DIM)
        out.append(jnp.where(own, pair, jnp.ones_like(pair)))
    return jnp.stack(out)


def _unpair(o):
    lane = lax.broadcasted_iota(jnp.int32, o.shape[1:], 1)
    return jnp.concatenate([jnp.where(lane < HEAD_DIM, o[2 * p], o[2 * p + 1]) for p in range(N_HEADS // 2)], axis=-1)


def _softmax_init(m_scr, acc_scr):
    m_scr[...] = jnp.full(m_scr.shape, NEG, F32)
    acc_scr[...] = jnp.zeros(acc_scr.shape, F32)


def _softmax_step(tiles, m_scr, acc_scr):
    m_prev = m_scr[...]
    top = tiles[0][0]
    for s, _ in tiles[1:]:
        top = jnp.maximum(top, s)
    m_new = jnp.maximum(m_prev, jnp.max(top, axis=-1, keepdims=True))
    acc = jnp.exp(m_prev - m_new) * acc_scr[...]
    for s, v in tiles:
        acc = acc + _bmm(jnp.exp(s - m_new).astype(BF16), v)
    acc_scr[...] = acc
    m_scr[...] = m_new


def _softmax_out(acc_scr):
    acc = acc_scr[...]
    den = jnp.stack([pltpu.roll(acc[h], HEAD_DIM, 1) for h in range(N_HEADS)])
    return _unpair(acc / den)


def _softmax_scratch(tq):
    return [pltpu.VMEM((N_HEADS, tq, LANES), F32), pltpu.VMEM((N_HEADS, tq, LANES), F32)]


def _dsa_kernel(*refs, tq, n_cache, past, t_new, top_k, dyn_new):
    if n_cache:
        (q_ref, qi_ref, mq_ref, cki_ref, cka_ref, cva_ref, nki_ref, nka_ref, nva_ref, bias_ref,
         o_ref, keys_ref, madd_ref, qcat_scr, wi_scr, m_scr, acc_scr) = refs
    else:
        (q_ref, qi_ref, mq_ref, nki_ref, nka_ref, nva_ref, bias_ref,
         o_ref, keys_ref, madd_ref, qcat_scr, wi_scr, m_scr, acc_scr) = refs
        cki_ref = cka_ref = cva_ref = None
    i = pl.program_id(1)
    n_new = (i + 1) * (tq // LANES) if dyn_new else 1
    n_blk = n_cache + n_new
    qpos0 = past + i * tq
    qpos = qpos0 + lax.broadcasted_iota(jnp.int32, (tq, 1), 0)
    col = lax.broadcasted_iota(jnp.int32, (tq, LANES), 1)
    n_keys = past + t_new

    qi = qi_ref[...]
    for h in range(N_HEADS):
        x = qi[:, h * IDX_DIM:(h + 1) * IDX_DIM]
        hi = x.astype(BF16).astype(F32)
        qcat_scr[h * tq:(h + 1) * tq, :] = jnp.concatenate([hi, hi, x - hi, jnp.zeros_like(x)], axis=-1).astype(BF16)
        wi_scr[h] = jnp.broadcast_to(mq_ref[:, IDX_DIM + h:IDX_DIM + h + 1], (tq, LANES))
    score_scale = IDX_DIM ** -0.5 * N_HEADS ** -0.5

    def score_block(kcat, kpos0, slot):
        s = _dot_nt(qcat_scr[...], kcat).reshape(N_HEADS, tq, LANES)
        acc = jnp.sum(jnp.maximum(s, 0.0) * wi_scr[...], axis=0)
        kpos = kpos0 + col
        adm = jnp.logical_and((kpos // CHUNK) <= (qpos // CHUNK), kpos < n_keys)
        keys_ref[slot] = jnp.where(adm, _sortable(acc * score_scale), INT_MIN)

    if n_cache:
        def cache_scores(j, _):
            r0 = pl.multiple_of(j * LANES, LANES)
            score_block(_split_keys(cki_ref[pl.ds(r0, LANES), :]), j * LANES, j)
            return 0
        lax.fori_loop(0, n_cache, cache_scores, 0)

    def new_scores(j, _):
        r0 = pl.multiple_of(j * LANES, LANES)
        score_block(nki_ref[pl.ds(r0, LANES), :], past + j * LANES, n_cache + j)
        return 0
    lax.fori_loop(0, n_new, new_scores, 0)

    def count(pred):
        def body(j, c):
            return c + jnp.where(pred(keys_ref[j]), 1.0, 0.0)
        c = lax.fori_loop(0, n_blk, body, jnp.zeros((tq, LANES), F32))
        return jnp.sum(c, axis=-1, keepdims=True)

    def search(b, ans):
        cand = ans ^ lax.shift_left(jnp.int32(1), jnp.int32(31) - jnp.asarray(b, jnp.int32))
        cand_b = jnp.broadcast_to(cand, (tq, LANES))
        cnt = count(lambda k: k >= cand_b)
        return jnp.where(cnt >= top_k, cand, ans)

    kth = lax.fori_loop(0, 32, search, jnp.full((tq, 1), INT_MIN, jnp.int32))
    kth_b = jnp.broadcast_to(jnp.maximum(kth, INT_MIN + 1), (tq, LANES))
    n_ge = count(lambda k: k >= kth_b)
    ties_overflow = jnp.max(n_ge) > top_k

    @pl.when(jnp.logical_not(ties_overflow))
    def _():
        def plain_mask(j, _):
            madd_ref[j] = jnp.where(keys_ref[j] >= kth_b, 0.0, NEG)
            return 0
        lax.fori_loop(0, n_blk, plain_mask, 0)

    @pl.when(ties_overflow)
    def _():
        n_gt = count(lambda k: k > kth_b)
        quota = top_k - n_gt
        tri = (lax.broadcasted_iota(jnp.int32, (LANES, LANES), 0)
               <= lax.broadcasted_iota(jnp.int32, (LANES, LANES), 1)).astype(BF16)

        def tie_mask(j, seen):
            k = keys_ref[j]
            eq = jnp.where(k == kth_b, 1.0, 0.0)
            rank = _dot(eq.astype(BF16), tri) + seen
            sel = jnp.where(k > kth_b, 1.0, jnp.where(rank <= quota, eq, 0.0))
            madd_ref[j] = (1.0 - sel) * NEG
            return seen + jnp.sum(eq, axis=-1, keepdims=True)
        lax.fori_loop(0, n_blk, tie_mask, jnp.zeros((tq, 1), F32))

    qm = _pair_queries(q_ref[...])
    _softmax_init(m_scr, acc_scr)

    def step(k_ref, v_ref, blocks, slot0, kblk0):
        tiles = []
        for blk in blocks:
            rows = _block_rows(blk, LANES)
            tile = jnp.clip(kblk0 + blk - qpos0 // LANES + BIAS_FAR, 0, bias_ref.shape[0] - 1)
            s = _bmm_nt(qm, _pair_keys(k_ref[rows, :])) + bias_ref[tile] + madd_ref[slot0 + blk][None]
            tiles.append((s, _pair_values(v_ref[rows, :])))
        _softmax_step(tiles, m_scr, acc_scr)

    if n_cache:
        def cache_step(j, _):
            step(cka_ref, cva_ref, (2 * j, 2 * j + 1), 0, 0)
            return 0
        lax.fori_loop(0, n_cache // 2, cache_step, 0)

    if dyn_new:
        @pl.when(n_new % 2 == 1)
        def _():
            madd_ref[n_blk] = jnp.full((tq, LANES), NEG, F32)

        def new_step(j, _):
            step(nka_ref, nva_ref, (2 * j, 2 * j + 1), n_cache, past // LANES)
            return 0
        lax.fori_loop(0, (n_new + 1) // 2, new_step, 0)
    else:
        step(nka_ref, nva_ref, (0,), n_cache, past // LANES)
    o_ref[...] = _softmax_out(acc_scr).astype(BF16)


def _dsa(q, qi, misc, kcat, ka16, va16, bias, cache, *, batch, t_q, tq):
    t_kv = ka16.shape[0] // batch
    nq = t_q // tq
    past = 0 if cache is None else cache[0].shape[1]
    n_cache = past // LANES
    n_keys = past + t_q
    top_k = min(IDX_TOPK_MAX, n_keys // 4)
    assert tq % LANES == 0 or nq == 1
    assert n_cache % 2 == 0 and (cache is not None or (t_kv // LANES) % 2 == 0)
    qrow = lambda w: pl.BlockSpec((tq, w), lambda b, i: (b * (t_kv // tq) + i, 0))
    krow = lambda w: pl.BlockSpec((t_kv, w), lambda b, i: (b, 0))
    crow = lambda w: pl.BlockSpec((None, past, w), lambda b, i: (b, 0, 0))
    ins = [q, qi, misc]
    in_specs = [qrow(D_HEADS), qrow(D_HEADS), qrow(LANES)]
    if cache is not None:
        ins += list(cache)
        in_specs += [crow(IDX_DIM), crow(D_HEADS), crow(D_HEADS)]
    ins += [kcat, ka16, va16, bias]
    in_specs += [krow(4 * IDX_DIM), krow(D_HEADS), krow(D_HEADS),
                 pl.BlockSpec(bias.shape, lambda b, i: (0, 0, 0, 0))]
    n_blk_max = n_cache + t_kv // LANES
    return pl.pallas_call(
        functools.partial(_dsa_kernel, tq=tq, n_cache=n_cache, past=past, t_new=t_q, top_k=top_k,
                          dyn_new=cache is None),
        out_shape=jax.ShapeDtypeStruct((batch * t_kv, D_HEADS), BF16),
        grid=(batch, nq),
        in_specs=in_specs,
        out_specs=qrow(D_HEADS),
        scratch_shapes=[pltpu.VMEM((n_blk_max, tq, LANES), jnp.int32),
                        pltpu.VMEM((n_blk_max, tq, LANES), F32),
                        pltpu.VMEM((N_HEADS * tq, 4 * IDX_DIM), BF16),
                        pltpu.VMEM((N_HEADS, tq, LANES), F32)] + _softmax_scratch(tq),
        compiler_params=_params("arbitrary", "arbitrary"),
        name="dsa_attention",
    )(*ins)


FOX_GROUP = 4
DSA_TQ = 2 * LANES
FOX_TQ = 4 * LANES


def _fox_kernel(*refs, tq, tk, tkc, n_cache, past, dyn_new):
    if n_cache:
        q_ref, fq_ref, ck_ref, cv_ref, cf_ref, nk_ref, nv_ref, nf_ref, o_ref, m_scr, acc_scr = refs
    else:
        q_ref, fq_ref, nk_ref, nv_ref, nf_ref, o_ref, m_scr, acc_scr = refs
        ck_ref = cv_ref = cf_ref = None
    i = pl.program_id(1)
    group = tq // tk if dyn_new else 1
    qpos = past + i * tq + lax.broadcasted_iota(jnp.int32, (1, tq, 1), 1)
    col = lax.broadcasted_iota(jnp.int32, (1, 1, LANES), 2)

    qm = _pair_queries(q_ref[...])
    fq = jnp.stack([fq_ref[:, h:h + 1] for h in range(N_HEADS)])
    _softmax_init(m_scr, acc_scr)

    def step(k_ref, v_ref, f_ref, blocks, size, kpos_base=None):
        tiles = []
        for blk in blocks:
            rows = _block_rows(blk, size)
            s = _bmm_nt(qm, _pair_keys(k_ref[rows, :])) + fq - f_ref[blk]
            if kpos_base is not None:
                s = jnp.where(kpos_base + blk * size + col <= qpos, s, NEG)
            tiles.append((s, _pair_values(v_ref[rows, :])))
        _softmax_step(tiles, m_scr, acc_scr)

    if n_cache:
        def cache_step(j, _):
            step(ck_ref, cv_ref, cf_ref, [FOX_GROUP * j + d for d in range(FOX_GROUP)], tkc)
            return 0
        lax.fori_loop(0, n_cache // FOX_GROUP, cache_step, 0)

    if dyn_new:
        def new_step(j, _):
            step(nk_ref, nv_ref, nf_ref, [group * j + d for d in range(group)], tk)
            return 0
        lax.fori_loop(0, i, new_step, 0)
        step(nk_ref, nv_ref, nf_ref, [group * i + d for d in range(group)], tk, past)
    else:
        step(nk_ref, nv_ref, nf_ref, [0], tk, past)
    o_ref[...] = _softmax_out(acc_scr).astype(BF16)


def _key_rows(f, tk):
    b, t, h = f.shape
    return f.reshape(b, t // tk, tk, h).transpose(0, 1, 3, 2)[:, :, :, None, :]


def _fox(q, fq, kb16, vb16, f_new, cache, *, batch, t_q, tq, tk, tkc):
    t_kv = kb16.shape[0] // batch
    nq = t_q // tq
    past = 0 if cache is None else cache[0].shape[1]
    n_cache = past // tkc
    assert tk == LANES and tkc == LANES and n_cache % FOX_GROUP == 0
    qrow = lambda w: pl.BlockSpec((tq, w), lambda b, i: (b * (t_kv // tq) + i, 0))
    krow = lambda w: pl.BlockSpec((t_kv, w), lambda b, i: (b, 0))
    crow = lambda w: pl.BlockSpec((None, past, w), lambda b, i: (b, 0, 0))
    fblk = lambda nb, w: pl.BlockSpec((None, nb, N_HEADS, 1, w), lambda b, i: (b, 0, 0, 0, 0))
    ins = [q, fq]
    in_specs = [qrow(D_HEADS), qrow(N_HEADS)]
    if cache is not None:
        ins += [cache[0], cache[1], _key_rows(cache[2], tkc)]
        in_specs += [crow(D_HEADS), crow(D_HEADS), fblk(n_cache, tkc)]
    ins += [kb16, vb16, _key_rows(f_new, tk)]
    in_specs += [krow(D_HEADS), krow(D_HEADS), fblk(t_kv // tk, tk)]
    return pl.pallas_call(
        functools.partial(_fox_kernel, tq=tq, tk=tk, tkc=tkc, n_cache=n_cache, past=past, dyn_new=cache is None),
        out_shape=jax.ShapeDtypeStruct((batch * t_kv, D_HEADS), BF16),
        grid=(batch, nq),
        in_specs=in_specs,
        out_specs=qrow(D_HEADS),
        scratch_shapes=_softmax_scratch(tq),
        compiler_params=_params("arbitrary", "arbitrary"),
        name="fox_attention",
    )(*ins)


def _out_ab_kernel(oa_ref, ob_ref, w_ref, x_ref, gt_ref, gpost_ref, o_ref):
    y = _dot(oa_ref[...], w_ref[:D_HEADS, :]) + _dot(ob_ref[...], w_ref[D_HEADS:, :])
    o_ref[...] = _residual(x_ref[...], y, gpost_ref[...], gt_ref[...], 1.0)


def _out_ab(oa, ob, w, x, gt, gpost, *, batch, t_q, tm):
    t_kv = oa.shape[0] // batch
    nt = t_q // tm
    arow = pl.BlockSpec((tm, D_HEADS), lambda b, i: (b * (t_kv // tm) + i, 0))
    xrow = pl.BlockSpec((tm, D_MODEL), lambda b, i: (b * nt + i, 0))
    return pl.pallas_call(
        _out_ab_kernel,
        out_shape=jax.ShapeDtypeStruct(x.shape, F32),
        grid=(batch, nt),
        in_specs=[arow, arow, pl.BlockSpec(w.shape, lambda b, i: (0, 0)), xrow,
                  pl.BlockSpec((tm // GROUP, 1, D_MODEL), lambda b, i: (b * nt + i, 0, 0)),
                  pl.BlockSpec((1, D_MODEL), lambda b, i: (0, 0))],
        out_specs=xrow,
        compiler_params=_params("arbitrary", "arbitrary"),
        name="out_ab",
    )(oa, ob, w, x, gt, gpost)


GDN_MISC0 = C_QKV + N_HEADS * DK_C


def _proj_gdn_kernel(x_ref, sh_ref, sc_ref, gpre_ref, w_ref, cw_ref, prev_ref, alog_ref, dtb_ref,
                     q_ref, k_ref, v_ref, gz_ref, misc_ref, tail_ref, zbuf):
    tm = x_ref.shape[0]
    halo = CONV_W - 1
    base = 8

    @pl.when(pl.program_id(1) == 0)
    def _():
        zbuf[base - halo:base, :] = prev_ref[...]

    h = _modulated(x_ref[...], gpre_ref[...], sh_ref[...], sc_ref[...]).astype(BF16)
    zbuf[base:base + tm, :] = _dot(h, w_ref[:, :C_QKV])
    gz_ref[...] = _dot(h, w_ref[:, C_QKV:GDN_MISC0])
    zm = _dot(h, w_ref[:, GDN_MISC0:])
    lane = lax.broadcasted_iota(jnp.int32, zm.shape, 1)
    beta = jax.nn.sigmoid(zm)
    g = -jnp.exp(alog_ref[...]) * jax.nn.softplus(zm + dtb_ref[...])
    misc_ref[...] = jnp.where(lane < N_HEADS, beta, g)

    for part, ref in enumerate((q_ref, k_ref, v_ref)):
        for hd in range(N_HEADS):
            c0 = part * N_HEADS * DK_C + hd * DK_C
            cols = slice(c0, c0 + DK_C)
            conv = zbuf[base:base + tm, cols] * cw_ref[halo:halo + 1, cols]
            for j in range(halo):
                conv = conv + zbuf[base - halo + j:base - halo + j + tm, cols] * cw_ref[j:j + 1, cols]
            a = _silu(conv)
            if part < 2:
                a = a * lax.rsqrt(jnp.sum(a * a, axis=-1, keepdims=True) + EPS)
                if part == 0:
                    a = a * DK_C ** -0.5
            ref[:, hd * DK_C:(hd + 1) * DK_C] = a
    tail = zbuf[base + tm - halo:base + tm, :]
    tail_ref[...] = tail
    zbuf[base - halo:base, :] = tail


def _proj_gdn(x, sh, sc, gpre, w, conv_w, conv_prev, alog_row, dtb_row, *, batch, tm):
    n = x.shape[0]
    t = n // batch
    nt = t // tm
    row = lambda wd: pl.BlockSpec((tm, wd), lambda b, i: (b * nt + i, 0))
    full = lambda a: pl.BlockSpec(a.shape, lambda b, i: (0,) * a.ndim)
    sds = lambda wd: jax.ShapeDtypeStruct((n, wd), F32)
    hd = N_HEADS * DK_C
    tail = pl.BlockSpec((None, CONV_W - 1, C_QKV), lambda b, i: (b, 0, 0))
    return pl.pallas_call(
        _proj_gdn_kernel,
        out_shape=[sds(hd), sds(hd), sds(hd), sds(hd), sds(LANES),
                   jax.ShapeDtypeStruct((batch, CONV_W - 1, C_QKV), F32)],
        grid=(batch, nt),
        in_specs=[row(D_MODEL),
                  pl.BlockSpec((tm // GROUP, 1, D_MODEL), lambda b, i: (b * nt + i, 0, 0)),
                  pl.BlockSpec((tm // GROUP, 1, D_MODEL), lambda b, i: (b * nt + i, 0, 0)),
                  pl.BlockSpec((1, D_MODEL), lambda b, i: (0, 0)),
                  full(w), full(conv_w), tail, full(alog_row), full(dtb_row)],
        out_specs=[row(hd), row(hd), row(hd), row(hd), row(LANES), tail],
        scratch_shapes=[pltpu.VMEM((8 + tm, C_QKV), F32)],
        compiler_params=_params("arbitrary", "arbitrary"),
        name="proj_gdn",
    )(x, sh, sc, gpre, w, conv_w, conv_prev, alog_row, dtb_row)


def _chunk_cumsum_kernel(g_ref, o_ref, *, chunk):
    x = g_ref[...]
    lane = lax.broadcasted_iota(jnp.int32, x.shape, 1) % chunk
    s = 1
    while s < chunk:
        x = x + jnp.where(lane >= s, pltpu.roll(x, s, 1), 0.0)
        s *= 2
    o_ref[...] = x


def _chunk_cumsum(g_rows, chunk):
    assert LANES % chunk == 0
    tiles = g_rows.reshape(-1, LANES)
    return pl.pallas_call(
        functools.partial(_chunk_cumsum_kernel, chunk=chunk),
        out_shape=jax.ShapeDtypeStruct(tiles.shape, F32),
        name="gdn_chunk_cumsum",
    )(tiles).reshape(g_rows.shape)


def _gdn_kernel(q_ref, k_ref, v_ref, col_ref, row_ref, s0_ref, o_ref, s_ref):
    c = q_ref.shape[0]

    @pl.when(pl.program_id(1) == 0)
    def _():
        s_ref[...] = s0_ref[...]

    heads = lambda ref: jnp.stack([ref[:, h * DK_C:(h + 1) * DK_C] for h in range(N_HEADS)])
    q, k, v = heads(q_ref), heads(k_ref), heads(v_ref)
    beta = jnp.stack([col_ref[:, h:h + 1] for h in range(N_HEADS)])
    gc = jnp.stack([col_ref[:, N_HEADS + h:N_HEADS + h + 1] for h in range(N_HEADS)])
    gc_row = row_ref[...]
    gc_last = gc[:, c - 1:c, :]
    ri = lax.broadcasted_iota(jnp.int32, (1, c, c), 1)
    ci = lax.broadcasted_iota(jnp.int32, (1, c, c), 2)
    causal = ri >= ci
    strict = ri > ci
    eye = jnp.where(ri == ci, 1.0, 0.0)
    decay = jnp.exp(jnp.where(causal, gc - gc_row, NEG))
    k16 = k.astype(BF16)
    a = jnp.where(strict, beta * _bmm_nt(k16, k16) * decay, 0.0)
    inv = eye - a
    p = a
    n = 2
    while n < c:
        p = _dot3(p, p, _bmm)
        inv = inv + _dot3(inv, p, _bmm)
        n *= 2
    sol = _dot3(inv, jnp.concatenate([v * beta, k * (beta * jnp.exp(gc))], axis=-1), _bmm)
    u, w = sol[:, :, :DK_C], sol[:, :, DK_C:]
    qk = jnp.where(causal, _bmm_nt(q.astype(BF16), k16) * decay, 0.0)
    q_dec = (q * jnp.exp(gc)).astype(BF16)
    k_dec = k * jnp.exp(gc_last - gc)
    k_dec_t = jnp.stack([k_dec[h].T for h in range(N_HEADS)]).astype(BF16)
    s = s_ref[...]
    s16 = s.astype(BF16)
    v16 = (u - _bmm(w.astype(BF16), s16)).astype(BF16)
    o = _bmm(q_dec, s16) + _bmm(qk.astype(BF16), v16)
    for h in range(N_HEADS):
        o_ref[:, h * DK_C:(h + 1) * DK_C] = o[h]
    s_ref[...] = s * jnp.exp(gc_last) + _bmm(k_dec_t, v16)


def _gdn(q, k, v, col, row, s0, *, batch, chunk):
    n = q.shape[0]
    nc = n // batch // chunk
    hd = N_HEADS * DK_C
    blk = lambda w: pl.BlockSpec((chunk, w), lambda b, i: (b * nc + i, 0))
    state = pl.BlockSpec((None, N_HEADS, DK_C, DK_C), lambda b, i: (b, 0, 0, 0))
    return pl.pallas_call(
        _gdn_kernel,
        out_shape=[jax.ShapeDtypeStruct((n, hd), F32), jax.ShapeDtypeStruct(s0.shape, F32)],
        grid=(batch, nc),
        in_specs=[blk(hd), blk(hd), blk(hd), blk(2 * N_HEADS),
                  pl.BlockSpec((None, N_HEADS, 1, chunk), lambda b, i: (b * nc + i, 0, 0, 0)), state],
        out_specs=[blk(hd), state],
        compiler_params=_params("arbitrary", "arbitrary"),
        name="gated_delta_rule",
    )(q, k, v, col, row, s0)


def _out_gdn_kernel(o_ref, gz_ref, nw_ref, w_ref, x_ref, gt_ref, gpost_ref, out_ref):
    parts = []
    for h in range(N_HEADS):
        cols = slice(h * DK_C, (h + 1) * DK_C)
        o = o_ref[:, cols]
        o = o * lax.rsqrt(jnp.mean(o * o, axis=-1, keepdims=True) + EPS) * nw_ref[...]
        parts.append((o * _silu(gz_ref[:, cols])).astype(BF16))
    y = _dot(jnp.concatenate(parts, axis=-1), w_ref[...])
    out_ref[...] = _residual(x_ref[...], y, gpost_ref[...], gt_ref[...], 1.0)


def _out_gdn(o, gz, norm_w, w, x, gt, gpost, tm=256):
    n = x.shape[0]
    tm = min(tm, n)
    row = pl.BlockSpec((tm, D_MODEL), lambda i: (i, 0))
    return pl.pallas_call(
        _out_gdn_kernel,
        out_shape=jax.ShapeDtypeStruct(x.shape, F32),
        grid=(n // tm,),
        in_specs=[row, row, pl.BlockSpec((1, DK_C), lambda i: (0, 0)), pl.BlockSpec(w.shape, lambda i: (0, 0)),
                  row, pl.BlockSpec((tm // GROUP, 1, D_MODEL), lambda i: (i, 0, 0)),
                  pl.BlockSpec((1, D_MODEL), lambda i: (0, 0))],
        out_specs=row,
        compiler_params=_params("arbitrary"),
        name="out_gdn",
    )(o, gz, norm_w, w, x, gt, gpost)


def _mixer_ab(x, mod, w, past, *, batch, t, bias):
    sh, sc, gt = mod
    (qa, qb, ka, va, kb, vb, ka16, va16, kb16, vb16, qi, misc, kcat) = _proj_ab(
        x, sh, sc, w["pre"], w["ab_main"], w["ab_idx_hi"], w["ab_idx_lo"])
    ki = misc[:, :IDX_DIM]
    fl = misc[:, IDX_DIM + N_HEADS:IDX_DIM + 2 * N_HEADS].reshape(batch, t, N_HEADS)
    t_kv = -(-t // LANES) * LANES
    pad_rows = lambda a: a if t_kv == t else jnp.pad(
        a.reshape(batch, t, -1), ((0, 0), (0, t_kv - t), (0, 0))).reshape(batch * t_kv, -1)
    fl_p = fl if t_kv == t else jnp.pad(fl, ((0, 0), (0, t_kv - t), (0, 0)))
    if past is None:
        logf_b, f_new = _logf(fl_p, w["ab_b_f"], None)
        cache_a = cache_b = None
    else:
        cak, cav, caik, cbk, cbv, cblf = past
        p = cak.shape[1]
        logf_b, f_new, f_cache = _logf(fl_p, w["ab_b_f"], cblf)
        cache_a = (caik, cak.reshape(batch, p, D_HEADS), cav.reshape(batch, p, D_HEADS))
        cache_b = (cbk.reshape(batch, p, D_HEADS), cbv.reshape(batch, p, D_HEADS), _unblocked_rows(f_cache))
    logf = _unblocked_rows(logf_b)[:, :t]
    f_new = _unblocked_rows(f_new)
    o_a = _dsa(pad_rows(qa), pad_rows(qi), pad_rows(misc), pad_rows(kcat), pad_rows(ka16), pad_rows(va16), bias, cache_a,
               batch=batch, t_q=t, tq=bias.shape[2])
    o_b = _fox(pad_rows(qb), f_new.reshape(batch * t_kv, N_HEADS), pad_rows(kb16), pad_rows(vb16), f_new, cache_b,
               batch=batch, t_q=t, tq=min(FOX_TQ, t), tk=LANES, tkc=LANES)
    x = _out_ab(o_a, o_b, w["ab_w_out"], x, gt, w["post"], batch=batch, t_q=t, tm=min(256, t))
    hd = (batch, t, N_HEADS, HEAD_DIM)
    state = (ka.reshape(hd), va.reshape(hd), ki.reshape(batch, t, IDX_DIM), kb.reshape(hd), vb.reshape(hd), logf)
    return x, state


def _mixer_gdn(x, mod, w, past, *, batch, t):
    sh, sc, gt = mod
    if past is None:
        conv_prev = jnp.zeros((batch, CONV_W - 1, C_QKV), F32)
        s0 = jnp.zeros((batch, N_HEADS, DK_C, DK_C), F32)
    else:
        conv_prev, s0 = past
    q, k, v, gz, misc, new_conv = _proj_gdn(x, sh, sc, w["pre"], w["gdn_w_in"], w["gdn_conv_w"], conv_prev,
                                            w["gdn_alog_row"], w["gdn_dtb_row"], batch=batch, tm=min(256, t))
    chunk = min(t, CHUNK)
    n = batch * t
    gc_rows = _chunk_cumsum(misc[:, N_HEADS:2 * N_HEADS].T, chunk)
    col = jnp.concatenate([misc[:, :N_HEADS], gc_rows.T], axis=-1)
    row = gc_rows.reshape(N_HEADS, n // chunk, chunk).transpose(1, 0, 2)[:, :, None, :]
    o, s_new = _gdn(q, k, v, col, row, s0, batch=batch, chunk=chunk)
    x = _out_gdn(o, gz, w["gdn_norm_w"], w["gdn_w_out"], x, gt, w["post"])
    return x, (new_conv, s_new)


def _forward(x, mod_all, w, past, *, batch, t):
    new_state = []
    groups = lambda v: jnp.repeat(v, t // GROUP, axis=0)[:, None, :]
    for l in range(mod_all.shape[0]):
        m = mod_all[l].reshape(batch, 3, 3, D_MODEL)
        mod = [[groups(m[:, s, j]) for j in range(3)] for s in range(3)]
        pre = lambda s: w["norm_pre"][l, s][None, :]
        post = lambda s: w["norm_post"][l, s][None, :]
        x = _ffn(x, mod[0], pre(0), post(0), w["ffn_w_gate"][l, 0], w["ffn_w_up"][l, 0], w["ffn_w_down"][l, 0])
        wl = dict(w, pre=pre(1), post=post(1))
        if l % 2 == 0:
            x, st = _mixer_ab(x, mod[1], wl, None if past is None else past[:6], batch=batch, t=t,
                              bias=w["bias_tiles"][min(DSA_TQ, t)])
        else:
            x, st = _mixer_gdn(x, mod[1], wl, None if past is None else past[6:], batch=batch, t=t)
        new_state.extend(st)
        x = _ffn(x, mod[2], pre(2), post(2), w["ffn_w_gate"][l, 1], w["ffn_w_up"][l, 1], w["ffn_w_down"][l, 1])
    return x, new_state


def _prepare_weights(rel_bias, ab_w_in, ab_b_f, ab_w_out, gdn_w_in, gdn_conv_w, gdn_A_log, gdn_dt_bias,
                     gdn_norm_w, gdn_w_out, norm_pre, norm_post, ffn_w_gate, ffn_w_up, ffn_w_down, tqs):
    o = [0]
    for s in (D_HEADS, D_HEADS, D_HEADS, D_HEADS, IDX_DIM, N_HEADS, D_HEADS, D_HEADS, D_HEADS, N_HEADS):
        o.append(o[-1] + s)
    cols = lambda i: ab_w_in[:, o[i]:o[i + 1]]
    qa, ka, va, qi, ki, wi, qb, kb, vb, fl = (cols(i) for i in range(10))
    idx = jnp.concatenate([qi, ki, wi, fl, jnp.zeros((D_MODEL, IDX_COLS - D_HEADS - IDX_DIM - 2 * N_HEADS), F32)], 1)
    idx_hi = idx.astype(BF16)
    gdn_in = jnp.concatenate([gdn_w_in, jnp.zeros((D_MODEL, LANES - 2 * N_HEADS), F32)], 1)
    lane_row = lambda v: jnp.concatenate([jnp.zeros((N_HEADS,), F32), v.astype(F32),
                                          jnp.zeros((LANES - 2 * N_HEADS,), F32)])[None, :]
    return dict(
        ab_main=jnp.concatenate([qa, qb, ka, va, kb, vb], 1).astype(BF16),
        ab_idx_hi=idx_hi, ab_idx_lo=(idx - idx_hi.astype(F32)).astype(BF16),
        ab_b_f=ab_b_f, ab_w_out=ab_w_out.astype(BF16),
        gdn_w_in=gdn_in.astype(BF16), gdn_conv_w=gdn_conv_w,
        gdn_alog_row=lane_row(gdn_A_log), gdn_dtb_row=lane_row(gdn_dt_bias),
        gdn_norm_w=gdn_norm_w[None, :], gdn_w_out=gdn_w_out.astype(BF16),
        norm_pre=norm_pre, norm_post=norm_post,
        ffn_w_gate=ffn_w_gate.astype(BF16), ffn_w_up=ffn_w_up.astype(BF16), ffn_w_down=ffn_w_down.astype(BF16),
        bias_tiles={tq: _bias_tiles(rel_bias, tq) for tq in tqs},
    )


def kernel(x_prompt, x_sample, cache_a_k, cache_a_v, cache_a_idx_k, cache_b_k, cache_b_v, cache_b_logf,
           state_c_conv, state_c_rec, c_prompt, c_sample, rel_bias, ab_w_in, ab_b_f, ab_w_out,
           gdn_w_in, gdn_conv_w, gdn_A_log, gdn_dt_bias, gdn_norm_w, gdn_w_out, norm_pre, norm_post,
           ada_w, ada_b, ffn_w_gate, ffn_w_up, ffn_w_down):
    bp, tp, _ = x_prompt.shape
    bs, ts, _ = x_sample.shape
    w = _prepare_weights(rel_bias, ab_w_in, ab_b_f, ab_w_out, gdn_w_in, gdn_conv_w, gdn_A_log, gdn_dt_bias,
                         gdn_norm_w, gdn_w_out, norm_pre, norm_post, ffn_w_gate, ffn_w_up, ffn_w_down,
                         tqs={min(DSA_TQ, tp), min(DSA_TQ, ts)})
    mod = _ada(jnp.concatenate([c_prompt, c_sample], 0), ada_w, ada_b)
    y_p, st_p = _forward(x_prompt.reshape(bp * tp, D_MODEL), mod[:, :bp], w, None, batch=bp, t=tp)
    past = (cache_a_k, cache_a_v, cache_a_idx_k, cache_b_k, cache_b_v, cache_b_logf, state_c_conv, state_c_rec)
    y_s, st_s = _forward(x_sample.reshape(bs * ts, D_MODEL), mod[:, bp:], w, past, batch=bs, t=ts)
    return (y_p.reshape(bp, tp, D_MODEL), y_s.reshape(bs, ts, D_MODEL), *st_p, *st_s)
```

```python
import functools
import math

import jax
import jax.numpy as jnp
from jax import lax
from jax.experimental import pallas as pl
from jax.experimental.pallas import tpu as pltpu

D_MODEL = 1024
D_FF = 4 * D_MODEL
CHUNK = 64
HEAD_DIM = 64
N_HEADS = 8
D_HEADS = N_HEADS * HEAD_DIM
IDX_DIM = 64
IDX_TOPK_MAX = 256
DK_C = 128
CONV_W = 4
C_QKV = 3 * N_HEADS * DK_C
REL_BUCKETS = 32
REL_MAX_DIST = 256
FFN_RES = 0.5
EPS = 1e-6

GROUP = 32
LANES = 128
NEG = -1e30
INT_MIN = -(2 ** 31)
VMEM_LIMIT = 56 * 1024 * 1024

F32 = jnp.float32
BF16 = jnp.bfloat16


def _params(*sem):
    return pltpu.CompilerParams(dimension_semantics=sem, vmem_limit_bytes=VMEM_LIMIT)


def _dot(a, b):
    return jnp.dot(a, b, preferred_element_type=F32)


def _dot_nt(a, b):
    return lax.dot_general(a, b, (((1,), (1,)), ((), ())), preferred_element_type=F32)


def _bmm(a, b):
    return lax.dot_general(a, b, (((2,), (1,)), ((0,), (0,))), preferred_element_type=F32)


def _bmm_nt(a, b):
    return lax.dot_general(a, b, (((2,), (2,)), ((0,), (0,))), preferred_element_type=F32)


def _split(x):
    hi = x.astype(BF16)
    lo = (x - hi.astype(F32)).astype(BF16)
    return hi, lo


def _dot3(a, b, mm=_dot):
    ah, al = _split(a)
    bh, bl = _split(b)
    return mm(ah, bh) + mm(al, bh) + mm(ah, bl)


def _silu(x):
    return x * jax.nn.sigmoid(x)


def _modulated(x, gpre, sh, sc):
    r = lax.rsqrt(jnp.mean(x * x, axis=-1, keepdims=True) + EPS)
    y = (x * r * gpre).reshape(sh.shape[0], GROUP, x.shape[-1])
    return (y * (1.0 + sc) + sh).reshape(x.shape)


def _residual(x, y, gpost, gate, weight):
    r = lax.rsqrt(jnp.mean(y * y, axis=-1, keepdims=True) + EPS)
    yn = (y * r * gpost).reshape(gate.shape[0], GROUP, x.shape[-1])
    return x + (weight * gate * yn).reshape(x.shape)


def _ada_kernel(c_ref, w_ref, b_ref, o_ref):
    o_ref[...] = _dot3(_silu(c_ref[...]), w_ref[...]) + b_ref[...]


def _ada(c_all, ada_w, ada_b):
    depth, _, n_out = ada_w.shape
    nb = c_all.shape[0]
    tn = 1024
    return pl.pallas_call(
        _ada_kernel,
        out_shape=jax.ShapeDtypeStruct((depth, nb, n_out), F32),
        grid=(depth, n_out // tn),
        in_specs=[pl.BlockSpec((nb, D_MODEL), lambda l, j: (0, 0)),
                  pl.BlockSpec((None, D_MODEL, tn), lambda l, j: (l, 0, j)),
                  pl.BlockSpec((None, 1, tn), lambda l, j: (l, 0, j))],
        out_specs=pl.BlockSpec((None, nb, tn), lambda l, j: (l, 0, j)),
        compiler_params=_params("arbitrary", "arbitrary"),
        name="ada_mod",
    )(c_all, ada_w, ada_b.reshape(depth, 1, n_out))


def _ffn_kernel(x_ref, sh_ref, sc_ref, gt_ref, gpre_ref, gpost_ref, wg_ref, wu_ref, wd_ref, o_ref,
                h_scr, acc_scr):
    j = pl.program_id(1)

    @pl.when(j == 0)
    def _():
        h_scr[...] = _modulated(x_ref[...], gpre_ref[...], sh_ref[...], sc_ref[...]).astype(BF16)
        acc_scr[...] = jnp.zeros_like(acc_scr)

    h = h_scr[...]
    a = _dot(h, wg_ref[...])
    b = _dot(h, wu_ref[...])
    acc_scr[...] += _dot((_silu(a) * b).astype(BF16), wd_ref[...])

    @pl.when(j == pl.num_programs(1) - 1)
    def _():
        o_ref[...] = _residual(x_ref[...], acc_scr[...], gpost_ref[...], gt_ref[...], FFN_RES)


def _ffn(x, mod, gpre, gpost, wg, wu, wd, tm=512, tf=512):
    n = x.shape[0]
    tm = min(tm, n)
    sh, sc, gt = mod
    row = pl.BlockSpec((tm, D_MODEL), lambda i, j: (i, 0))
    grp = pl.BlockSpec((tm // GROUP, 1, D_MODEL), lambda i, j: (i, 0, 0))
    vec = pl.BlockSpec((1, D_MODEL), lambda i, j: (0, 0))
    return pl.pallas_call(
        _ffn_kernel,
        out_shape=jax.ShapeDtypeStruct((n, D_MODEL), F32),
        grid=(n // tm, D_FF // tf),
        in_specs=[row, grp, grp, grp, vec, vec,
                  pl.BlockSpec((D_MODEL, tf), lambda i, j: (0, j)),
                  pl.BlockSpec((D_MODEL, tf), lambda i, j: (0, j)),
                  pl.BlockSpec((tf, D_MODEL), lambda i, j: (j, 0))],
        out_specs=row,
        scratch_shapes=[pltpu.VMEM((tm, D_MODEL), BF16), pltpu.VMEM((tm, D_MODEL), F32)],
        compiler_params=_params("arbitrary", "arbitrary"),
        name="ffn_sublayer",
    )(x, sh, sc, gt, gpre, gpost, wg, wu, wd)


AB_MAIN = ("qa", "qb", "ka", "va", "kb", "vb")
IDX_COLS = D_HEADS + LANES


def _proj_ab_kernel(x_ref, sh_ref, sc_ref, gpre_ref, wm_ref, wih_ref, wil_ref,
                    qa_ref, qb_ref, ka_ref, va_ref, kb_ref, vb_ref,
                    ka16_ref, va16_ref, kb16_ref, vb16_ref, qi_ref, misc_ref, kcat_ref):
    h = _modulated(x_ref[...], gpre_ref[...], sh_ref[...], sc_ref[...])
    hi, lo = _split(h)
    for n, ref in enumerate((qa_ref, qb_ref)):
        ref[...] = _dot(hi, wm_ref[:, n * D_HEADS:(n + 1) * D_HEADS]).astype(BF16)
    for n, (ref, ref16) in enumerate(((ka_ref, ka16_ref), (va_ref, va16_ref),
                                      (kb_ref, kb16_ref), (vb_ref, vb16_ref))):
        z = _dot(hi, wm_ref[:, (n + 2) * D_HEADS:(n + 3) * D_HEADS])
        ref[...] = z
        ref16[...] = z.astype(BF16)
    z = _dot(hi, wih_ref[...]) + _dot(lo, wih_ref[...]) + _dot(hi, wil_ref[...])
    qi_ref[...] = z[:, :D_HEADS]
    misc_ref[...] = z[:, D_HEADS:]
    kcat_ref[...] = _split_keys(z[:, D_HEADS:D_HEADS + IDX_DIM])


def _proj_ab(x, sh, sc, gpre, wm, wih, wil, tm=256):
    n = x.shape[0]
    tm = min(tm, n)
    row = lambda w: pl.BlockSpec((tm, w), lambda i: (i, 0))
    full = lambda a: pl.BlockSpec(a.shape, lambda i: (0, 0))
    grp = pl.BlockSpec((tm // GROUP, 1, D_MODEL), lambda i: (i, 0, 0))
    sds = lambda w, dt: jax.ShapeDtypeStruct((n, w), dt)
    out_shape = ([sds(D_HEADS, BF16)] * 2 + [sds(D_HEADS, F32)] * 4 + [sds(D_HEADS, BF16)] * 4
                 + [sds(D_HEADS, F32), sds(LANES, F32), sds(4 * IDX_DIM, BF16)])
    out_specs = [row(D_HEADS)] * 11 + [row(LANES), row(4 * IDX_DIM)]
    return pl.pallas_call(
        _proj_ab_kernel,
        out_shape=out_shape,
        grid=(n // tm,),
        in_specs=[row(D_MODEL), grp, grp, pl.BlockSpec((1, D_MODEL), lambda i: (0, 0)),
                  full(wm), full(wih), full(wil)],
        out_specs=out_specs,
        compiler_params=_params("arbitrary"),
        name="proj_ab",
    )(x, sh, sc, gpre, wm, wih, wil)


def _lane_cumsum(x):
    lane = lax.broadcasted_iota(jnp.int32, x.shape, 1)
    s = 1
    while s < LANES:
        x = x + jnp.where(lane >= s, pltpu.roll(x, s, 1), 0.0)
        s *= 2
    return x


def _logf_kernel(*refs, n_cache, n_new):
    if n_cache:
        fl_ref, bf_ref, cache_ref, logf_ref, fnew_ref, fcache_ref = refs
    else:
        fl_ref, bf_ref, logf_ref, fnew_ref = refs
    carry = jnp.zeros((N_HEADS, 1), F32)
    for blk in range(n_cache):
        c = _lane_cumsum(cache_ref[blk]) + carry
        fcache_ref[blk] = c
        carry = c[:, LANES - 1:LANES]
    for blk in range(n_new):
        lf = jax.nn.log_sigmoid(fl_ref[blk] + bf_ref[...])
        logf_ref[blk] = lf
        c = _lane_cumsum(lf) + carry
        fnew_ref[blk] = c
        carry = c[:, LANES - 1:LANES]


def _blocked_rows(a):
    b, t, h = a.shape
    return a.reshape(b, t // LANES, LANES, h).transpose(0, 1, 3, 2)


def _unblocked_rows(a):
    b, nb, h, _ = a.shape
    return a.transpose(0, 1, 3, 2).reshape(b, nb * LANES, h)


def _logf(fl, b_f, cache_logf):
    b, t, _ = fl.shape
    n_new = t // LANES
    blk = lambda nb: pl.BlockSpec((None, nb, N_HEADS, LANES), lambda i: (i, 0, 0, 0))
    sds = lambda nb: jax.ShapeDtypeStruct((b, nb, N_HEADS, LANES), F32)
    ins = [_blocked_rows(fl), b_f.reshape(N_HEADS, 1)]
    in_specs = [blk(n_new), pl.BlockSpec((N_HEADS, 1), lambda i: (0, 0))]
    out_shape = [sds(n_new), sds(n_new)]
    out_specs = [blk(n_new), blk(n_new)]
    n_cache = 0
    if cache_logf is not None:
        n_cache = cache_logf.shape[1] // LANES
        ins.append(_blocked_rows(cache_logf))
        in_specs.append(blk(n_cache))
        out_shape.append(sds(n_cache))
        out_specs.append(blk(n_cache))
    outs = pl.pallas_call(
        functools.partial(_logf_kernel, n_cache=n_cache, n_new=n_new),
        out_shape=out_shape, grid=(b,), in_specs=in_specs, out_specs=out_specs,
        compiler_params=_params("arbitrary"), name="logf_cumsum",
    )(*ins)
    return outs


BIAS_FAR = 3


def _t5_bucket(rel):
    nb = REL_BUCKETS // 2
    max_exact = nb // 2
    base = jnp.where(rel > 0, nb, 0)
    n = jnp.abs(rel)
    nf = jnp.maximum(n, 1).astype(F32)
    large = max_exact + (jnp.log(nf / max_exact) / math.log(REL_MAX_DIST / max_exact)
                         * (nb - max_exact)).astype(jnp.int32)
    large = jnp.minimum(large, nb - 1)
    return base + jnp.where(n < max_exact, n, large)


def _bias_kernel(tab_ref, o_ref):
    tq = o_ref.shape[2]
    row = lax.broadcasted_iota(jnp.int32, (tq, LANES), 0)
    col = lax.broadcasted_iota(jnp.int32, (tq, LANES), 1)
    for t in range(o_ref.shape[0]):
        bucket = _t5_bucket(LANES * (t - BIAS_FAR) + col - row)
        for h in range(N_HEADS):
            acc = jnp.zeros((tq, LANES), F32)
            for b in range(REL_BUCKETS):
                acc = jnp.where(bucket == b, tab_ref[b, h], acc)
            o_ref[t, h] = acc


def _bias_tiles(rel_bias, tq):
    assert LANES * BIAS_FAR - (LANES - 1) >= REL_MAX_DIST
    n_tiles = BIAS_FAR + max(1, tq // LANES)
    return pl.pallas_call(
        _bias_kernel,
        out_shape=jax.ShapeDtypeStruct((n_tiles, N_HEADS, tq, LANES), F32),
        in_specs=[pl.BlockSpec(memory_space=pltpu.SMEM)],
        name="rel_bias_tiles",
    )(rel_bias)


def _sortable(x):
    b = pltpu.bitcast(x, jnp.int32)
    return b ^ ((b >> 31) & 0x7FFFFFFF)


def _split_keys(ki):
    hi = ki.astype(BF16).astype(F32)
    return jnp.concatenate([hi, ki - hi, hi, ki - hi], axis=-1).astype(BF16)


def _block_rows(blk, size):
    start = blk * size
    return pl.ds(start if isinstance(start, int) else pl.multiple_of(start, size), size)


def _pair_queries(q16):
    lane = lax.broadcasted_iota(jnp.int32, (q16.shape[0], LANES), 1)
    out = []
    for h in range(N_HEADS):
        pair = q16[:, LANES * (h // 2):LANES * (h // 2 + 1)].astype(F32) * HEAD_DIM ** -0.5
        own = (lane < HEAD_DIM) if h % 2 == 0 else (lane >= HEAD_DIM)
        out.append(jnp.where(own, pair, 0.0).astype(BF16))
    return jnp.stack(out)


def _pair_keys(x):
    x = x.astype(BF16)
    return jnp.stack([x[:, LANES * (h // 2):LANES * (h // 2 + 1)] for h in range(N_HEADS)])


def _pair_values(x):
    x = x.astype(BF16)
    lane = lax.broadcasted_iota(jnp.int32, (x.shape[0], LANES), 1)
    out = []
    for h in range(N_HEADS):
        pair = x[:, LANES * (h // 2):LANES * (h // 2 + 1)]
        own = (lane < HEAD_DIM) if h % 2 == 0 else (lane >= HEAD_DIM)
        out.append(jnp.where(own, pair, jnp.ones_like(pair)))
    return jnp.stack(out)


def _unpair(o):
    lane = lax.broadcasted_iota(jnp.int32, o.shape[1:], 1)
    return jnp.concatenate([jnp.where(lane < HEAD_DIM, o[2 * p], o[2 * p + 1]) for p in range(N_HEADS // 2)], axis=-1)


def _softmax_init(m_scr, acc_scr):
    m_scr[...] = jnp.full(m_scr.shape, NEG, F32)
    acc_scr[...] = jnp.zeros(acc_scr.shape, F32)


def _softmax_step(tiles, m_scr, acc_scr):
    m_prev = m_scr[...]
    top = tiles[0][0]
    for s, _ in tiles[1:]:
        top = jnp.maximum(top, s)
    m_new = jnp.maximum(m_prev, jnp.max(top, axis=-1, keepdims=True))
    acc = jnp.exp(m_prev - m_new) * acc_scr[...]
    for s, v in tiles:
        acc = acc + _bmm(jnp.exp(s - m_new).astype(BF16), v)
    acc_scr[...] = acc
    m_scr[...] = m_new


def _softmax_out(acc_scr):
    acc = acc_scr[...]
    den = jnp.stack([pltpu.roll(acc[h], HEAD_DIM, 1) for h in range(N_HEADS)])
    return _unpair(acc / den)


def _softmax_scratch(tq):
    return [pltpu.VMEM((N_HEADS, tq, LANES), F32), pltpu.VMEM((N_HEADS, tq, LANES), F32)]


def _dsa_kernel(*refs, tq, n_cache, past, t_new, top_k, dyn_new):
    if n_cache:
        (q_ref, qi_ref, mq_ref, cki_ref, cka_ref, cva_ref, nki_ref, nka_ref, nva_ref, bias_ref,
         o_ref, keys_ref, madd_ref, qcat_scr, wi_scr, cand_scr, m_scr, acc_scr) = refs
    else:
        (q_ref, qi_ref, mq_ref, nki_ref, nka_ref, nva_ref, bias_ref,
         o_ref, keys_ref, madd_ref, qcat_scr, wi_scr, cand_scr, m_scr, acc_scr) = refs
        cki_ref = cka_ref = cva_ref = None
    i = pl.program_id(1)
    n_new = (i + 1) * (tq // LANES) if dyn_new else 1
    n_blk = n_cache + n_new
    qpos0 = past + i * tq
    qpos = qpos0 + lax.broadcasted_iota(jnp.int32, (tq, 1), 0)
    col = lax.broadcasted_iota(jnp.int32, (tq, LANES), 1)
    n_keys = past + t_new

    qi = qi_ref[...]
    for h in range(N_HEADS):
        x = qi[:, h * IDX_DIM:(h + 1) * IDX_DIM]
        hi = x.astype(BF16).astype(F32)
        qcat_scr[h * tq:(h + 1) * tq, :] = jnp.concatenate([hi, hi, x - hi, jnp.zeros_like(x)], axis=-1).astype(BF16)
        wi_scr[h] = jnp.broadcast_to(mq_ref[:, IDX_DIM + h:IDX_DIM + h + 1], (tq, LANES))
    score_scale = IDX_DIM ** -0.5 * N_HEADS ** -0.5

    def score_blocks(kcat, kpos0, slot0):
        s = _dot_nt(qcat_scr[...], kcat)
        for d in range(kcat.shape[0] // LANES):
            sd = s[:, d * LANES:(d + 1) * LANES].reshape(N_HEADS, tq, LANES)
            acc = jnp.sum(jnp.maximum(sd, 0.0) * wi_scr[...], axis=0)
            kpos = kpos0 + d * LANES + col
            adm = jnp.logical_and((kpos // CHUNK) <= (qpos // CHUNK), kpos < n_keys)
            keys_ref[slot0 + d] = jnp.where(adm, _sortable(acc * score_scale), INT_MIN)

    if n_cache:
        def cache_scores(j, _):
            score_blocks(_split_keys(cki_ref[_block_rows(j, 2 * LANES), :]), 2 * j * LANES, 2 * j)
            return 0
        lax.fori_loop(0, n_cache // 2, cache_scores, 0)

    if dyn_new:
        def new_scores(j, _):
            score_blocks(nki_ref[_block_rows(j, 2 * LANES), :], past + 2 * j * LANES, n_cache + 2 * j)
            return 0
        lax.fori_loop(0, n_new // 2, new_scores, 0)

        @pl.when(n_new % 2 == 1)
        def _():
            score_blocks(nki_ref[_block_rows(n_new - 1, LANES), :], past + (n_new - 1) * LANES, n_blk - 1)
    else:
        score_blocks(nki_ref[_block_rows(0, LANES), :], past, n_cache)

    def count(cand, strict=False):
        cand_scr[...] = jnp.broadcast_to(cand, (tq, LANES))

        strips = [slice(r, min(r + COUNT_STRIP, tq)) for r in range(0, tq, COUNT_STRIP)]

        def one(j, cs):
            out = []
            for c, rows in zip(cs, strips):
                k = keys_ref[j, rows, :]
                hit = (k > cand_scr[rows, :]) if strict else (k >= cand_scr[rows, :])
                out.append(c + jnp.where(hit, 1.0, 0.0))
            return tuple(out)

        cs = tuple(jnp.zeros((r.stop - r.start, LANES), F32) for r in strips)
        cs = lax.fori_loop(0, n_blk // 2, lambda j, c: one(2 * j + 1, one(2 * j, c)), cs)
        if isinstance(n_blk, int):
            cs = one(n_blk - 1, cs) if n_blk % 2 else cs
        elif (tq // LANES) % 2:
            cs = lax.cond(n_blk % 2 == 1, lambda c: one(n_blk - 1, c), lambda c: c, cs)
        return jnp.sum(jnp.concatenate(cs, axis=0), axis=-1, keepdims=True)

    def search(b, ans):
        cand = ans ^ lax.shift_left(jnp.int32(1), jnp.int32(31) - jnp.asarray(b, jnp.int32))
        return jnp.where(count(cand) >= top_k, cand, ans)

    kth = lax.fori_loop(0, 32, search, jnp.full((tq, 1), INT_MIN, jnp.int32))
    kth = jnp.maximum(kth, INT_MIN + 1)
    kth_b = jnp.broadcast_to(kth, (tq, LANES))
    n_ge = count(kth)
    ties_overflow = jnp.max(n_ge) > top_k

    @pl.when(jnp.logical_not(ties_overflow))
    def _():
        def plain_mask(j, _):
            madd_ref[j] = jnp.where(keys_ref[j] >= kth_b, 0.0, NEG)
            return 0
        lax.fori_loop(0, n_blk, plain_mask, 0)

    @pl.when(ties_overflow)
    def _():
        n_gt = count(kth, strict=True)
        quota = top_k - n_gt
        tri = (lax.broadcasted_iota(jnp.int32, (LANES, LANES), 0)
               <= lax.broadcasted_iota(jnp.int32, (LANES, LANES), 1)).astype(BF16)

        def tie_mask(j, seen):
            k = keys_ref[j]
            eq = jnp.where(k == kth_b, 1.0, 0.0)
            rank = _dot(eq.astype(BF16), tri) + seen
            sel = jnp.where(k > kth_b, 1.0, jnp.where(rank <= quota, eq, 0.0))
            madd_ref[j] = (1.0 - sel) * NEG
            return seen + jnp.sum(eq, axis=-1, keepdims=True)
        lax.fori_loop(0, n_blk, tie_mask, jnp.zeros((tq, 1), F32))

    qm = _pair_queries(q_ref[...])
    _softmax_init(m_scr, acc_scr)

    def step(k_ref, v_ref, blocks, slot0, kblk0):
        tiles = []
        for blk in blocks:
            rows = _block_rows(blk, LANES)
            tile = jnp.clip(kblk0 + blk - qpos0 // LANES + BIAS_FAR, 0, bias_ref.shape[0] - 1)
            s = _bmm_nt(qm, _pair_keys(k_ref[rows, :])) + bias_ref[tile] + madd_ref[slot0 + blk][None]
            tiles.append((s, _pair_values(v_ref[rows, :])))
        _softmax_step(tiles, m_scr, acc_scr)

    if n_cache:
        def cache_step(j, _):
            step(cka_ref, cva_ref, (2 * j, 2 * j + 1), 0, 0)
            return 0
        lax.fori_loop(0, n_cache // 2, cache_step, 0)

    if dyn_new:
        @pl.when(n_new % 2 == 1)
        def _():
            madd_ref[n_blk] = jnp.full((tq, LANES), NEG, F32)

        def new_step(j, _):
            step(nka_ref, nva_ref, (2 * j, 2 * j + 1), n_cache, past // LANES)
            return 0
        lax.fori_loop(0, (n_new + 1) // 2, new_step, 0)
    else:
        step(nka_ref, nva_ref, (0,), n_cache, past // LANES)
    o_ref[...] = _softmax_out(acc_scr).astype(BF16)


def _dsa(q, qi, misc, kcat, ka16, va16, bias, cache, *, batch, t_q, tq):
    t_kv = ka16.shape[0] // batch
    nq = t_q // tq
    past = 0 if cache is None else cache[0].shape[1]
    n_cache = past // LANES
    n_keys = past + t_q
    top_k = min(IDX_TOPK_MAX, n_keys // 4)
    assert tq % LANES == 0 or nq == 1
    assert n_cache % 2 == 0 and (cache is not None or (t_kv // LANES) % 2 == 0)
    qrow = lambda w: pl.BlockSpec((tq, w), lambda b, i: (b * (t_kv // tq) + i, 0))
    krow = lambda w: pl.BlockSpec((t_kv, w), lambda b, i: (b, 0))
    crow = lambda w: pl.BlockSpec((None, past, w), lambda b, i: (b, 0, 0))
    ins = [q, qi, misc]
    in_specs = [qrow(D_HEADS), qrow(D_HEADS), qrow(LANES)]
    if cache is not None:
        ins += list(cache)
        in_specs += [crow(IDX_DIM), crow(D_HEADS), crow(D_HEADS)]
    ins += [kcat, ka16, va16, bias]
    in_specs += [krow(4 * IDX_DIM), krow(D_HEADS), krow(D_HEADS),
                 pl.BlockSpec(bias.shape, lambda b, i: (0, 0, 0, 0))]
    n_blk_max = n_cache + t_kv // LANES
    return pl.pallas_call(
        functools.partial(_dsa_kernel, tq=tq, n_cache=n_cache, past=past, t_new=t_q, top_k=top_k,
                          dyn_new=cache is None),
        out_shape=jax.ShapeDtypeStruct((batch * t_q, D_HEADS), BF16),
        grid=(batch, nq),
        in_specs=in_specs,
        out_specs=pl.BlockSpec((tq, D_HEADS), lambda b, i: (b * nq + i, 0)),
        scratch_shapes=[pltpu.VMEM((n_blk_max, tq, LANES), jnp.int32),
                        pltpu.VMEM((n_blk_max, tq, LANES), F32),
                        pltpu.VMEM((N_HEADS * tq, 4 * IDX_DIM), BF16),
                        pltpu.VMEM((N_HEADS, tq, LANES), F32),
                        pltpu.VMEM((tq, LANES), jnp.int32)] + _softmax_scratch(tq),
        compiler_params=_params("arbitrary", "arbitrary"),
        name="dsa_attention",
    )(*ins)


FOX_GROUP = 4
DSA_TQ = 2 * LANES
COUNT_STRIP = 64
FOX_TQ = 4 * LANES


def _fox_kernel(*refs, tq, tk, tkc, n_cache, past, dyn_new):
    if n_cache:
        q_ref, fq_ref, ck_ref, cv_ref, cf_ref, nk_ref, nv_ref, nf_ref, o_ref, m_scr, acc_scr = refs
    else:
        q_ref, fq_ref, nk_ref, nv_ref, nf_ref, o_ref, m_scr, acc_scr = refs
        ck_ref = cv_ref = cf_ref = None
    i = pl.program_id(1)
    group = tq // tk if dyn_new else 1
    qpos = past + i * tq + lax.broadcasted_iota(jnp.int32, (1, tq, 1), 1)
    col = lax.broadcasted_iota(jnp.int32, (1, 1, LANES), 2)

    qm = _pair_queries(q_ref[...])
    fq = jnp.stack([fq_ref[:, h:h + 1] for h in range(N_HEADS)])
    _softmax_init(m_scr, acc_scr)

    def step(k_ref, v_ref, f_ref, blocks, size, kpos_base=None):
        tiles = []
        for blk in blocks:
            rows = _block_rows(blk, size)
            s = _bmm_nt(qm, _pair_keys(k_ref[rows, :])) + fq - f_ref[blk]
            if kpos_base is not None:
                s = jnp.where(kpos_base + blk * size + col <= qpos, s, NEG)
            tiles.append((s, _pair_values(v_ref[rows, :])))
        _softmax_step(tiles, m_scr, acc_scr)

    if n_cache:
        def cache_step(j, _):
            step(ck_ref, cv_ref, cf_ref, [FOX_GROUP * j + d for d in range(FOX_GROUP)], tkc)
            return 0
        lax.fori_loop(0, n_cache // FOX_GROUP, cache_step, 0)

    if dyn_new:
        def new_step(j, _):
            step(nk_ref, nv_ref, nf_ref, [group * j + d for d in range(group)], tk)
            return 0
        lax.fori_loop(0, i, new_step, 0)
        step(nk_ref, nv_ref, nf_ref, [group * i + d for d in range(group)], tk, past)
    else:
        step(nk_ref, nv_ref, nf_ref, [0], tk, past)
    o_ref[...] = _softmax_out(acc_scr).astype(BF16)


def _key_rows(f, tk):
    b, t, h = f.shape
    return f.reshape(b, t // tk, tk, h).transpose(0, 1, 3, 2)[:, :, :, None, :]


def _fox(q, fq, kb16, vb16, f_new, cache, *, batch, t_q, tq, tk, tkc):
    t_kv = kb16.shape[0] // batch
    nq = t_q // tq
    past = 0 if cache is None else cache[0].shape[1]
    n_cache = past // tkc
    assert tk == LANES and tkc == LANES and n_cache % FOX_GROUP == 0
    qrow = lambda w: pl.BlockSpec((tq, w), lambda b, i: (b * (t_kv // tq) + i, 0))
    krow = lambda w: pl.BlockSpec((t_kv, w), lambda b, i: (b, 0))
    crow = lambda w: pl.BlockSpec((None, past, w), lambda b, i: (b, 0, 0))
    fblk = lambda nb, w: pl.BlockSpec((None, nb, N_HEADS, 1, w), lambda b, i: (b, 0, 0, 0, 0))
    ins = [q, fq]
    in_specs = [qrow(D_HEADS), qrow(N_HEADS)]
    if cache is not None:
        ins += [cache[0], cache[1], _key_rows(cache[2], tkc)]
        in_specs += [crow(D_HEADS), crow(D_HEADS), fblk(n_cache, tkc)]
    ins += [kb16, vb16, _key_rows(f_new, tk)]
    in_specs += [krow(D_HEADS), krow(D_HEADS), fblk(t_kv // tk, tk)]
    return pl.pallas_call(
        functools.partial(_fox_kernel, tq=tq, tk=tk, tkc=tkc, n_cache=n_cache, past=past, dyn_new=cache is None),
        out_shape=jax.ShapeDtypeStruct((batch * t_q, D_HEADS), BF16),
        grid=(batch, nq),
        in_specs=in_specs,
        out_specs=pl.BlockSpec((tq, D_HEADS), lambda b, i: (b * nq + i, 0)),
        scratch_shapes=_softmax_scratch(tq),
        compiler_params=_params("arbitrary", "arbitrary"),
        name="fox_attention",
    )(*ins)


def _out_ab_kernel(oa_ref, ob_ref, w_ref, x_ref, gt_ref, gpost_ref, o_ref):
    y = _dot(oa_ref[...], w_ref[:D_HEADS, :]) + _dot(ob_ref[...], w_ref[D_HEADS:, :])
    o_ref[...] = _residual(x_ref[...], y, gpost_ref[...], gt_ref[...], 1.0)


def _out_ab(oa, ob, w, x, gt, gpost, *, batch, t_q, tm):
    t_kv = oa.shape[0] // batch
    nt = t_q // tm
    arow = pl.BlockSpec((tm, D_HEADS), lambda b, i: (b * (t_kv // tm) + i, 0))
    xrow = pl.BlockSpec((tm, D_MODEL), lambda b, i: (b * nt + i, 0))
    return pl.pallas_call(
        _out_ab_kernel,
        out_shape=jax.ShapeDtypeStruct(x.shape, F32),
        grid=(batch, nt),
        in_specs=[arow, arow, pl.BlockSpec(w.shape, lambda b, i: (0, 0)), xrow,
                  pl.BlockSpec((tm // GROUP, 1, D_MODEL), lambda b, i: (b * nt + i, 0, 0)),
                  pl.BlockSpec((1, D_MODEL), lambda b, i: (0, 0))],
        out_specs=xrow,
        compiler_params=_params("arbitrary", "arbitrary"),
        name="out_ab",
    )(oa, ob, w, x, gt, gpost)


GDN_MISC0 = C_QKV + N_HEADS * DK_C


def _proj_gdn_kernel(x_ref, sh_ref, sc_ref, gpre_ref, w_ref, cw_ref, prev_ref, alog_ref, dtb_ref,
                     q_ref, k_ref, v_ref, gz_ref, misc_ref, tail_ref, zbuf):
    tm = x_ref.shape[0]
    halo = CONV_W - 1
    base = 8

    @pl.when(pl.program_id(1) == 0)
    def _():
        zbuf[base - halo:base, :] = prev_ref[...]

    h = _modulated(x_ref[...], gpre_ref[...], sh_ref[...], sc_ref[...]).astype(BF16)
    zbuf[base:base + tm, :] = _dot(h, w_ref[:, :C_QKV])
    gz_ref[...] = _dot(h, w_ref[:, C_QKV:GDN_MISC0])
    zm = _dot(h, w_ref[:, GDN_MISC0:])
    lane = lax.broadcasted_iota(jnp.int32, zm.shape, 1)
    beta = jax.nn.sigmoid(zm)
    g = -jnp.exp(alog_ref[...]) * jax.nn.softplus(zm + dtb_ref[...])
    misc_ref[...] = jnp.where(lane < N_HEADS, beta, g)

    for part, ref in enumerate((q_ref, k_ref, v_ref)):
        for hd in range(N_HEADS):
            c0 = part * N_HEADS * DK_C + hd * DK_C
            cols = slice(c0, c0 + DK_C)
            conv = zbuf[base:base + tm, cols] * cw_ref[halo:halo + 1, cols]
            for j in range(halo):
                conv = conv + zbuf[base - halo + j:base - halo + j + tm, cols] * cw_ref[j:j + 1, cols]
            a = _silu(conv)
            if part < 2:
                a = a * lax.rsqrt(jnp.sum(a * a, axis=-1, keepdims=True) + EPS)
                if part == 0:
                    a = a * DK_C ** -0.5
            ref[:, hd * DK_C:(hd + 1) * DK_C] = a
    tail = zbuf[base + tm - halo:base + tm, :]
    tail_ref[...] = tail
    zbuf[base - halo:base, :] = tail


def _proj_gdn(x, sh, sc, gpre, w, conv_w, conv_prev, alog_row, dtb_row, *, batch, tm):
    n = x.shape[0]
    t = n // batch
    nt = t // tm
    row = lambda wd: pl.BlockSpec((tm, wd), lambda b, i: (b * nt + i, 0))
    full = lambda a: pl.BlockSpec(a.shape, lambda b, i: (0,) * a.ndim)
    sds = lambda wd: jax.ShapeDtypeStruct((n, wd), F32)
    hd = N_HEADS * DK_C
    tail = pl.BlockSpec((None, CONV_W - 1, C_QKV), lambda b, i: (b, 0, 0))
    return pl.pallas_call(
        _proj_gdn_kernel,
        out_shape=[sds(hd), sds(hd), sds(hd), sds(hd), sds(LANES),
                   jax.ShapeDtypeStruct((batch, CONV_W - 1, C_QKV), F32)],
        grid=(batch, nt),
        in_specs=[row(D_MODEL),
                  pl.BlockSpec((tm // GROUP, 1, D_MODEL), lambda b, i: (b * nt + i, 0, 0)),
                  pl.BlockSpec((tm // GROUP, 1, D_MODEL), lambda b, i: (b * nt + i, 0, 0)),
                  pl.BlockSpec((1, D_MODEL), lambda b, i: (0, 0)),
                  full(w), full(conv_w), tail, full(alog_row), full(dtb_row)],
        out_specs=[row(hd), row(hd), row(hd), row(hd), row(LANES), tail],
        scratch_shapes=[pltpu.VMEM((8 + tm, C_QKV), F32)],
        compiler_params=_params("arbitrary", "arbitrary"),
        name="proj_gdn",
    )(x, sh, sc, gpre, w, conv_w, conv_prev, alog_row, dtb_row)


def _chunk_cumsum_kernel(g_ref, o_ref, *, chunk):
    x = g_ref[...]
    lane = lax.broadcasted_iota(jnp.int32, x.shape, 1) % chunk
    s = 1
    while s < chunk:
        x = x + jnp.where(lane >= s, pltpu.roll(x, s, 1), 0.0)
        s *= 2
    o_ref[...] = x


def _chunk_cumsum(g_rows, chunk):
    assert LANES % chunk == 0
    tiles = g_rows.reshape(-1, LANES)
    return pl.pallas_call(
        functools.partial(_chunk_cumsum_kernel, chunk=chunk),
        out_shape=jax.ShapeDtypeStruct(tiles.shape, F32),
        name="gdn_chunk_cumsum",
    )(tiles).reshape(g_rows.shape)


def _gdn_kernel(q_ref, k_ref, v_ref, col_ref, row_ref, s0_ref, o_ref, s_ref):
    c = q_ref.shape[0]

    @pl.when(pl.program_id(1) == 0)
    def _():
        s_ref[...] = s0_ref[...]

    heads = lambda ref: jnp.stack([ref[:, h * DK_C:(h + 1) * DK_C] for h in range(N_HEADS)])
    q, k, v = heads(q_ref), heads(k_ref), heads(v_ref)
    beta = jnp.stack([col_ref[:, h:h + 1] for h in range(N_HEADS)])
    gc = jnp.stack([col_ref[:, N_HEADS + h:N_HEADS + h + 1] for h in range(N_HEADS)])
    gc_row = row_ref[...]
    gc_last = gc[:, c - 1:c, :]
    ri = lax.broadcasted_iota(jnp.int32, (1, c, c), 1)
    ci = lax.broadcasted_iota(jnp.int32, (1, c, c), 2)
    causal = ri >= ci
    strict = ri > ci
    eye = jnp.where(ri == ci, 1.0, 0.0)
    decay = jnp.exp(jnp.where(causal, gc - gc_row, NEG))
    k16 = k.astype(BF16)
    a = jnp.where(strict, beta * _bmm_nt(k16, k16) * decay, 0.0)
    inv = eye - a
    p = a
    n = 2
    while n < c:
        p = _dot3(p, p, _bmm)
        inv = inv + _dot3(inv, p, _bmm)
        n *= 2
    sol = _dot3(inv, jnp.concatenate([v * beta, k * (beta * jnp.exp(gc))], axis=-1), _bmm)
    u, w = sol[:, :, :DK_C], sol[:, :, DK_C:]
    qk = jnp.where(causal, _bmm_nt(q.astype(BF16), k16) * decay, 0.0)
    q_dec = (q * jnp.exp(gc)).astype(BF16)
    k_dec = k * jnp.exp(gc_last - gc)
    k_dec_t = jnp.stack([k_dec[h].T for h in range(N_HEADS)]).astype(BF16)
    s = s_ref[...]
    s16 = s.astype(BF16)
    v16 = (u - _bmm(w.astype(BF16), s16)).astype(BF16)
    o = _bmm(q_dec, s16) + _bmm(qk.astype(BF16), v16)
    for h in range(N_HEADS):
        o_ref[:, h * DK_C:(h + 1) * DK_C] = o[h]
    s_ref[...] = s * jnp.exp(gc_last) + _bmm(k_dec_t, v16)


def _gdn(q, k, v, col, row, s0, *, batch, chunk):
    n = q.shape[0]
    nc = n // batch // chunk
    hd = N_HEADS * DK_C
    blk = lambda w: pl.BlockSpec((chunk, w), lambda b, i: (b * nc + i, 0))
    state = pl.BlockSpec((None, N_HEADS, DK_C, DK_C), lambda b, i: (b, 0, 0, 0))
    return pl.pallas_call(
        _gdn_kernel,
        out_shape=[jax.ShapeDtypeStruct((n, hd), F32), jax.ShapeDtypeStruct(s0.shape, F32)],
        grid=(batch, nc),
        in_specs=[blk(hd), blk(hd), blk(hd), blk(2 * N_HEADS),
                  pl.BlockSpec((None, N_HEADS, 1, chunk), lambda b, i: (b * nc + i, 0, 0, 0)), state],
        out_specs=[blk(hd), state],
        compiler_params=_params("arbitrary", "arbitrary"),
        name="gated_delta_rule",
    )(q, k, v, col, row, s0)


def _out_gdn_kernel(o_ref, gz_ref, nw_ref, w_ref, x_ref, gt_ref, gpost_ref, out_ref):
    parts = []
    for h in range(N_HEADS):
        cols = slice(h * DK_C, (h + 1) * DK_C)
        o = o_ref[:, cols]
        o = o * lax.rsqrt(jnp.mean(o * o, axis=-1, keepdims=True) + EPS) * nw_ref[...]
        parts.append((o * _silu(gz_ref[:, cols])).astype(BF16))
    y = _dot(jnp.concatenate(parts, axis=-1), w_ref[...])
    out_ref[...] = _residual(x_ref[...], y, gpost_ref[...], gt_ref[...], 1.0)


def _out_gdn(o, gz, norm_w, w, x, gt, gpost, tm=256):
    n = x.shape[0]
    tm = min(tm, n)
    row = pl.BlockSpec((tm, D_MODEL), lambda i: (i, 0))
    return pl.pallas_call(
        _out_gdn_kernel,
        out_shape=jax.ShapeDtypeStruct(x.shape, F32),
        grid=(n // tm,),
        in_specs=[row, row, pl.BlockSpec((1, DK_C), lambda i: (0, 0)), pl.BlockSpec(w.shape, lambda i: (0, 0)),
                  row, pl.BlockSpec((tm // GROUP, 1, D_MODEL), lambda i: (i, 0, 0)),
                  pl.BlockSpec((1, D_MODEL), lambda i: (0, 0))],
        out_specs=row,
        compiler_params=_params("arbitrary"),
        name="out_gdn",
    )(o, gz, norm_w, w, x, gt, gpost)


def _mixer_ab(x, mod, w, past, *, batch, t, bias):
    sh, sc, gt = mod
    (qa, qb, ka, va, kb, vb, ka16, va16, kb16, vb16, qi, misc, kcat) = _proj_ab(
        x, sh, sc, w["pre"], w["ab_main"], w["ab_idx_hi"], w["ab_idx_lo"])
    ki = misc[:, :IDX_DIM]
    fl = misc[:, IDX_DIM + N_HEADS:IDX_DIM + 2 * N_HEADS].reshape(batch, t, N_HEADS)
    t_kv = -(-t // LANES) * LANES
    pad_rows = lambda a: a if t_kv == t else jnp.pad(
        a.reshape(batch, t, -1), ((0, 0), (0, t_kv - t), (0, 0))).reshape(batch * t_kv, -1)
    fl_p = fl if t_kv == t else jnp.pad(fl, ((0, 0), (0, t_kv - t), (0, 0)))
    if past is None:
        logf_b, f_new = _logf(fl_p, w["ab_b_f"], None)
        cache_a = cache_b = None
    else:
        cak, cav, caik, cbk, cbv, cblf = past
        p = cak.shape[1]
        logf_b, f_new, f_cache = _logf(fl_p, w["ab_b_f"], cblf)
        cache_a = (caik, cak.reshape(batch, p, D_HEADS), cav.reshape(batch, p, D_HEADS))
        cache_b = (cbk.reshape(batch, p, D_HEADS), cbv.reshape(batch, p, D_HEADS), _unblocked_rows(f_cache))
    logf = _unblocked_rows(logf_b)[:, :t]
    f_new = _unblocked_rows(f_new)
    o_a = _dsa(pad_rows(qa), pad_rows(qi), pad_rows(misc), pad_rows(kcat), pad_rows(ka16), pad_rows(va16), bias, cache_a,
               batch=batch, t_q=t, tq=bias.shape[2])
    o_b = _fox(pad_rows(qb), f_new.reshape(batch * t_kv, N_HEADS), pad_rows(kb16), pad_rows(vb16), f_new, cache_b,
               batch=batch, t_q=t, tq=min(FOX_TQ, t), tk=LANES, tkc=LANES)
    x = _out_ab(o_a, o_b, w["ab_w_out"], x, gt, w["post"], batch=batch, t_q=t, tm=min(256, t))
    hd = (batch, t, N_HEADS, HEAD_DIM)
    state = (ka.reshape(hd), va.reshape(hd), ki.reshape(batch, t, IDX_DIM), kb.reshape(hd), vb.reshape(hd), logf)
    return x, state


def _mixer_gdn(x, mod, w, past, *, batch, t):
    sh, sc, gt = mod
    if past is None:
        conv_prev = jnp.zeros((batch, CONV_W - 1, C_QKV), F32)
        s0 = jnp.zeros((batch, N_HEADS, DK_C, DK_C), F32)
    else:
        conv_prev, s0 = past
    q, k, v, gz, misc, new_conv = _proj_gdn(x, sh, sc, w["pre"], w["gdn_w_in"], w["gdn_conv_w"], conv_prev,
                                            w["gdn_alog_row"], w["gdn_dtb_row"], batch=batch, tm=min(256, t))
    chunk = min(t, CHUNK)
    n = batch * t
    gc_rows = _chunk_cumsum(misc[:, N_HEADS:2 * N_HEADS].T, chunk)
    col = jnp.concatenate([misc[:, :N_HEADS], gc_rows.T], axis=-1)
    row = gc_rows.reshape(N_HEADS, n // chunk, chunk).transpose(1, 0, 2)[:, :, None, :]
    o, s_new = _gdn(q, k, v, col, row, s0, batch=batch, chunk=chunk)
    x = _out_gdn(o, gz, w["gdn_norm_w"], w["gdn_w_out"], x, gt, w["post"])
    return x, (new_conv, s_new)


def _forward(x, mod_all, w, past, *, batch, t):
    new_state = []
    groups = lambda v: jnp.repeat(v, t // GROUP, axis=0)[:, None, :]
    for l in range(mod_all.shape[0]):
        m = mod_all[l].reshape(batch, 3, 3, D_MODEL)
        mod = [[groups(m[:, s, j]) for j in range(3)] for s in range(3)]
        pre = lambda s: w["norm_pre"][l, s][None, :]
        post = lambda s: w["norm_post"][l, s][None, :]
        x = _ffn(x, mod[0], pre(0), post(0), w["ffn_w_gate"][l, 0], w["ffn_w_up"][l, 0], w["ffn_w_down"][l, 0])
        wl = dict(w, pre=pre(1), post=post(1))
        if l % 2 == 0:
            x, st = _mixer_ab(x, mod[1], wl, None if past is None else past[:6], batch=batch, t=t,
                              bias=w["bias_tiles"][min(DSA_TQ, t)])
        else:
            x, st = _mixer_gdn(x, mod[1], wl, None if past is None else past[6:], batch=batch, t=t)
        new_state.extend(st)
        x = _ffn(x, mod[2], pre(2), post(2), w["ffn_w_gate"][l, 1], w["ffn_w_up"][l, 1], w["ffn_w_down"][l, 1])
    return x, new_state


def _prepare_weights(rel_bias, ab_w_in, ab_b_f, ab_w_out, gdn_w_in, gdn_conv_w, gdn_A_log, gdn_dt_bias,
                     gdn_norm_w, gdn_w_out, norm_pre, norm_post, ffn_w_gate, ffn_w_up, ffn_w_down, tqs):
    o = [0]
    for s in (D_HEADS, D_HEADS, D_HEADS, D_HEADS, IDX_DIM, N_HEADS, D_HEADS, D_HEADS, D_HEADS, N_HEADS):
        o.append(o[-1] + s)
    cols = lambda i: ab_w_in[:, o[i]:o[i + 1]]
    qa, ka, va, qi, ki, wi, qb, kb, vb, fl = (cols(i) for i in range(10))
    idx = jnp.concatenate([qi, ki, wi, fl, jnp.zeros((D_MODEL, IDX_COLS - D_HEADS - IDX_DIM - 2 * N_HEADS), F32)], 1)
    idx_hi = idx.astype(BF16)
    gdn_in = jnp.concatenate([gdn_w_in, jnp.zeros((D_MODEL, LANES - 2 * N_HEADS), F32)], 1)
    lane_row = lambda v: jnp.concatenate([jnp.zeros((N_HEADS,), F32), v.astype(F32),
                                          jnp.zeros((LANES - 2 * N_HEADS,), F32)])[None, :]
    return dict(
        ab_main=jnp.concatenate([qa, qb, ka, va, kb, vb], 1).astype(BF16),
        ab_idx_hi=idx_hi, ab_idx_lo=(idx - idx_hi.astype(F32)).astype(BF16),
        ab_b_f=ab_b_f, ab_w_out=ab_w_out.astype(BF16),
        gdn_w_in=gdn_in.astype(BF16), gdn_conv_w=gdn_conv_w,
        gdn_alog_row=lane_row(gdn_A_log), gdn_dtb_row=lane_row(gdn_dt_bias),
        gdn_norm_w=gdn_norm_w[None, :], gdn_w_out=gdn_w_out.astype(BF16),
        norm_pre=norm_pre, norm_post=norm_post,
        ffn_w_gate=ffn_w_gate.astype(BF16), ffn_w_up=ffn_w_up.astype(BF16), ffn_w_down=ffn_w_down.astype(BF16),
        bias_tiles={tq: _bias_tiles(rel_bias, tq) for tq in tqs},
    )


def kernel(x_prompt, x_sample, cache_a_k, cache_a_v, cache_a_idx_k, cache_b_k, cache_b_v, cache_b_logf,
           state_c_conv, state_c_rec, c_prompt, c_sample, rel_bias, ab_w_in, ab_b_f, ab_w_out,
           gdn_w_in, gdn_conv_w, gdn_A_log, gdn_dt_bias, gdn_norm_w, gdn_w_out, norm_pre, norm_post,
           ada_w, ada_b, ffn_w_gate, ffn_w_up, ffn_w_down):
    bp, tp, _ = x_prompt.shape
    bs, ts, _ = x_sample.shape
    w = _prepare_weights(rel_bias, ab_w_in, ab_b_f, ab_w_out, gdn_w_in, gdn_conv_w, gdn_A_log, gdn_dt_bias,
                         gdn_norm_w, gdn_w_out, norm_pre, norm_post, ffn_w_gate, ffn_w_up, ffn_w_down,
                         tqs={min(DSA_TQ, tp), min(DSA_TQ, ts)})
    mod = _ada(jnp.concatenate([c_prompt, c_sample], 0), ada_w, ada_b)
    y_p, st_p = _forward(x_prompt.reshape(bp * tp, D_MODEL), mod[:, :bp], w, None, batch=bp, t=tp)
    past = (cache_a_k, cache_a_v, cache_a_idx_k, cache_b_k, cache_b_v, cache_b_logf, state_c_conv, state_c_rec)
    y_s, st_s = _forward(x_sample.reshape(bs * ts, D_MODEL), mod[:, bp:], w, past, batch=bs, t=ts)
    return (y_p.reshape(bp, tp, D_MODEL), y_s.reshape(bs, ts, D_MODEL), *st_p, *st_s)
```

```python
import functools
import math

import jax
import jax.numpy as jnp
from jax import lax
from jax.experimental import pallas as pl
from jax.experimental.pallas import tpu as pltpu

D_MODEL = 1024
D_FF = 4 * D_MODEL
CHUNK = 64
HEAD_DIM = 64
N_HEADS = 8
D_HEADS = N_HEADS * HEAD_DIM
IDX_DIM = 64
IDX_TOPK_MAX = 256
DK_C = 128
CONV_W = 4
C_QKV = 3 * N_HEADS * DK_C
REL_BUCKETS = 32
REL_MAX_DIST = 256
FFN_RES = 0.5
EPS = 1e-6

GROUP = 32
LANES = 128
NEG = -1e30
INT_MIN = -(2 ** 31)
VMEM_LIMIT = 56 * 1024 * 1024

F32 = jnp.float32
BF16 = jnp.bfloat16


def _params(*sem):
    return pltpu.CompilerParams(dimension_semantics=sem, vmem_limit_bytes=VMEM_LIMIT)


def _dot(a, b):
    return jnp.dot(a, b, preferred_element_type=F32)


def _dot_nt(a, b):
    return lax.dot_general(a, b, (((1,), (1,)), ((), ())), preferred_element_type=F32)


def _bmm(a, b):
    return lax.dot_general(a, b, (((2,), (1,)), ((0,), (0,))), preferred_element_type=F32)


def _bmm_nt(a, b):
    return lax.dot_general(a, b, (((2,), (2,)), ((0,), (0,))), preferred_element_type=F32)


def _split(x):
    hi = x.astype(BF16)
    lo = (x - hi.astype(F32)).astype(BF16)
    return hi, lo


def _dot3(a, b, mm=_dot):
    ah, al = _split(a)
    bh, bl = _split(b)
    return mm(ah, bh) + mm(al, bh) + mm(ah, bl)


def _silu(x):
    return x * jax.nn.sigmoid(x)


def _modulated(x, gpre, sh, sc):
    r = lax.rsqrt(jnp.mean(x * x, axis=-1, keepdims=True) + EPS)
    y = (x * r * gpre).reshape(sh.shape[0], GROUP, x.shape[-1])
    return (y * (1.0 + sc) + sh).reshape(x.shape)


def _residual(x, y, gpost, gate, weight):
    r = lax.rsqrt(jnp.mean(y * y, axis=-1, keepdims=True) + EPS)
    yn = (y * r * gpost).reshape(gate.shape[0], GROUP, x.shape[-1])
    return x + (weight * gate * yn).reshape(x.shape)


def _ada_kernel(c_ref, w_ref, b_ref, o_ref):
    o_ref[...] = _dot3(_silu(c_ref[...]), w_ref[...]) + b_ref[...]


def _ada(c_all, ada_w, ada_b):
    depth, _, n_out = ada_w.shape
    nb = c_all.shape[0]
    tn = 1024
    return pl.pallas_call(
        _ada_kernel,
        out_shape=jax.ShapeDtypeStruct((depth, nb, n_out), F32),
        grid=(depth, n_out // tn),
        in_specs=[pl.BlockSpec((nb, D_MODEL), lambda l, j: (0, 0)),
                  pl.BlockSpec((None, D_MODEL, tn), lambda l, j: (l, 0, j)),
                  pl.BlockSpec((None, 1, tn), lambda l, j: (l, 0, j))],
        out_specs=pl.BlockSpec((None, nb, tn), lambda l, j: (l, 0, j)),
        compiler_params=_params("arbitrary", "arbitrary"),
        name="ada_mod",
    )(c_all, ada_w, ada_b.reshape(depth, 1, n_out))


def _ffn_kernel(x_ref, sh_ref, sc_ref, gt_ref, gpre_ref, gpost_ref, wg_ref, wu_ref, wd_ref, o_ref,
                h_scr, acc_scr):
    j = pl.program_id(1)

    @pl.when(j == 0)
    def _():
        h_scr[...] = _modulated(x_ref[...], gpre_ref[...], sh_ref[...], sc_ref[...]).astype(BF16)
        acc_scr[...] = jnp.zeros_like(acc_scr)

    h = h_scr[...]
    a = _dot(h, wg_ref[...])
    b = _dot(h, wu_ref[...])
    acc_scr[...] += _dot((_silu(a) * b).astype(BF16), wd_ref[...])

    @pl.when(j == pl.num_programs(1) - 1)
    def _():
        o_ref[...] = _residual(x_ref[...], acc_scr[...], gpost_ref[...], gt_ref[...], FFN_RES)


def _ffn(x, mod, gpre, gpost, wg, wu, wd, tm=512, tf=512):
    n = x.shape[0]
    tm = min(tm, n)
    sh, sc, gt = mod
    row = pl.BlockSpec((tm, D_MODEL), lambda i, j: (i, 0))
    grp = pl.BlockSpec((tm // GROUP, 1, D_MODEL), lambda i, j: (i, 0, 0))
    vec = pl.BlockSpec((1, D_MODEL), lambda i, j: (0, 0))
    return pl.pallas_call(
        _ffn_kernel,
        out_shape=jax.ShapeDtypeStruct((n, D_MODEL), F32),
        grid=(n // tm, D_FF // tf),
        in_specs=[row, grp, grp, grp, vec, vec,
                  pl.BlockSpec((D_MODEL, tf), lambda i, j: (0, j)),
                  pl.BlockSpec((D_MODEL, tf), lambda i, j: (0, j)),
                  pl.BlockSpec((tf, D_MODEL), lambda i, j: (j, 0))],
        out_specs=row,
        scratch_shapes=[pltpu.VMEM((tm, D_MODEL), BF16), pltpu.VMEM((tm, D_MODEL), F32)],
        compiler_params=_params("arbitrary", "arbitrary"),
        name="ffn_sublayer",
    )(x, sh, sc, gt, gpre, gpost, wg, wu, wd)


AB_MAIN = ("qa", "qb", "ka", "va", "kb", "vb")
IDX_COLS = D_HEADS + LANES


def _proj_ab_kernel(x_ref, sh_ref, sc_ref, gpre_ref, wm_ref, wih_ref, wil_ref,
                    qa_ref, qb_ref, ka_ref, va_ref, kb_ref, vb_ref,
                    ka16_ref, va16_ref, kb16_ref, vb16_ref, qi_ref, misc_ref, kcat_ref, *, feature_major):
    h = _modulated(x_ref[...], gpre_ref[...], sh_ref[...], sc_ref[...])
    hi, lo = _split(h)
    for n, ref in enumerate((qa_ref, qb_ref)):
        ref[...] = _dot(hi, wm_ref[:, n * D_HEADS:(n + 1) * D_HEADS]).astype(BF16)
    for n, (ref, ref16) in enumerate(((ka_ref, ka16_ref), (va_ref, va16_ref),
                                      (kb_ref, kb16_ref), (vb_ref, vb16_ref))):
        z = _dot(hi, wm_ref[:, (n + 2) * D_HEADS:(n + 3) * D_HEADS])
        ref[...] = z.T.reshape(ref.shape) if feature_major else z
        ref16[...] = z.astype(BF16)
    z = _dot(hi, wih_ref[...]) + _dot(lo, wih_ref[...]) + _dot(hi, wil_ref[...])
    qi_ref[...] = z[:, :D_HEADS]
    misc_ref[...] = z[:, D_HEADS:]
    kcat_ref[...] = _split_keys(z[:, D_HEADS:D_HEADS + IDX_DIM])


def _proj_ab(x, sh, sc, gpre, wm, wih, wil, *, batch, tm=256):
    n = x.shape[0]
    t = n // batch
    tm = min(tm, n)
    feature_major = t % tm == 0 and tm % LANES == 0
    row = lambda w: pl.BlockSpec((tm, w), lambda i: (i, 0))
    full = lambda a: pl.BlockSpec(a.shape, lambda i: (0, 0))
    grp = pl.BlockSpec((tm // GROUP, 1, D_MODEL), lambda i: (i, 0, 0))
    sds = lambda w, dt: jax.ShapeDtypeStruct((n, w), dt)
    if feature_major:
        nt = t // tm
        kv_shape = jax.ShapeDtypeStruct((batch, N_HEADS, HEAD_DIM, t), F32)
        kv_spec = pl.BlockSpec((None, N_HEADS, HEAD_DIM, tm), lambda i: (i // nt, 0, 0, i % nt))
    else:
        kv_shape, kv_spec = sds(D_HEADS, F32), row(D_HEADS)
    out_shape = ([sds(D_HEADS, BF16)] * 2 + [kv_shape] * 4 + [sds(D_HEADS, BF16)] * 4
                 + [sds(D_HEADS, F32), sds(LANES, F32), sds(4 * IDX_DIM, BF16)])
    out_specs = [row(D_HEADS)] * 2 + [kv_spec] * 4 + [row(D_HEADS)] * 5 + [row(LANES), row(4 * IDX_DIM)]
    outs = pl.pallas_call(
        functools.partial(_proj_ab_kernel, feature_major=feature_major),
        out_shape=out_shape,
        grid=(n // tm,),
        in_specs=[row(D_MODEL), grp, grp, pl.BlockSpec((1, D_MODEL), lambda i: (0, 0)),
                  full(wm), full(wih), full(wil)],
        out_specs=out_specs,
        compiler_params=_params("arbitrary"),
        name="proj_ab",
    )(x, sh, sc, gpre, wm, wih, wil)
    to_cache = ((lambda a: jnp.transpose(a, (0, 3, 1, 2))) if feature_major
                else (lambda a: a.reshape(batch, t, N_HEADS, HEAD_DIM)))
    return (*outs[:2], *(to_cache(a) for a in outs[2:6]), *outs[6:])


def _lane_cumsum(x):
    lane = lax.broadcasted_iota(jnp.int32, x.shape, 1)
    s = 1
    while s < LANES:
        x = x + jnp.where(lane >= s, pltpu.roll(x, s, 1), 0.0)
        s *= 2
    return x


def _logf_kernel(*refs, n_cache, n_new):
    if n_cache:
        fl_ref, bf_ref, cache_ref, logf_ref, fnew_ref, fcache_ref = refs
    else:
        fl_ref, bf_ref, logf_ref, fnew_ref = refs
    carry = jnp.zeros((N_HEADS, 1), F32)
    for blk in range(n_cache):
        c = _lane_cumsum(cache_ref[blk]) + carry
        fcache_ref[blk] = c
        carry = c[:, LANES - 1:LANES]
    for blk in range(n_new):
        lf = jax.nn.log_sigmoid(fl_ref[blk] + bf_ref[...])
        logf_ref[blk] = lf
        c = _lane_cumsum(lf) + carry
        fnew_ref[blk] = c
        carry = c[:, LANES - 1:LANES]


def _blocked_rows(a):
    b, t, h = a.shape
    return a.reshape(b, t // LANES, LANES, h).transpose(0, 1, 3, 2)


def _unblocked_rows(a):
    b, nb, h, _ = a.shape
    return a.transpose(0, 1, 3, 2).reshape(b, nb * LANES, h)


def _logf(fl, b_f, cache_logf):
    b, t, _ = fl.shape
    n_new = t // LANES
    blk = lambda nb: pl.BlockSpec((None, nb, N_HEADS, LANES), lambda i: (i, 0, 0, 0))
    sds = lambda nb: jax.ShapeDtypeStruct((b, nb, N_HEADS, LANES), F32)
    ins = [_blocked_rows(fl), b_f.reshape(N_HEADS, 1)]
    in_specs = [blk(n_new), pl.BlockSpec((N_HEADS, 1), lambda i: (0, 0))]
    out_shape = [sds(n_new), sds(n_new)]
    out_specs = [blk(n_new), blk(n_new)]
    n_cache = 0
    if cache_logf is not None:
        n_cache = cache_logf.shape[1] // LANES
        ins.append(_blocked_rows(cache_logf))
        in_specs.append(blk(n_cache))
        out_shape.append(sds(n_cache))
        out_specs.append(blk(n_cache))
    outs = pl.pallas_call(
        functools.partial(_logf_kernel, n_cache=n_cache, n_new=n_new),
        out_shape=out_shape, grid=(b,), in_specs=in_specs, out_specs=out_specs,
        compiler_params=_params("arbitrary"), name="logf_cumsum",
    )(*ins)
    return outs


BIAS_FAR = 3


def _t5_bucket(rel):
    nb = REL_BUCKETS // 2
    max_exact = nb // 2
    base = jnp.where(rel > 0, nb, 0)
    n = jnp.abs(rel)
    nf = jnp.maximum(n, 1).astype(F32)
    large = max_exact + (jnp.log(nf / max_exact) / math.log(REL_MAX_DIST / max_exact)
                         * (nb - max_exact)).astype(jnp.int32)
    large = jnp.minimum(large, nb - 1)
    return base + jnp.where(n < max_exact, n, large)


def _bias_kernel(tab_ref, o_ref):
    tq = o_ref.shape[2]
    row = lax.broadcasted_iota(jnp.int32, (tq, LANES), 0)
    col = lax.broadcasted_iota(jnp.int32, (tq, LANES), 1)
    for t in range(o_ref.shape[0]):
        bucket = _t5_bucket(LANES * (t - BIAS_FAR) + col - row)
        for h in range(N_HEADS):
            acc = jnp.zeros((tq, LANES), F32)
            for b in range(REL_BUCKETS):
                acc = jnp.where(bucket == b, tab_ref[b, h], acc)
            o_ref[t, h] = acc


def _bias_tiles(rel_bias, tq):
    assert LANES * BIAS_FAR - (LANES - 1) >= REL_MAX_DIST
    n_tiles = BIAS_FAR + max(1, tq // LANES)
    return pl.pallas_call(
        _bias_kernel,
        out_shape=jax.ShapeDtypeStruct((n_tiles, N_HEADS, tq, LANES), F32),
        in_specs=[pl.BlockSpec(memory_space=pltpu.SMEM)],
        name="rel_bias_tiles",
    )(rel_bias)


def _sortable(x):
    b = pltpu.bitcast(x, jnp.int32)
    return b ^ ((b >> 31) & 0x7FFFFFFF)


def _split_keys(ki):
    hi = ki.astype(BF16).astype(F32)
    return jnp.concatenate([hi, ki - hi, hi, ki - hi], axis=-1).astype(BF16)


def _block_rows(blk, size):
    start = blk * size
    return pl.ds(start if isinstance(start, int) else pl.multiple_of(start, size), size)


def _pair_queries(q16):
    lane = lax.broadcasted_iota(jnp.int32, (q16.shape[0], LANES), 1)
    out = []
    for h in range(N_HEADS):
        pair = q16[:, LANES * (h // 2):LANES * (h // 2 + 1)].astype(F32) * HEAD_DIM ** -0.5
        own = (lane < HEAD_DIM) if h % 2 == 0 else (lane >= HEAD_DIM)
        out.append(jnp.where(own, pair, 0.0).astype(BF16))
    return jnp.stack(out)


def _pair_keys(x):
    x = x.astype(BF16)
    return jnp.stack([x[:, LANES * (h // 2):LANES * (h // 2 + 1)] for h in range(N_HEADS)])


def _pair_values(x):
    x = x.astype(BF16)
    lane = lax.broadcasted_iota(jnp.int32, (x.shape[0], LANES), 1)
    out = []
    for h in range(N_HEADS):
        pair = x[:, LANES * (h // 2):LANES * (h // 2 + 1)]
        own = (lane < HEAD_DIM) if h % 2 == 0 else (lane >= HEAD_DIM)
        out.append(jnp.where(own, pair, jnp.ones_like(pair)))
    return jnp.stack(out)


def _head_queries(q16):
    return jnp.stack([(q16[:, h * HEAD_DIM:(h + 1) * HEAD_DIM].astype(F32) * HEAD_DIM ** -0.5).astype(BF16)
                      for h in range(N_HEADS)])


def _cache_values(vt):
    vt = vt.astype(BF16)
    ones = jnp.ones_like(vt[0])
    return jnp.stack([jnp.concatenate([vt[h], ones] if h % 2 == 0 else [ones, vt[h]], axis=0) for h in range(N_HEADS)])


def _cache_spec(past):
    return pl.BlockSpec((None, N_HEADS, HEAD_DIM, past), lambda b, i: (b, 0, 0, 0))


def _feature_major(cache):
    return jnp.transpose(cache, (0, 2, 3, 1))


def _cache_group(tq, n_cache):
    return next(g for g in (8, 4, 2, 1) if g <= (8 if tq <= 64 else FOX_GROUP) and n_cache % g == 0)


def _block_cols(blk, size):
    return pl.ds(pl.multiple_of(blk * size, size), size)


def _unpair(o):
    lane = lax.broadcasted_iota(jnp.int32, o.shape[1:], 1)
    return jnp.concatenate([jnp.where(lane < HEAD_DIM, o[2 * p], o[2 * p + 1]) for p in range(N_HEADS // 2)], axis=-1)


def _softmax_init(m_scr, acc_scr):
    m_scr[...] = jnp.full(m_scr.shape, NEG, F32)
    acc_scr[...] = jnp.zeros(acc_scr.shape, F32)


def _softmax_step(tiles, m_scr, acc_scr):
    m_prev = m_scr[...]
    top = tiles[0][0]
    for tile in tiles[1:]:
        top = jnp.maximum(top, tile[0])
    m_new = jnp.maximum(m_prev, jnp.max(top, axis=-1, keepdims=True))
    acc = jnp.exp(m_prev - m_new) * acc_scr[...]
    for s, v, feature_major in tiles:
        acc = acc + (_bmm_nt if feature_major else _bmm)(jnp.exp(s - m_new).astype(BF16), v)
    acc_scr[...] = acc
    m_scr[...] = m_new


def _softmax_out(acc_scr):
    acc = acc_scr[...]
    den = jnp.stack([pltpu.roll(acc[h], HEAD_DIM, 1) for h in range(N_HEADS)])
    return _unpair(acc / den)


def _softmax_scratch(tq):
    return [pltpu.VMEM((N_HEADS, tq, LANES), F32), pltpu.VMEM((N_HEADS, tq, LANES), F32)]


def _dsa_kernel(*refs, tq, n_cache, past, t_new, top_k, dyn_new):
    if n_cache:
        (q_ref, qi_ref, mq_ref, cki_ref, cka_ref, cva_ref, nki_ref, nka_ref, nva_ref, bias_ref,
         o_ref, keys_ref, madd_ref, qcat_scr, wi_scr, cand_scr, m_scr, acc_scr) = refs
    else:
        (q_ref, qi_ref, mq_ref, nki_ref, nka_ref, nva_ref, bias_ref,
         o_ref, keys_ref, madd_ref, qcat_scr, wi_scr, cand_scr, m_scr, acc_scr) = refs
        cki_ref = cka_ref = cva_ref = None
    i = pl.program_id(1)
    n_new = (i + 1) * (tq // LANES) if dyn_new else 1
    n_blk = n_cache + n_new
    qpos0 = past + i * tq
    qpos = qpos0 + lax.broadcasted_iota(jnp.int32, (tq, 1), 0)
    col = lax.broadcasted_iota(jnp.int32, (tq, LANES), 1)
    n_keys = past + t_new

    qi = qi_ref[...]
    for h in range(N_HEADS):
        x = qi[:, h * IDX_DIM:(h + 1) * IDX_DIM]
        hi = x.astype(BF16).astype(F32)
        qcat_scr[h * tq:(h + 1) * tq, :] = jnp.concatenate([hi, hi, x - hi, jnp.zeros_like(x)], axis=-1).astype(BF16)
        wi_scr[h] = jnp.broadcast_to(mq_ref[:, IDX_DIM + h:IDX_DIM + h + 1], (tq, LANES))
    score_scale = IDX_DIM ** -0.5 * N_HEADS ** -0.5

    def score_blocks(kcat, kpos0, slot0):
        s = _dot_nt(qcat_scr[...], kcat)
        for d in range(kcat.shape[0] // LANES):
            sd = s[:, d * LANES:(d + 1) * LANES].reshape(N_HEADS, tq, LANES)
            acc = jnp.sum(jnp.maximum(sd, 0.0) * wi_scr[...], axis=0)
            kpos = kpos0 + d * LANES + col
            adm = jnp.logical_and((kpos // CHUNK) <= (qpos // CHUNK), kpos < n_keys)
            keys_ref[slot0 + d] = jnp.where(adm, _sortable(acc * score_scale), INT_MIN)

    if n_cache:
        def cache_scores(j, _):
            score_blocks(_split_keys(cki_ref[_block_rows(j, 2 * LANES), :]), 2 * j * LANES, 2 * j)
            return 0
        lax.fori_loop(0, n_cache // 2, cache_scores, 0)

    if dyn_new:
        def new_scores(j, _):
            score_blocks(nki_ref[_block_rows(j, 2 * LANES), :], past + 2 * j * LANES, n_cache + 2 * j)
            return 0
        lax.fori_loop(0, n_new // 2, new_scores, 0)

        @pl.when(n_new % 2 == 1)
        def _():
            score_blocks(nki_ref[_block_rows(n_new - 1, LANES), :], past + (n_new - 1) * LANES, n_blk - 1)
    else:
        score_blocks(nki_ref[_block_rows(0, LANES), :], past, n_cache)

    def count(cand, strict=False):
        cand_scr[...] = jnp.broadcast_to(cand, (tq, LANES))

        strips = [slice(r, min(r + COUNT_STRIP, tq)) for r in range(0, tq, COUNT_STRIP)]

        def one(j, cs):
            out = []
            for c, rows in zip(cs, strips):
                k = keys_ref[j, rows, :]
                hit = (k > cand_scr[rows, :]) if strict else (k >= cand_scr[rows, :])
                out.append(c + jnp.where(hit, 1.0, 0.0))
            return tuple(out)

        cs = tuple(jnp.zeros((r.stop - r.start, LANES), F32) for r in strips)
        cs = lax.fori_loop(0, n_blk // 2, lambda j, c: one(2 * j + 1, one(2 * j, c)), cs)
        if isinstance(n_blk, int):
            cs = one(n_blk - 1, cs) if n_blk % 2 else cs
        elif (tq // LANES) % 2:
            cs = lax.cond(n_blk % 2 == 1, lambda c: one(n_blk - 1, c), lambda c: c, cs)
        return jnp.sum(jnp.concatenate(cs, axis=0), axis=-1, keepdims=True)

    def search(b, ans):
        cand = ans ^ lax.shift_left(jnp.int32(1), jnp.int32(31) - jnp.asarray(b, jnp.int32))
        return jnp.where(count(cand) >= top_k, cand, ans)

    kth = lax.fori_loop(0, 32, search, jnp.full((tq, 1), INT_MIN, jnp.int32))
    kth = jnp.maximum(kth, INT_MIN + 1)
    kth_b = jnp.broadcast_to(kth, (tq, LANES))
    n_ge = count(kth)
    ties_overflow = jnp.max(n_ge) > top_k

    @pl.when(jnp.logical_not(ties_overflow))
    def _():
        def plain_mask(j, _):
            madd_ref[j] = jnp.where(keys_ref[j] >= kth_b, 0.0, NEG)
            return 0
        lax.fori_loop(0, n_blk, plain_mask, 0)

    @pl.when(ties_overflow)
    def _():
        n_gt = count(kth, strict=True)
        quota = top_k - n_gt
        tri = (lax.broadcasted_iota(jnp.int32, (LANES, LANES), 0)
               <= lax.broadcasted_iota(jnp.int32, (LANES, LANES), 1)).astype(BF16)

        def tie_mask(j, seen):
            k = keys_ref[j]
            eq = jnp.where(k == kth_b, 1.0, 0.0)
            rank = _dot(eq.astype(BF16), tri) + seen
            sel = jnp.where(k > kth_b, 1.0, jnp.where(rank <= quota, eq, 0.0))
            madd_ref[j] = (1.0 - sel) * NEG
            return seen + jnp.sum(eq, axis=-1, keepdims=True)
        lax.fori_loop(0, n_blk, tie_mask, jnp.zeros((tq, 1), F32))

    qm = _pair_queries(q_ref[...])
    _softmax_init(m_scr, acc_scr)

    def step(k_ref, v_ref, blocks, slot0, kblk0, cached=False):
        tiles = []
        for blk in blocks:
            tile = jnp.clip(kblk0 + blk - qpos0 // LANES + BIAS_FAR, 0, bias_ref.shape[0] - 1)
            if cached:
                cols = _block_cols(blk, LANES)
                qk, v = _bmm(qh, k_ref[:, :, cols].astype(BF16)), _cache_values(v_ref[:, :, cols])
            else:
                rows = _block_rows(blk, LANES)
                qk, v = _bmm_nt(qm, _pair_keys(k_ref[rows, :])), _pair_values(v_ref[rows, :])
            tiles.append((qk + bias_ref[tile] + madd_ref[slot0 + blk][None], v, cached))
        _softmax_step(tiles, m_scr, acc_scr)

    if n_cache:
        qh = _head_queries(q_ref[...])

        group = _cache_group(tq, n_cache)

        def cache_step(j, _):
            step(cka_ref, cva_ref, [group * j + d for d in range(group)], 0, 0, cached=True)
            return 0
        lax.fori_loop(0, n_cache // group, cache_step, 0)

    if dyn_new:
        @pl.when(n_new % 2 == 1)
        def _():
            madd_ref[n_blk] = jnp.full((tq, LANES), NEG, F32)

        def new_step(j, _):
            step(nka_ref, nva_ref, (2 * j, 2 * j + 1), n_cache, past // LANES)
            return 0
        lax.fori_loop(0, (n_new + 1) // 2, new_step, 0)
    else:
        step(nka_ref, nva_ref, (0,), n_cache, past // LANES)
    o_ref[...] = _softmax_out(acc_scr).astype(BF16)


def _dsa(q, qi, misc, kcat, ka16, va16, bias, cache, *, batch, t_q, tq):
    t_kv = ka16.shape[0] // batch
    nq = t_q // tq
    past = 0 if cache is None else cache[0].shape[1]
    n_cache = past // LANES
    n_keys = past + t_q
    top_k = min(IDX_TOPK_MAX, n_keys // 4)
    assert tq % LANES == 0 or nq == 1
    assert n_cache % 2 == 0 and (cache is not None or (t_kv // LANES) % 2 == 0)
    qrow = lambda w: pl.BlockSpec((tq, w), lambda b, i: (b * (t_kv // tq) + i, 0))
    krow = lambda w: pl.BlockSpec((t_kv, w), lambda b, i: (b, 0))
    crow = lambda w: pl.BlockSpec((None, past, w), lambda b, i: (b, 0, 0))
    ins = [q, qi, misc]
    in_specs = [qrow(D_HEADS), qrow(D_HEADS), qrow(LANES)]
    if cache is not None:
        ins += list(cache)
        in_specs += [crow(IDX_DIM), _cache_spec(past), _cache_spec(past)]
    ins += [kcat, ka16, va16, bias]
    in_specs += [krow(4 * IDX_DIM), krow(D_HEADS), krow(D_HEADS),
                 pl.BlockSpec(bias.shape, lambda b, i: (0, 0, 0, 0))]
    n_blk_max = n_cache + t_kv // LANES
    return pl.pallas_call(
        functools.partial(_dsa_kernel, tq=tq, n_cache=n_cache, past=past, t_new=t_q, top_k=top_k,
                          dyn_new=cache is None),
        out_shape=jax.ShapeDtypeStruct((batch * t_q, D_HEADS), BF16),
        grid=(batch, nq),
        in_specs=in_specs,
        out_specs=pl.BlockSpec((tq, D_HEADS), lambda b, i: (b * nq + i, 0)),
        scratch_shapes=[pltpu.VMEM((n_blk_max, tq, LANES), jnp.int32),
                        pltpu.VMEM((n_blk_max, tq, LANES), F32),
                        pltpu.VMEM((N_HEADS * tq, 4 * IDX_DIM), BF16),
                        pltpu.VMEM((N_HEADS, tq, LANES), F32),
                        pltpu.VMEM((tq, LANES), jnp.int32)] + _softmax_scratch(tq),
        compiler_params=_params("arbitrary", "arbitrary"),
        name="dsa_attention",
    )(*ins)


FOX_GROUP = 4
DSA_TQ = 2 * LANES
COUNT_STRIP = 64
FOX_TQ = 4 * LANES


def _fox_kernel(*refs, tq, tk, tkc, n_cache, past, dyn_new):
    if n_cache:
        q_ref, fq_ref, ck_ref, cv_ref, cf_ref, nk_ref, nv_ref, nf_ref, o_ref, m_scr, acc_scr = refs
    else:
        q_ref, fq_ref, nk_ref, nv_ref, nf_ref, o_ref, m_scr, acc_scr = refs
        ck_ref = cv_ref = cf_ref = None
    i = pl.program_id(1)
    group = tq // tk if dyn_new else 1
    qpos = past + i * tq + lax.broadcasted_iota(jnp.int32, (1, tq, 1), 1)
    col = lax.broadcasted_iota(jnp.int32, (1, 1, LANES), 2)

    qm = _pair_queries(q_ref[...])
    fq = jnp.stack([fq_ref[:, h:h + 1] for h in range(N_HEADS)])
    _softmax_init(m_scr, acc_scr)

    def step(k_ref, v_ref, f_ref, blocks, size, kpos_base=None, cached=False):
        tiles = []
        for blk in blocks:
            if cached:
                cols = _block_cols(blk, size)
                qk, v = _bmm(qh, k_ref[:, :, cols].astype(BF16)), _cache_values(v_ref[:, :, cols])
            else:
                rows = _block_rows(blk, size)
                qk, v = _bmm_nt(qm, _pair_keys(k_ref[rows, :])), _pair_values(v_ref[rows, :])
            s = qk + fq - f_ref[blk]
            if kpos_base is not None:
                s = jnp.where(kpos_base + blk * size + col <= qpos, s, NEG)
            tiles.append((s, v, cached))
        _softmax_step(tiles, m_scr, acc_scr)

    if n_cache:
        qh = _head_queries(q_ref[...])

        group_c = _cache_group(tq, n_cache)

        def cache_step(j, _):
            step(ck_ref, cv_ref, cf_ref, [group_c * j + d for d in range(group_c)], tkc, cached=True)
            return 0
        lax.fori_loop(0, n_cache // group_c, cache_step, 0)

    if dyn_new:
        def new_step(j, _):
            step(nk_ref, nv_ref, nf_ref, [group * j + d for d in range(group)], tk)
            return 0
        lax.fori_loop(0, i, new_step, 0)
        step(nk_ref, nv_ref, nf_ref, [group * i + d for d in range(group)], tk, past)
    else:
        step(nk_ref, nv_ref, nf_ref, [0], tk, past)
    o_ref[...] = _softmax_out(acc_scr).astype(BF16)


def _key_rows(f, tk):
    b, t, h = f.shape
    return f.reshape(b, t // tk, tk, h).transpose(0, 1, 3, 2)[:, :, :, None, :]


def _fox(q, fq, kb16, vb16, f_new, cache, *, batch, t_q, tq, tk, tkc):
    t_kv = kb16.shape[0] // batch
    nq = t_q // tq
    past = 0 if cache is None else cache[2].shape[1]
    n_cache = past // tkc
    assert tk == LANES and tkc == LANES
    qrow = lambda w: pl.BlockSpec((tq, w), lambda b, i: (b * (t_kv // tq) + i, 0))
    krow = lambda w: pl.BlockSpec((t_kv, w), lambda b, i: (b, 0))
    crow = lambda w: pl.BlockSpec((None, past, w), lambda b, i: (b, 0, 0))
    fblk = lambda nb, w: pl.BlockSpec((None, nb, N_HEADS, 1, w), lambda b, i: (b, 0, 0, 0, 0))
    ins = [q, fq]
    in_specs = [qrow(D_HEADS), qrow(N_HEADS)]
    if cache is not None:
        ins += [cache[0], cache[1], _key_rows(cache[2], tkc)]
        in_specs += [_cache_spec(past), _cache_spec(past), fblk(n_cache, tkc)]
    ins += [kb16, vb16, _key_rows(f_new, tk)]
    in_specs += [krow(D_HEADS), krow(D_HEADS), fblk(t_kv // tk, tk)]
    return pl.pallas_call(
        functools.partial(_fox_kernel, tq=tq, tk=tk, tkc=tkc, n_cache=n_cache, past=past, dyn_new=cache is None),
        out_shape=jax.ShapeDtypeStruct((batch * t_q, D_HEADS), BF16),
        grid=(batch, nq),
        in_specs=in_specs,
        out_specs=pl.BlockSpec((tq, D_HEADS), lambda b, i: (b * nq + i, 0)),
        scratch_shapes=_softmax_scratch(tq),
        compiler_params=_params("arbitrary", "arbitrary"),
        name="fox_attention",
    )(*ins)


def _out_ab_kernel(oa_ref, ob_ref, w_ref, x_ref, gt_ref, gpost_ref, o_ref):
    y = _dot(oa_ref[...], w_ref[:D_HEADS, :]) + _dot(ob_ref[...], w_ref[D_HEADS:, :])
    o_ref[...] = _residual(x_ref[...], y, gpost_ref[...], gt_ref[...], 1.0)


def _out_ab(oa, ob, w, x, gt, gpost, *, batch, t_q, tm):
    t_kv = oa.shape[0] // batch
    nt = t_q // tm
    arow = pl.BlockSpec((tm, D_HEADS), lambda b, i: (b * (t_kv // tm) + i, 0))
    xrow = pl.BlockSpec((tm, D_MODEL), lambda b, i: (b * nt + i, 0))
    return pl.pallas_call(
        _out_ab_kernel,
        out_shape=jax.ShapeDtypeStruct(x.shape, F32),
        grid=(batch, nt),
        in_specs=[arow, arow, pl.BlockSpec(w.shape, lambda b, i: (0, 0)), xrow,
                  pl.BlockSpec((tm // GROUP, 1, D_MODEL), lambda b, i: (b * nt + i, 0, 0)),
                  pl.BlockSpec((1, D_MODEL), lambda b, i: (0, 0))],
        out_specs=xrow,
        compiler_params=_params("arbitrary", "arbitrary"),
        name="out_ab",
    )(oa, ob, w, x, gt, gpost)


GDN_MISC0 = C_QKV + N_HEADS * DK_C


def _proj_gdn_kernel(x_ref, sh_ref, sc_ref, gpre_ref, w_ref, cw_ref, prev_ref, alog_ref, dtb_ref,
                     q_ref, k_ref, v_ref, gz_ref, misc_ref, tail_ref, zbuf):
    tm = x_ref.shape[0]
    halo = CONV_W - 1
    base = 8

    @pl.when(pl.program_id(1) == 0)
    def _():
        zbuf[base - halo:base, :] = prev_ref[...]

    h = _modulated(x_ref[...], gpre_ref[...], sh_ref[...], sc_ref[...]).astype(BF16)
    zbuf[base:base + tm, :] = _dot(h, w_ref[:, :C_QKV])
    gz_ref[...] = _dot(h, w_ref[:, C_QKV:GDN_MISC0])
    zm = _dot(h, w_ref[:, GDN_MISC0:])
    lane = lax.broadcasted_iota(jnp.int32, zm.shape, 1)
    beta = jax.nn.sigmoid(zm)
    g = -jnp.exp(alog_ref[...]) * jax.nn.softplus(zm + dtb_ref[...])
    misc_ref[...] = jnp.where(lane < N_HEADS, beta, g)

    for part, ref in enumerate((q_ref, k_ref, v_ref)):
        for hd in range(N_HEADS):
            c0 = part * N_HEADS * DK_C + hd * DK_C
            cols = slice(c0, c0 + DK_C)
            conv = zbuf[base:base + tm, cols] * cw_ref[halo:halo + 1, cols]
            for j in range(halo):
                conv = conv + zbuf[base - halo + j:base - halo + j + tm, cols] * cw_ref[j:j + 1, cols]
            a = _silu(conv)
            if part < 2:
                a = a * lax.rsqrt(jnp.sum(a * a, axis=-1, keepdims=True) + EPS)
                if part == 0:
                    a = a * DK_C ** -0.5
            ref[:, hd * DK_C:(hd + 1) * DK_C] = a
    tail = zbuf[base + tm - halo:base + tm, :]
    tail_ref[...] = tail
    zbuf[base - halo:base, :] = tail


def _proj_gdn(x, sh, sc, gpre, w, conv_w, conv_prev, alog_row, dtb_row, *, batch, tm):
    n = x.shape[0]
    t = n // batch
    nt = t // tm
    row = lambda wd: pl.BlockSpec((tm, wd), lambda b, i: (b * nt + i, 0))
    full = lambda a: pl.BlockSpec(a.shape, lambda b, i: (0,) * a.ndim)
    sds = lambda wd: jax.ShapeDtypeStruct((n, wd), F32)
    hd = N_HEADS * DK_C
    tail = pl.BlockSpec((None, CONV_W - 1, C_QKV), lambda b, i: (b, 0, 0))
    return pl.pallas_call(
        _proj_gdn_kernel,
        out_shape=[sds(hd), sds(hd), sds(hd), sds(hd), sds(LANES),
                   jax.ShapeDtypeStruct((batch, CONV_W - 1, C_QKV), F32)],
        grid=(batch, nt),
        in_specs=[row(D_MODEL),
                  pl.BlockSpec((tm // GROUP, 1, D_MODEL), lambda b, i: (b * nt + i, 0, 0)),
                  pl.BlockSpec((tm // GROUP, 1, D_MODEL), lambda b, i: (b * nt + i, 0, 0)),
                  pl.BlockSpec((1, D_MODEL), lambda b, i: (0, 0)),
                  full(w), full(conv_w), tail, full(alog_row), full(dtb_row)],
        out_specs=[row(hd), row(hd), row(hd), row(hd), row(LANES), tail],
        scratch_shapes=[pltpu.VMEM((8 + tm, C_QKV), F32)],
        compiler_params=_params("arbitrary", "arbitrary"),
        name="proj_gdn",
    )(x, sh, sc, gpre, w, conv_w, conv_prev, alog_row, dtb_row)


def _chunk_cumsum_kernel(g_ref, o_ref, *, chunk):
    x = g_ref[...]
    lane = lax.broadcasted_iota(jnp.int32, x.shape, 1) % chunk
    s = 1
    while s < chunk:
        x = x + jnp.where(lane >= s, pltpu.roll(x, s, 1), 0.0)
        s *= 2
    o_ref[...] = x


def _chunk_cumsum(g_rows, chunk):
    assert LANES % chunk == 0
    tiles = g_rows.reshape(-1, LANES)
    return pl.pallas_call(
        functools.partial(_chunk_cumsum_kernel, chunk=chunk),
        out_shape=jax.ShapeDtypeStruct(tiles.shape, F32),
        name="gdn_chunk_cumsum",
    )(tiles).reshape(g_rows.shape)


def _gdn_kernel(q_ref, k_ref, v_ref, col_ref, row_ref, s0_ref, o_ref, s_ref):
    c = q_ref.shape[0]

    @pl.when(pl.program_id(1) == 0)
    def _():
        s_ref[...] = s0_ref[...]

    heads = lambda ref: jnp.stack([ref[:, h * DK_C:(h + 1) * DK_C] for h in range(N_HEADS)])
    q, k, v = heads(q_ref), heads(k_ref), heads(v_ref)
    beta = jnp.stack([col_ref[:, h:h + 1] for h in range(N_HEADS)])
    gc = jnp.stack([col_ref[:, N_HEADS + h:N_HEADS + h + 1] for h in range(N_HEADS)])
    gc_row = row_ref[...]
    gc_last = gc[:, c - 1:c, :]
    ri = lax.broadcasted_iota(jnp.int32, (1, c, c), 1)
    ci = lax.broadcasted_iota(jnp.int32, (1, c, c), 2)
    causal = ri >= ci
    strict = ri > ci
    eye = jnp.where(ri == ci, 1.0, 0.0)
    decay = jnp.exp(jnp.where(causal, gc - gc_row, NEG))
    k16 = k.astype(BF16)
    a = jnp.where(strict, beta * _bmm_nt(k16, k16) * decay, 0.0)
    inv = eye - a
    p = a
    n = 2
    while n < c:
        p = _dot3(p, p, _bmm)
        inv = inv + _dot3(inv, p, _bmm)
        n *= 2
    sol = _dot3(inv, jnp.concatenate([v * beta, k * (beta * jnp.exp(gc))], axis=-1), _bmm)
    u, w = sol[:, :, :DK_C], sol[:, :, DK_C:]
    qk = jnp.where(causal, _bmm_nt(q.astype(BF16), k16) * decay, 0.0)
    q_dec = (q * jnp.exp(gc)).astype(BF16)
    k_dec = k * jnp.exp(gc_last - gc)
    k_dec_t = jnp.stack([k_dec[h].T for h in range(N_HEADS)]).astype(BF16)
    s = s_ref[...]
    s16 = s.astype(BF16)
    v16 = (u - _bmm(w.astype(BF16), s16)).astype(BF16)
    o = _bmm(q_dec, s16) + _bmm(qk.astype(BF16), v16)
    for h in range(N_HEADS):
        o_ref[:, h * DK_C:(h + 1) * DK_C] = o[h]
    s_ref[...] = s * jnp.exp(gc_last) + _bmm(k_dec_t, v16)


def _gdn(q, k, v, col, row, s0, *, batch, chunk):
    n = q.shape[0]
    nc = n // batch // chunk
    hd = N_HEADS * DK_C
    blk = lambda w: pl.BlockSpec((chunk, w), lambda b, i: (b * nc + i, 0))
    state = pl.BlockSpec((None, N_HEADS, DK_C, DK_C), lambda b, i: (b, 0, 0, 0))
    return pl.pallas_call(
        _gdn_kernel,
        out_shape=[jax.ShapeDtypeStruct((n, hd), F32), jax.ShapeDtypeStruct(s0.shape, F32)],
        grid=(batch, nc),
        in_specs=[blk(hd), blk(hd), blk(hd), blk(2 * N_HEADS),
                  pl.BlockSpec((None, N_HEADS, 1, chunk), lambda b, i: (b * nc + i, 0, 0, 0)), state],
        out_specs=[blk(hd), state],
        compiler_params=_params("arbitrary", "arbitrary"),
        name="gated_delta_rule",
    )(q, k, v, col, row, s0)


def _out_gdn_kernel(o_ref, gz_ref, nw_ref, w_ref, x_ref, gt_ref, gpost_ref, out_ref):
    parts = []
    for h in range(N_HEADS):
        cols = slice(h * DK_C, (h + 1) * DK_C)
        o = o_ref[:, cols]
        o = o * lax.rsqrt(jnp.mean(o * o, axis=-1, keepdims=True) + EPS) * nw_ref[...]
        parts.append((o * _silu(gz_ref[:, cols])).astype(BF16))
    y = _dot(jnp.concatenate(parts, axis=-1), w_ref[...])
    out_ref[...] = _residual(x_ref[...], y, gpost_ref[...], gt_ref[...], 1.0)


def _out_gdn(o, gz, norm_w, w, x, gt, gpost, tm=256):
    n = x.shape[0]
    tm = min(tm, n)
    row = pl.BlockSpec((tm, D_MODEL), lambda i: (i, 0))
    return pl.pallas_call(
        _out_gdn_kernel,
        out_shape=jax.ShapeDtypeStruct(x.shape, F32),
        grid=(n // tm,),
        in_specs=[row, row, pl.BlockSpec((1, DK_C), lambda i: (0, 0)), pl.BlockSpec(w.shape, lambda i: (0, 0)),
                  row, pl.BlockSpec((tm // GROUP, 1, D_MODEL), lambda i: (i, 0, 0)),
                  pl.BlockSpec((1, D_MODEL), lambda i: (0, 0))],
        out_specs=row,
        compiler_params=_params("arbitrary"),
        name="out_gdn",
    )(o, gz, norm_w, w, x, gt, gpost)


def _mixer_ab(x, mod, w, past, *, batch, t, bias):
    sh, sc, gt = mod
    (qa, qb, ka, va, kb, vb, ka16, va16, kb16, vb16, qi, misc, kcat) = _proj_ab(
        x, sh, sc, w["pre"], w["ab_main"], w["ab_idx_hi"], w["ab_idx_lo"], batch=batch)
    ki = misc[:, :IDX_DIM]
    fl = misc[:, IDX_DIM + N_HEADS:IDX_DIM + 2 * N_HEADS].reshape(batch, t, N_HEADS)
    t_kv = -(-t // LANES) * LANES
    pad_rows = lambda a: a if t_kv == t else jnp.pad(
        a.reshape(batch, t, -1), ((0, 0), (0, t_kv - t), (0, 0))).reshape(batch * t_kv, -1)
    fl_p = fl if t_kv == t else jnp.pad(fl, ((0, 0), (0, t_kv - t), (0, 0)))
    if past is None:
        logf_b, f_new = _logf(fl_p, w["ab_b_f"], None)
        cache_a = cache_b = None
    else:
        cak, cav, caik, cbk, cbv, cblf = past
        p = cak.shape[1]
        logf_b, f_new, f_cache = _logf(fl_p, w["ab_b_f"], cblf)
        cache_a = (caik, _feature_major(cak), _feature_major(cav))
        cache_b = (_feature_major(cbk), _feature_major(cbv), _unblocked_rows(f_cache))
    logf = _unblocked_rows(logf_b)[:, :t]
    f_new = _unblocked_rows(f_new)
    o_a = _dsa(pad_rows(qa), pad_rows(qi), pad_rows(misc), pad_rows(kcat), pad_rows(ka16), pad_rows(va16), bias, cache_a,
               batch=batch, t_q=t, tq=bias.shape[2])
    o_b = _fox(pad_rows(qb), f_new.reshape(batch * t_kv, N_HEADS), pad_rows(kb16), pad_rows(vb16), f_new, cache_b,
               batch=batch, t_q=t, tq=min(FOX_TQ, t), tk=LANES, tkc=LANES)
    x = _out_ab(o_a, o_b, w["ab_w_out"], x, gt, w["post"], batch=batch, t_q=t, tm=min(256, t))
    return x, (ka, va, ki.reshape(batch, t, IDX_DIM), kb, vb, logf)


def _mixer_gdn(x, mod, w, past, *, batch, t):
    sh, sc, gt = mod
    if past is None:
        conv_prev = jnp.zeros((batch, CONV_W - 1, C_QKV), F32)
        s0 = jnp.zeros((batch, N_HEADS, DK_C, DK_C), F32)
    else:
        conv_prev, s0 = past
    q, k, v, gz, misc, new_conv = _proj_gdn(x, sh, sc, w["pre"], w["gdn_w_in"], w["gdn_conv_w"], conv_prev,
                                            w["gdn_alog_row"], w["gdn_dtb_row"], batch=batch, tm=min(256, t))
    chunk = min(t, CHUNK)
    n = batch * t
    gc_rows = _chunk_cumsum(misc[:, N_HEADS:2 * N_HEADS].T, chunk)
    col = jnp.concatenate([misc[:, :N_HEADS], gc_rows.T], axis=-1)
    row = gc_rows.reshape(N_HEADS, n // chunk, chunk).transpose(1, 0, 2)[:, :, None, :]
    o, s_new = _gdn(q, k, v, col, row, s0, batch=batch, chunk=chunk)
    x = _out_gdn(o, gz, w["gdn_norm_w"], w["gdn_w_out"], x, gt, w["post"])
    return x, (new_conv, s_new)


def _forward(x, mod_all, w, past, *, batch, t):
    new_state = []
    groups = lambda v: jnp.repeat(v, t // GROUP, axis=0)[:, None, :]
    for l in range(mod_all.shape[0]):
        m = mod_all[l].reshape(batch, 3, 3, D_MODEL)
        mod = [[groups(m[:, s, j]) for j in range(3)] for s in range(3)]
        pre = lambda s: w["norm_pre"][l, s][None, :]
        post = lambda s: w["norm_post"][l, s][None, :]
        x = _ffn(x, mod[0], pre(0), post(0), w["ffn_w_gate"][l, 0], w["ffn_w_up"][l, 0], w["ffn_w_down"][l, 0])
        wl = dict(w, pre=pre(1), post=post(1))
        if l % 2 == 0:
            x, st = _mixer_ab(x, mod[1], wl, None if past is None else past[:6], batch=batch, t=t,
                              bias=w["bias_tiles"][min(DSA_TQ, t)])
        else:
            x, st = _mixer_gdn(x, mod[1], wl, None if past is None else past[6:], batch=batch, t=t)
        new_state.extend(st)
        x = _ffn(x, mod[2], pre(2), post(2), w["ffn_w_gate"][l, 1], w["ffn_w_up"][l, 1], w["ffn_w_down"][l, 1])
    return x, new_state


def _prepare_weights(rel_bias, ab_w_in, ab_b_f, ab_w_out, gdn_w_in, gdn_conv_w, gdn_A_log, gdn_dt_bias,
                     gdn_norm_w, gdn_w_out, norm_pre, norm_post, ffn_w_gate, ffn_w_up, ffn_w_down, tqs):
    o = [0]
    for s in (D_HEADS, D_HEADS, D_HEADS, D_HEADS, IDX_DIM, N_HEADS, D_HEADS, D_HEADS, D_HEADS, N_HEADS):
        o.append(o[-1] + s)
    cols = lambda i: ab_w_in[:, o[i]:o[i + 1]]
    qa, ka, va, qi, ki, wi, qb, kb, vb, fl = (cols(i) for i in range(10))
    idx = jnp.concatenate([qi, ki, wi, fl, jnp.zeros((D_MODEL, IDX_COLS - D_HEADS - IDX_DIM - 2 * N_HEADS), F32)], 1)
    idx_hi = idx.astype(BF16)
    gdn_in = jnp.concatenate([gdn_w_in, jnp.zeros((D_MODEL, LANES - 2 * N_HEADS), F32)], 1)
    lane_row = lambda v: jnp.concatenate([jnp.zeros((N_HEADS,), F32), v.astype(F32),
                                          jnp.zeros((LANES - 2 * N_HEADS,), F32)])[None, :]
    return dict(
        ab_main=jnp.concatenate([qa, qb, ka, va, kb, vb], 1).astype(BF16),
        ab_idx_hi=idx_hi, ab_idx_lo=(idx - idx_hi.astype(F32)).astype(BF16),
        ab_b_f=ab_b_f, ab_w_out=ab_w_out.astype(BF16),
        gdn_w_in=gdn_in.astype(BF16), gdn_conv_w=gdn_conv_w,
        gdn_alog_row=lane_row(gdn_A_log), gdn_dtb_row=lane_row(gdn_dt_bias),
        gdn_norm_w=gdn_norm_w[None, :], gdn_w_out=gdn_w_out.astype(BF16),
        norm_pre=norm_pre, norm_post=norm_post,
        ffn_w_gate=ffn_w_gate.astype(BF16), ffn_w_up=ffn_w_up.astype(BF16), ffn_w_down=ffn_w_down.astype(BF16),
        bias_tiles={tq: _bias_tiles(rel_bias, tq) for tq in tqs},
    )


def kernel(x_prompt, x_sample, cache_a_k, cache_a_v, cache_a_idx_k, cache_b_k, cache_b_v, cache_b_logf,
           state_c_conv, state_c_rec, c_prompt, c_sample, rel_bias, ab_w_in, ab_b_f, ab_w_out,
           gdn_w_in, gdn_conv_w, gdn_A_log, gdn_dt_bias, gdn_norm_w, gdn_w_out, norm_pre, norm_post,
           ada_w, ada_b, ffn_w_gate, ffn_w_up, ffn_w_down):
    bp, tp, _ = x_prompt.shape
    bs, ts, _ = x_sample.shape
    w = _prepare_weights(rel_bias, ab_w_in, ab_b_f, ab_w_out, gdn_w_in, gdn_conv_w, gdn_A_log, gdn_dt_bias,
                         gdn_norm_w, gdn_w_out, norm_pre, norm_post, ffn_w_gate, ffn_w_up, ffn_w_down,
                         tqs={min(DSA_TQ, tp), min(DSA_TQ, ts)})
    mod = _ada(jnp.concatenate([c_prompt, c_sample], 0), ada_w, ada_b)
    y_p, st_p = _forward(x_prompt.reshape(bp * tp, D_MODEL), mod[:, :bp], w, None, batch=bp, t=tp)
    past = (cache_a_k, cache_a_v, cache_a_idx_k, cache_b_k, cache_b_v, cache_b_logf, state_c_conv, state_c_rec)
    y_s, st_s = _forward(x_sample.reshape(bs * ts, D_MODEL), mod[:, bp:], w, past, batch=bs, t=ts)
    return (y_p.reshape(bp, tp, D_MODEL), y_s.reshape(bs, ts, D_MODEL), *st_p, *st_s)
```

```python
import functools
import math

import jax
import jax.numpy as jnp
from jax import lax
from jax.experimental import pallas as pl
from jax.experimental.pallas import tpu as pltpu

D_MODEL = 1024
D_FF = 4 * D_MODEL
CHUNK = 64
HEAD_DIM = 64
N_HEADS = 8
D_HEADS = N_HEADS * HEAD_DIM
IDX_DIM = 64
IDX_TOPK_MAX = 256
DK_C = 128
CONV_W = 4
C_QKV = 3 * N_HEADS * DK_C
REL_BUCKETS = 32
REL_MAX_DIST = 256
FFN_RES = 0.5
EPS = 1e-6

GROUP = 32
LANES = 128
NEG = -1e30
INT_MIN = -(2 ** 31)
VMEM_LIMIT = 56 * 1024 * 1024

F32 = jnp.float32
BF16 = jnp.bfloat16


def _params(*sem):
    return pltpu.CompilerParams(dimension_semantics=sem, vmem_limit_bytes=VMEM_LIMIT)


def _dot(a, b):
    return jnp.dot(a, b, preferred_element_type=F32)


def _dot_nt(a, b):
    return lax.dot_general(a, b, (((1,), (1,)), ((), ())), preferred_element_type=F32)


def _bmm(a, b):
    return lax.dot_general(a, b, (((2,), (1,)), ((0,), (0,))), preferred_element_type=F32)


def _bmm_nt(a, b):
    return lax.dot_general(a, b, (((2,), (2,)), ((0,), (0,))), preferred_element_type=F32)


def _split(x):
    hi = x.astype(BF16)
    lo = (x - hi.astype(F32)).astype(BF16)
    return hi, lo


def _dot3(a, b, mm=_dot):
    ah, al = _split(a)
    bh, bl = _split(b)
    return mm(ah, bh) + mm(al, bh) + mm(ah, bl)


def _silu(x):
    return x * jax.nn.sigmoid(x)


def _modulated(x, gpre, sh, sc):
    r = lax.rsqrt(jnp.mean(x * x, axis=-1, keepdims=True) + EPS)
    y = (x * r * gpre).reshape(sh.shape[0], GROUP, x.shape[-1])
    return (y * (1.0 + sc) + sh).reshape(x.shape)


def _residual(x, y, gpost, gate, weight):
    r = lax.rsqrt(jnp.mean(y * y, axis=-1, keepdims=True) + EPS)
    yn = (y * r * gpost).reshape(gate.shape[0], GROUP, x.shape[-1])
    return x + (weight * gate * yn).reshape(x.shape)


def _ada_kernel(c_ref, w_ref, b_ref, o_ref):
    o_ref[...] = _dot3(_silu(c_ref[...]), w_ref[...]) + b_ref[...]


def _ada(c_all, ada_w, ada_b):
    depth, _, n_out = ada_w.shape
    nb = c_all.shape[0]
    tn = 1024
    return pl.pallas_call(
        _ada_kernel,
        out_shape=jax.ShapeDtypeStruct((depth, nb, n_out), F32),
        grid=(depth, n_out // tn),
        in_specs=[pl.BlockSpec((nb, D_MODEL), lambda l, j: (0, 0)),
                  pl.BlockSpec((None, D_MODEL, tn), lambda l, j: (l, 0, j)),
                  pl.BlockSpec((None, 1, tn), lambda l, j: (l, 0, j))],
        out_specs=pl.BlockSpec((None, nb, tn), lambda l, j: (l, 0, j)),
        compiler_params=_params("arbitrary", "arbitrary"),
        name="ada_mod",
    )(c_all, ada_w, ada_b.reshape(depth, 1, n_out))


FFN_SPLIT = 4


def _ffn_kernel(x_ref, sh_ref, sc_ref, gt_ref, gpre_ref, gpost_ref, wg_ref, wu_ref, wd_ref, o_ref,
                h_scr, acc_scr):
    j = pl.program_id(1)

    @pl.when(j == 0)
    def _():
        h_scr[...] = _modulated(x_ref[...], gpre_ref[...], sh_ref[...], sc_ref[...]).astype(BF16)
        acc_scr[...] = jnp.zeros_like(acc_scr)

    h = h_scr[...]
    tf = wg_ref.shape[1]
    part = tf // FFN_SPLIT
    y = None
    for s in range(FFN_SPLIT):
        cols = slice(s * part, (s + 1) * part)
        a = _dot(h, wg_ref[:, cols])
        b = _dot(h, wu_ref[:, cols])
        d = _dot((_silu(a) * b).astype(BF16), wd_ref[cols, :])
        y = d if y is None else y + d
    acc_scr[...] += y

    @pl.when(j == pl.num_programs(1) - 1)
    def _():
        o_ref[...] = _residual(x_ref[...], acc_scr[...], gpost_ref[...], gt_ref[...], FFN_RES)


def _ffn(x, mod, gpre, gpost, wg, wu, wd, tm=512, tf=1024):
    n = x.shape[0]
    tm = min(tm, n)
    sh, sc, gt = mod
    row = pl.BlockSpec((tm, D_MODEL), lambda i, j: (i, 0))
    grp = pl.BlockSpec((tm // GROUP, 1, D_MODEL), lambda i, j: (i, 0, 0))
    vec = pl.BlockSpec((1, D_MODEL), lambda i, j: (0, 0))
    return pl.pallas_call(
        _ffn_kernel,
        out_shape=jax.ShapeDtypeStruct((n, D_MODEL), F32),
        grid=(n // tm, D_FF // tf),
        in_specs=[row, grp, grp, grp, vec, vec,
                  pl.BlockSpec((D_MODEL, tf), lambda i, j: (0, j)),
                  pl.BlockSpec((D_MODEL, tf), lambda i, j: (0, j)),
                  pl.BlockSpec((tf, D_MODEL), lambda i, j: (j, 0))],
        out_specs=row,
        scratch_shapes=[pltpu.VMEM((tm, D_MODEL), BF16), pltpu.VMEM((tm, D_MODEL), F32)],
        compiler_params=_params("arbitrary", "arbitrary"),
        name="ffn_sublayer",
    )(x, sh, sc, gt, gpre, gpost, wg, wu, wd)


AB_MAIN = ("qa", "qb", "ka", "va", "kb", "vb")
IDX_COLS = D_HEADS + LANES


def _proj_ab_kernel(x_ref, sh_ref, sc_ref, gpre_ref, wm_ref, wih_ref, wil_ref,
                    qa_ref, qb_ref, ka_ref, va_ref, kb_ref, vb_ref,
                    ka16_ref, va16_ref, kb16_ref, vb16_ref, qi_ref, misc_ref, kcat_ref, *, feature_major):
    h = _modulated(x_ref[...], gpre_ref[...], sh_ref[...], sc_ref[...])
    hi, lo = _split(h)
    for n, ref in enumerate((qa_ref, qb_ref)):
        ref[...] = _dot(hi, wm_ref[:, n * D_HEADS:(n + 1) * D_HEADS]).astype(BF16)
    for n, (ref, ref16) in enumerate(((ka_ref, ka16_ref), (va_ref, va16_ref),
                                      (kb_ref, kb16_ref), (vb_ref, vb16_ref))):
        z = _dot(hi, wm_ref[:, (n + 2) * D_HEADS:(n + 3) * D_HEADS])
        ref[...] = z.T.reshape(ref.shape) if feature_major else z
        ref16[...] = z.astype(BF16)
    z = _dot(hi, wih_ref[...]) + _dot(lo, wih_ref[...]) + _dot(hi, wil_ref[...])
    qi_ref[...] = z[:, :D_HEADS]
    misc_ref[...] = z[:, D_HEADS:]
    kcat_ref[...] = _split_keys(z[:, D_HEADS:D_HEADS + IDX_DIM])


def _proj_ab(x, sh, sc, gpre, wm, wih, wil, *, batch, tm=256):
    n = x.shape[0]
    t = n // batch
    tm = min(tm, n)
    feature_major = t % tm == 0 and tm % LANES == 0
    row = lambda w: pl.BlockSpec((tm, w), lambda i: (i, 0))
    full = lambda a: pl.BlockSpec(a.shape, lambda i: (0, 0))
    grp = pl.BlockSpec((tm // GROUP, 1, D_MODEL), lambda i: (i, 0, 0))
    sds = lambda w, dt: jax.ShapeDtypeStruct((n, w), dt)
    if feature_major:
        nt = t // tm
        kv_shape = jax.ShapeDtypeStruct((batch, N_HEADS, HEAD_DIM, t), F32)
        kv_spec = pl.BlockSpec((None, N_HEADS, HEAD_DIM, tm), lambda i: (i // nt, 0, 0, i % nt))
    else:
        kv_shape, kv_spec = sds(D_HEADS, F32), row(D_HEADS)
    out_shape = ([sds(D_HEADS, BF16)] * 2 + [kv_shape] * 4 + [sds(D_HEADS, BF16)] * 4
                 + [sds(D_HEADS, F32), sds(LANES, F32), sds(4 * IDX_DIM, BF16)])
    out_specs = [row(D_HEADS)] * 2 + [kv_spec] * 4 + [row(D_HEADS)] * 5 + [row(LANES), row(4 * IDX_DIM)]
    outs = pl.pallas_call(
        functools.partial(_proj_ab_kernel, feature_major=feature_major),
        out_shape=out_shape,
        grid=(n // tm,),
        in_specs=[row(D_MODEL), grp, grp, pl.BlockSpec((1, D_MODEL), lambda i: (0, 0)),
                  full(wm), full(wih), full(wil)],
        out_specs=out_specs,
        compiler_params=_params("arbitrary"),
        name="proj_ab",
    )(x, sh, sc, gpre, wm, wih, wil)
    to_cache = ((lambda a: jnp.transpose(a, (0, 3, 1, 2))) if feature_major
                else (lambda a: a.reshape(batch, t, N_HEADS, HEAD_DIM)))
    return (*outs[:2], *(to_cache(a) for a in outs[2:6]), *outs[6:])


def _lane_cumsum(x):
    lane = lax.broadcasted_iota(jnp.int32, x.shape, 1)
    s = 1
    while s < LANES:
        x = x + jnp.where(lane >= s, pltpu.roll(x, s, 1), 0.0)
        s *= 2
    return x


def _logf_kernel(*refs, n_cache, n_new):
    if n_cache:
        fl_ref, bf_ref, cache_ref, logf_ref, fnew_ref, fcache_ref = refs
    else:
        fl_ref, bf_ref, logf_ref, fnew_ref = refs
    carry = jnp.zeros((N_HEADS, 1), F32)
    for blk in range(n_cache):
        c = _lane_cumsum(cache_ref[blk]) + carry
        fcache_ref[blk] = c
        carry = c[:, LANES - 1:LANES]
    for blk in range(n_new):
        lf = jax.nn.log_sigmoid(fl_ref[blk] + bf_ref[...])
        logf_ref[blk] = lf
        c = _lane_cumsum(lf) + carry
        fnew_ref[blk] = c
        carry = c[:, LANES - 1:LANES]


def _blocked_rows(a):
    b, t, h = a.shape
    return a.reshape(b, t // LANES, LANES, h).transpose(0, 1, 3, 2)


def _unblocked_rows(a):
    b, nb, h, _ = a.shape
    return a.transpose(0, 1, 3, 2).reshape(b, nb * LANES, h)


def _logf(fl, b_f, cache_logf):
    b, t, _ = fl.shape
    n_new = t // LANES
    blk = lambda nb: pl.BlockSpec((None, nb, N_HEADS, LANES), lambda i: (i, 0, 0, 0))
    sds = lambda nb: jax.ShapeDtypeStruct((b, nb, N_HEADS, LANES), F32)
    ins = [_blocked_rows(fl), b_f.reshape(N_HEADS, 1)]
    in_specs = [blk(n_new), pl.BlockSpec((N_HEADS, 1), lambda i: (0, 0))]
    out_shape = [sds(n_new), sds(n_new)]
    out_specs = [blk(n_new), blk(n_new)]
    n_cache = 0
    if cache_logf is not None:
        n_cache = cache_logf.shape[1] // LANES
        ins.append(_blocked_rows(cache_logf))
        in_specs.append(blk(n_cache))
        out_shape.append(sds(n_cache))
        out_specs.append(blk(n_cache))
    outs = pl.pallas_call(
        functools.partial(_logf_kernel, n_cache=n_cache, n_new=n_new),
        out_shape=out_shape, grid=(b,), in_specs=in_specs, out_specs=out_specs,
        compiler_params=_params("arbitrary"), name="logf_cumsum",
    )(*ins)
    return outs


BIAS_FAR = 3


def _t5_bucket(rel):
    nb = REL_BUCKETS // 2
    max_exact = nb // 2
    base = jnp.where(rel > 0, nb, 0)
    n = jnp.abs(rel)
    nf = jnp.maximum(n, 1).astype(F32)
    large = max_exact + (jnp.log(nf / max_exact) / math.log(REL_MAX_DIST / max_exact)
                         * (nb - max_exact)).astype(jnp.int32)
    large = jnp.minimum(large, nb - 1)
    return base + jnp.where(n < max_exact, n, large)


def _bias_kernel(tab_ref, o_ref):
    tq = o_ref.shape[2]
    row = lax.broadcasted_iota(jnp.int32, (tq, LANES), 0)
    col = lax.broadcasted_iota(jnp.int32, (tq, LANES), 1)
    for t in range(o_ref.shape[0]):
        bucket = _t5_bucket(LANES * (t - BIAS_FAR) + col - row)
        for h in range(N_HEADS):
            acc = jnp.zeros((tq, LANES), F32)
            for b in range(REL_BUCKETS):
                acc = jnp.where(bucket == b, tab_ref[b, h], acc)
            o_ref[t, h] = acc


def _bias_tiles(rel_bias, tq):
    assert LANES * BIAS_FAR - (LANES - 1) >= REL_MAX_DIST
    n_tiles = BIAS_FAR + max(1, tq // LANES)
    return pl.pallas_call(
        _bias_kernel,
        out_shape=jax.ShapeDtypeStruct((n_tiles, N_HEADS, tq, LANES), F32),
        in_specs=[pl.BlockSpec(memory_space=pltpu.SMEM)],
        name="rel_bias_tiles",
    )(rel_bias)


def _sortable(x):
    b = pltpu.bitcast(x, jnp.int32)
    return b ^ ((b >> 31) & 0x7FFFFFFF)


def _split_keys(ki):
    hi = ki.astype(BF16).astype(F32)
    return jnp.concatenate([hi, ki - hi, hi, ki - hi], axis=-1).astype(BF16)


def _block_rows(blk, size):
    start = blk * size
    return pl.ds(start if isinstance(start, int) else pl.multiple_of(start, size), size)


def _pair_queries(q16):
    lane = lax.broadcasted_iota(jnp.int32, (q16.shape[0], LANES), 1)
    out = []
    for h in range(N_HEADS):
        pair = q16[:, LANES * (h // 2):LANES * (h // 2 + 1)].astype(F32) * HEAD_DIM ** -0.5
        own = (lane < HEAD_DIM) if h % 2 == 0 else (lane >= HEAD_DIM)
        out.append(jnp.where(own, pair, 0.0).astype(BF16))
    return jnp.stack(out)


def _pair_keys(x):
    x = x.astype(BF16)
    return jnp.stack([x[:, LANES * (h // 2):LANES * (h // 2 + 1)] for h in range(N_HEADS)])


def _pair_values(x):
    x = x.astype(BF16)
    lane = lax.broadcasted_iota(jnp.int32, (x.shape[0], LANES), 1)
    out = []
    for h in range(N_HEADS):
        pair = x[:, LANES * (h // 2):LANES * (h // 2 + 1)]
        own = (lane < HEAD_DIM) if h % 2 == 0 else (lane >= HEAD_DIM)
        out.append(jnp.where(own, pair, jnp.ones_like(pair)))
    return jnp.stack(out)


def _head_queries(q16):
    return jnp.stack([(q16[:, h * HEAD_DIM:(h + 1) * HEAD_DIM].astype(F32) * HEAD_DIM ** -0.5).astype(BF16)
                      for h in range(N_HEADS)])


def _cache_values(vt):
    vt = vt.astype(BF16)
    ones = jnp.ones_like(vt[0])
    return jnp.stack([jnp.concatenate([vt[h], ones] if h % 2 == 0 else [ones, vt[h]], axis=0) for h in range(N_HEADS)])


def _cache_spec(past):
    return pl.BlockSpec((None, N_HEADS, HEAD_DIM, past), lambda b, i: (b, 0, 0, 0))


def _feature_major(cache):
    return jnp.transpose(cache, (0, 2, 3, 1))


def _cache_group(tq, n_cache):
    return next(g for g in (8, 4, 2, 1) if g <= (8 if tq <= 64 else FOX_GROUP) and n_cache % g == 0)


def _block_cols(blk, size):
    return pl.ds(pl.multiple_of(blk * size, size), size)


def _unpair(o):
    lane = lax.broadcasted_iota(jnp.int32, o.shape[1:], 1)
    return jnp.concatenate([jnp.where(lane < HEAD_DIM, o[2 * p], o[2 * p + 1]) for p in range(N_HEADS // 2)], axis=-1)


def _softmax_init(m_scr, acc_scr):
    m_scr[...] = jnp.full(m_scr.shape, NEG, F32)
    acc_scr[...] = jnp.zeros(acc_scr.shape, F32)


def _softmax_step(tiles, m_scr, acc_scr):
    m_prev = m_scr[...]
    top = tiles[0][0]
    for tile in tiles[1:]:
        top = jnp.maximum(top, tile[0])
    m_new = jnp.maximum(m_prev, jnp.max(top, axis=-1, keepdims=True))
    acc = jnp.exp(m_prev - m_new) * acc_scr[...]
    for s, v, feature_major in tiles:
        acc = acc + (_bmm_nt if feature_major else _bmm)(jnp.exp(s - m_new).astype(BF16), v)
    acc_scr[...] = acc
    m_scr[...] = m_new


def _softmax_out(acc_scr):
    acc = acc_scr[...]
    den = jnp.stack([pltpu.roll(acc[h], HEAD_DIM, 1) for h in range(N_HEADS)])
    return _unpair(acc / den)


def _softmax_scratch(tq):
    return [pltpu.VMEM((N_HEADS, tq, LANES), F32), pltpu.VMEM((N_HEADS, tq, LANES), F32)]


def _dsa_kernel(*refs, tq, n_cache, past, t_new, top_k, dyn_new):
    if n_cache:
        (q_ref, qi_ref, mq_ref, cki_ref, cka_ref, cva_ref, nki_ref, nka_ref, nva_ref, bias_ref,
         o_ref, keys_ref, madd_ref, qcat_scr, wi_scr, cand_scr, m_scr, acc_scr) = refs
    else:
        (q_ref, qi_ref, mq_ref, nki_ref, nka_ref, nva_ref, bias_ref,
         o_ref, keys_ref, madd_ref, qcat_scr, wi_scr, cand_scr, m_scr, acc_scr) = refs
        cki_ref = cka_ref = cva_ref = None
    i = pl.program_id(1)
    n_new = (i + 1) * (tq // LANES) if dyn_new else 1
    n_blk = n_cache + n_new
    qpos0 = past + i * tq
    qpos = qpos0 + lax.broadcasted_iota(jnp.int32, (tq, 1), 0)
    col = lax.broadcasted_iota(jnp.int32, (tq, LANES), 1)
    n_keys = past + t_new

    qi = qi_ref[...]
    for h in range(N_HEADS):
        x = qi[:, h * IDX_DIM:(h + 1) * IDX_DIM]
        hi = x.astype(BF16).astype(F32)
        qcat_scr[h * tq:(h + 1) * tq, :] = jnp.concatenate([hi, hi, x - hi, jnp.zeros_like(x)], axis=-1).astype(BF16)
        wi_scr[h] = jnp.broadcast_to(mq_ref[:, IDX_DIM + h:IDX_DIM + h + 1], (tq, LANES))
    score_scale = IDX_DIM ** -0.5 * N_HEADS ** -0.5

    def score_blocks(kcat, kpos0, slot0):
        nb = kcat.shape[0] // LANES
        accs = [jnp.zeros((tq, LANES), F32)] * nb
        for h in range(N_HEADS):
            s = _dot_nt(qcat_scr[h * tq:(h + 1) * tq, :], kcat)
            for d in range(nb):
                accs[d] = accs[d] + jnp.maximum(s[:, d * LANES:(d + 1) * LANES], 0.0) * wi_scr[h]
        for d in range(nb):
            kpos = kpos0 + d * LANES + col
            adm = jnp.logical_and((kpos // CHUNK) <= (qpos // CHUNK), kpos < n_keys)
            keys_ref[slot0 + d] = jnp.where(adm, _sortable(accs[d] * score_scale), INT_MIN)

    if n_cache:
        def cache_scores(j, _):
            score_blocks(_split_keys(cki_ref[_block_rows(j, 2 * LANES), :]), 2 * j * LANES, 2 * j)
            return 0
        lax.fori_loop(0, n_cache // 2, cache_scores, 0)

    if dyn_new:
        def new_scores(j, _):
            score_blocks(nki_ref[_block_rows(j, 2 * LANES), :], past + 2 * j * LANES, n_cache + 2 * j)
            return 0
        lax.fori_loop(0, n_new // 2, new_scores, 0)

        @pl.when(n_new % 2 == 1)
        def _():
            score_blocks(nki_ref[_block_rows(n_new - 1, LANES), :], past + (n_new - 1) * LANES, n_blk - 1)
    else:
        score_blocks(nki_ref[_block_rows(0, LANES), :], past, n_cache)

    def count(cand, strict=False):
        cand_scr[...] = jnp.broadcast_to(cand, (tq, LANES))
        strips = [slice(r, min(r + COUNT_STRIP, tq)) for r in range(0, tq, COUNT_STRIP)]

        def part(first, nb, cs):
            for d in range(nb):
                out = []
                for c, rows in zip(cs, strips):
                    k = keys_ref[first + d, rows, :]
                    hit = (k > cand_scr[rows, :]) if strict else (k >= cand_scr[rows, :])
                    out.append(c + jnp.where(hit, 1.0, 0.0))
                cs = tuple(out)
            return cs

        cs = tuple(jnp.zeros((r.stop - r.start, LANES), F32) for r in strips)
        cs = lax.fori_loop(0, n_blk // COUNT_BLOCKS, lambda t, cs: part(COUNT_BLOCKS * t, COUNT_BLOCKS, cs), cs)
        done = n_blk // COUNT_BLOCKS * COUNT_BLOCKS
        nb = COUNT_BLOCKS // 2
        while nb:
            if nb == 1 and not isinstance(n_blk, int) and (tq // LANES) % 2 == 0 and n_cache % 2 == 0:
                break
            if isinstance(n_blk, int):
                if (n_blk - done) >= nb:
                    cs, done = part(done, nb, cs), done + nb
            else:
                take = (n_blk - done) >= nb
                cs = lax.cond(take, lambda cs, done=done, nb=nb: part(done, nb, cs), lambda cs: cs, cs)
                done = done + jnp.where(take, nb, 0)
            nb //= 2
        return jnp.sum(jnp.concatenate(cs, axis=0), axis=-1, keepdims=True)

    def search(b, ans):
        cand = ans ^ lax.shift_left(jnp.int32(1), jnp.int32(31) - jnp.asarray(b, jnp.int32))
        return jnp.where(count(cand) >= top_k, cand, ans)

    kth = lax.fori_loop(0, 32, search, jnp.full((tq, 1), INT_MIN, jnp.int32))
    kth = jnp.maximum(kth, INT_MIN + 1)
    kth_b = jnp.broadcast_to(kth, (tq, LANES))
    n_ge = count(kth)
    ties_overflow = jnp.max(n_ge) > top_k

    @pl.when(jnp.logical_not(ties_overflow))
    def _():
        def plain_mask(j, _):
            madd_ref[j] = jnp.where(keys_ref[j] >= kth_b, 0.0, NEG)
            return 0
        lax.fori_loop(0, n_blk, plain_mask, 0)

    @pl.when(ties_overflow)
    def _():
        n_gt = count(kth, strict=True)
        quota = top_k - n_gt
        tri = (lax.broadcasted_iota(jnp.int32, (LANES, LANES), 0)
               <= lax.broadcasted_iota(jnp.int32, (LANES, LANES), 1)).astype(BF16)

        def tie_mask(j, seen):
            k = keys_ref[j]
            eq = jnp.where(k == kth_b, 1.0, 0.0)
            rank = _dot(eq.astype(BF16), tri) + seen
            sel = jnp.where(k > kth_b, 1.0, jnp.where(rank <= quota, eq, 0.0))
            madd_ref[j] = (1.0 - sel) * NEG
            return seen + jnp.sum(eq, axis=-1, keepdims=True)
        lax.fori_loop(0, n_blk, tie_mask, jnp.zeros((tq, 1), F32))

    qm = _pair_queries(q_ref[...])
    _softmax_init(m_scr, acc_scr)

    def step(k_ref, v_ref, blocks, slot0, kblk0, cached=False):
        tiles = []
        for blk in blocks:
            tile = jnp.clip(kblk0 + blk - qpos0 // LANES + BIAS_FAR, 0, bias_ref.shape[0] - 1)
            if cached:
                cols = _block_cols(blk, LANES)
                qk, v = _bmm(qh, k_ref[:, :, cols].astype(BF16)), _cache_values(v_ref[:, :, cols])
            else:
                rows = _block_rows(blk, LANES)
                qk, v = _bmm_nt(qm, _pair_keys(k_ref[rows, :])), _pair_values(v_ref[rows, :])
            tiles.append((qk + bias_ref[tile] + madd_ref[slot0 + blk][None], v, cached))
        _softmax_step(tiles, m_scr, acc_scr)

    if n_cache:
        qh = _head_queries(q_ref[...])

        group = _cache_group(tq, n_cache)

        def cache_step(j, _):
            step(cka_ref, cva_ref, [group * j + d for d in range(group)], 0, 0, cached=True)
            return 0
        lax.fori_loop(0, n_cache // group, cache_step, 0)

    if dyn_new:
        @pl.when(n_new % 2 == 1)
        def _():
            madd_ref[n_blk] = jnp.full((tq, LANES), NEG, F32)

        def new_step(j, _):
            step(nka_ref, nva_ref, (2 * j, 2 * j + 1), n_cache, past // LANES)
            return 0
        lax.fori_loop(0, (n_new + 1) // 2, new_step, 0)
    else:
        step(nka_ref, nva_ref, (0,), n_cache, past // LANES)
    o_ref[...] = _softmax_out(acc_scr).astype(BF16)


def _dsa(q, qi, misc, kcat, ka16, va16, bias, cache, *, batch, t_q, tq):
    t_kv = ka16.shape[0] // batch
    nq = t_q // tq
    past = 0 if cache is None else cache[0].shape[1]
    n_cache = past // LANES
    n_keys = past + t_q
    top_k = min(IDX_TOPK_MAX, n_keys // 4)
    assert tq % LANES == 0 or nq == 1
    assert n_cache % 2 == 0 and (cache is not None or (t_kv // LANES) % 2 == 0)
    qrow = lambda w: pl.BlockSpec((tq, w), lambda b, i: (b * (t_kv // tq) + i, 0))
    krow = lambda w: pl.BlockSpec((t_kv, w), lambda b, i: (b, 0))
    crow = lambda w: pl.BlockSpec((None, past, w), lambda b, i: (b, 0, 0))
    ins = [q, qi, misc]
    in_specs = [qrow(D_HEADS), qrow(D_HEADS), qrow(LANES)]
    if cache is not None:
        ins += list(cache)
        in_specs += [crow(IDX_DIM), _cache_spec(past), _cache_spec(past)]
    ins += [kcat, ka16, va16, bias]
    in_specs += [krow(4 * IDX_DIM), krow(D_HEADS), krow(D_HEADS),
                 pl.BlockSpec(bias.shape, lambda b, i: (0, 0, 0, 0))]
    n_blk_max = n_cache + t_kv // LANES
    return pl.pallas_call(
        functools.partial(_dsa_kernel, tq=tq, n_cache=n_cache, past=past, t_new=t_q, top_k=top_k,
                          dyn_new=cache is None),
        out_shape=jax.ShapeDtypeStruct((batch * t_q, D_HEADS), BF16),
        grid=(batch, nq),
        in_specs=in_specs,
        out_specs=pl.BlockSpec((tq, D_HEADS), lambda b, i: (b * nq + i, 0)),
        scratch_shapes=[pltpu.VMEM((n_blk_max, tq, LANES), jnp.int32),
                        pltpu.VMEM((n_blk_max, tq, LANES), F32),
                        pltpu.VMEM((N_HEADS * tq, 4 * IDX_DIM), BF16),
                        pltpu.VMEM((N_HEADS, tq, LANES), F32),
                        pltpu.VMEM((tq, LANES), jnp.int32)] + _softmax_scratch(tq),
        compiler_params=_params("arbitrary", "arbitrary"),
        name="dsa_attention",
    )(*ins)


FOX_GROUP = 4
DSA_TQ = 2 * LANES
COUNT_BLOCKS = 4
COUNT_STRIP = 64
FOX_TQ = 4 * LANES


def _fox_kernel(*refs, tq, tk, tkc, n_cache, past, dyn_new):
    if n_cache:
        q_ref, fq_ref, ck_ref, cv_ref, cf_ref, nk_ref, nv_ref, nf_ref, o_ref, m_scr, acc_scr = refs
    else:
        q_ref, fq_ref, nk_ref, nv_ref, nf_ref, o_ref, m_scr, acc_scr = refs
        ck_ref = cv_ref = cf_ref = None
    i = pl.program_id(1)
    group = tq // tk if dyn_new else 1
    qpos = past + i * tq + lax.broadcasted_iota(jnp.int32, (1, tq, 1), 1)
    col = lax.broadcasted_iota(jnp.int32, (1, 1, LANES), 2)

    qm = _pair_queries(q_ref[...])
    fq = jnp.stack([fq_ref[:, h:h + 1] for h in range(N_HEADS)])
    _softmax_init(m_scr, acc_scr)

    def step(k_ref, v_ref, f_ref, blocks, size, kpos_base=None, cached=False):
        tiles = []
        for blk in blocks:
            if cached:
                cols = _block_cols(blk, size)
                qk, v = _bmm(qh, k_ref[:, :, cols].astype(BF16)), _cache_values(v_ref[:, :, cols])
            else:
                rows = _block_rows(blk, size)
                qk, v = _bmm_nt(qm, _pair_keys(k_ref[rows, :])), _pair_values(v_ref[rows, :])
            s = qk + fq - f_ref[blk]
            if kpos_base is not None:
                s = jnp.where(kpos_base + blk * size + col <= qpos, s, NEG)
            tiles.append((s, v, cached))
        _softmax_step(tiles, m_scr, acc_scr)

    if n_cache:
        qh = _head_queries(q_ref[...])

        group_c = _cache_group(tq, n_cache)

        def cache_step(j, _):
            step(ck_ref, cv_ref, cf_ref, [group_c * j + d for d in range(group_c)], tkc, cached=True)
            return 0
        lax.fori_loop(0, n_cache // group_c, cache_step, 0)

    if dyn_new:
        def new_step(j, _):
            step(nk_ref, nv_ref, nf_ref, [group * j + d for d in range(group)], tk)
            return 0
        lax.fori_loop(0, i, new_step, 0)
        step(nk_ref, nv_ref, nf_ref, [group * i + d for d in range(group)], tk, past)
    else:
        step(nk_ref, nv_ref, nf_ref, [0], tk, past)
    o_ref[...] = _softmax_out(acc_scr).astype(BF16)


def _key_rows(f, tk):
    b, t, h = f.shape
    return f.reshape(b, t // tk, tk, h).transpose(0, 1, 3, 2)[:, :, :, None, :]


def _fox(q, fq, kb16, vb16, f_new, cache, *, batch, t_q, tq, tk, tkc):
    t_kv = kb16.shape[0] // batch
    nq = t_q // tq
    past = 0 if cache is None else cache[2].shape[1]
    n_cache = past // tkc
    assert tk == LANES and tkc == LANES
    qrow = lambda w: pl.BlockSpec((tq, w), lambda b, i: (b * (t_kv // tq) + i, 0))
    krow = lambda w: pl.BlockSpec((t_kv, w), lambda b, i: (b, 0))
    crow = lambda w: pl.BlockSpec((None, past, w), lambda b, i: (b, 0, 0))
    fblk = lambda nb, w: pl.BlockSpec((None, nb, N_HEADS, 1, w), lambda b, i: (b, 0, 0, 0, 0))
    ins = [q, fq]
    in_specs = [qrow(D_HEADS), qrow(N_HEADS)]
    if cache is not None:
        ins += [cache[0], cache[1], _key_rows(cache[2], tkc)]
        in_specs += [_cache_spec(past), _cache_spec(past), fblk(n_cache, tkc)]
    ins += [kb16, vb16, _key_rows(f_new, tk)]
    in_specs += [krow(D_HEADS), krow(D_HEADS), fblk(t_kv // tk, tk)]
    return pl.pallas_call(
        functools.partial(_fox_kernel, tq=tq, tk=tk, tkc=tkc, n_cache=n_cache, past=past, dyn_new=cache is None),
        out_shape=jax.ShapeDtypeStruct((batch * t_q, D_HEADS), BF16),
        grid=(batch, nq),
        in_specs=in_specs,
        out_specs=pl.BlockSpec((tq, D_HEADS), lambda b, i: (b * nq + i, 0)),
        scratch_shapes=_softmax_scratch(tq),
        compiler_params=_params("arbitrary", "arbitrary"),
        name="fox_attention",
    )(*ins)


def _out_ab_kernel(oa_ref, ob_ref, w_ref, x_ref, gt_ref, gpost_ref, o_ref):
    y = _dot(oa_ref[...], w_ref[:D_HEADS, :]) + _dot(ob_ref[...], w_ref[D_HEADS:, :])
    o_ref[...] = _residual(x_ref[...], y, gpost_ref[...], gt_ref[...], 1.0)


def _out_ab(oa, ob, w, x, gt, gpost, *, batch, t_q, tm):
    t_kv = oa.shape[0] // batch
    nt = t_q // tm
    arow = pl.BlockSpec((tm, D_HEADS), lambda b, i: (b * (t_kv // tm) + i, 0))
    xrow = pl.BlockSpec((tm, D_MODEL), lambda b, i: (b * nt + i, 0))
    return pl.pallas_call(
        _out_ab_kernel,
        out_shape=jax.ShapeDtypeStruct(x.shape, F32),
        grid=(batch, nt),
        in_specs=[arow, arow, pl.BlockSpec(w.shape, lambda b, i: (0, 0)), xrow,
                  pl.BlockSpec((tm // GROUP, 1, D_MODEL), lambda b, i: (b * nt + i, 0, 0)),
                  pl.BlockSpec((1, D_MODEL), lambda b, i: (0, 0))],
        out_specs=xrow,
        compiler_params=_params("arbitrary", "arbitrary"),
        name="out_ab",
    )(oa, ob, w, x, gt, gpost)


GDN_MISC0 = C_QKV + N_HEADS * DK_C


def _proj_gdn_kernel(x_ref, sh_ref, sc_ref, gpre_ref, w_ref, cw_ref, prev_ref, alog_ref, dtb_ref,
                     q_ref, k_ref, v_ref, gz_ref, misc_ref, tail_ref, zbuf):
    tm = x_ref.shape[0]
    halo = CONV_W - 1
    base = 8

    @pl.when(pl.program_id(1) == 0)
    def _():
        zbuf[base - halo:base, :] = prev_ref[...]

    h = _modulated(x_ref[...], gpre_ref[...], sh_ref[...], sc_ref[...]).astype(BF16)
    zbuf[base:base + tm, :] = _dot(h, w_ref[:, :C_QKV])
    gz_ref[...] = _dot(h, w_ref[:, C_QKV:GDN_MISC0])
    zm = _dot(h, w_ref[:, GDN_MISC0:])
    lane = lax.broadcasted_iota(jnp.int32, zm.shape, 1)
    beta = jax.nn.sigmoid(zm)
    g = -jnp.exp(alog_ref[...]) * jax.nn.softplus(zm + dtb_ref[...])
    misc_ref[...] = jnp.where(lane < N_HEADS, beta, g)

    for part, ref in enumerate((q_ref, k_ref, v_ref)):
        for hd in range(N_HEADS):
            c0 = part * N_HEADS * DK_C + hd * DK_C
            cols = slice(c0, c0 + DK_C)
            conv = zbuf[base:base + tm, cols] * cw_ref[halo:halo + 1, cols]
            for j in range(halo):
                conv = conv + zbuf[base - halo + j:base - halo + j + tm, cols] * cw_ref[j:j + 1, cols]
            a = _silu(conv)
            if part < 2:
                a = a * lax.rsqrt(jnp.sum(a * a, axis=-1, keepdims=True) + EPS)
                if part == 0:
                    a = a * DK_C ** -0.5
            ref[:, hd * DK_C:(hd + 1) * DK_C] = a
    tail = zbuf[base + tm - halo:base + tm, :]
    tail_ref[...] = tail
    zbuf[base - halo:base, :] = tail


def _proj_gdn(x, sh, sc, gpre, w, conv_w, conv_prev, alog_row, dtb_row, *, batch, tm):
    n = x.shape[0]
    t = n // batch
    nt = t // tm
    row = lambda wd: pl.BlockSpec((tm, wd), lambda b, i: (b * nt + i, 0))
    full = lambda a: pl.BlockSpec(a.shape, lambda b, i: (0,) * a.ndim)
    sds = lambda wd: jax.ShapeDtypeStruct((n, wd), F32)
    hd = N_HEADS * DK_C
    tail = pl.BlockSpec((None, CONV_W - 1, C_QKV), lambda b, i: (b, 0, 0))
    return pl.pallas_call(
        _proj_gdn_kernel,
        out_shape=[sds(hd), sds(hd), sds(hd), sds(hd), sds(LANES),
                   jax.ShapeDtypeStruct((batch, CONV_W - 1, C_QKV), F32)],
        grid=(batch, nt),
        in_specs=[row(D_MODEL),
                  pl.BlockSpec((tm // GROUP, 1, D_MODEL), lambda b, i: (b * nt + i, 0, 0)),
                  pl.BlockSpec((tm // GROUP, 1, D_MODEL), lambda b, i: (b * nt + i, 0, 0)),
                  pl.BlockSpec((1, D_MODEL), lambda b, i: (0, 0)),
                  full(w), full(conv_w), tail, full(alog_row), full(dtb_row)],
        out_specs=[row(hd), row(hd), row(hd), row(hd), row(LANES), tail],
        scratch_shapes=[pltpu.VMEM((8 + tm, C_QKV), F32)],
        compiler_params=_params("arbitrary", "arbitrary"),
        name="proj_gdn",
    )(x, sh, sc, gpre, w, conv_w, conv_prev, alog_row, dtb_row)


def _chunk_cumsum_kernel(g_ref, o_ref, *, chunk):
    x = g_ref[...]
    lane = lax.broadcasted_iota(jnp.int32, x.shape, 1) % chunk
    s = 1
    while s < chunk:
        x = x + jnp.where(lane >= s, pltpu.roll(x, s, 1), 0.0)
        s *= 2
    o_ref[...] = x


def _chunk_cumsum(g_rows, chunk):
    assert LANES % chunk == 0
    tiles = g_rows.reshape(-1, LANES)
    return pl.pallas_call(
        functools.partial(_chunk_cumsum_kernel, chunk=chunk),
        out_shape=jax.ShapeDtypeStruct(tiles.shape, F32),
        name="gdn_chunk_cumsum",
    )(tiles).reshape(g_rows.shape)


GDN_CHUNKS = 4


def _gdn_kernel(q_ref, k_ref, v_ref, col_ref, row_ref, s0_ref, o_ref, s_ref):
    n_chunks, _, _, c = row_ref.shape

    @pl.when(pl.program_id(1) == 0)
    def _():
        s_ref[...] = s0_ref[...]

    def stacked(ref, col0, width):
        return jnp.stack([ref[j * c:(j + 1) * c, col0 + h * width:col0 + (h + 1) * width]
                          for j in range(n_chunks) for h in range(N_HEADS)])
    q, k, v = stacked(q_ref, 0, DK_C), stacked(k_ref, 0, DK_C), stacked(v_ref, 0, DK_C)
    beta = stacked(col_ref, 0, 1)
    gc = stacked(col_ref, N_HEADS, 1)
    gc_row = jnp.concatenate([row_ref[j] for j in range(n_chunks)], axis=0)
    gc_last = gc[:, c - 1:c, :]
    ri = lax.broadcasted_iota(jnp.int32, (1, c, c), 1)
    ci = lax.broadcasted_iota(jnp.int32, (1, c, c), 2)
    causal = ri >= ci
    strict = ri > ci
    eye = jnp.where(ri == ci, 1.0, 0.0)
    decay = jnp.exp(jnp.where(causal, gc - gc_row, NEG))
    k16 = k.astype(BF16)
    a = jnp.where(strict, beta * _bmm_nt(k16, k16) * decay, 0.0)
    inv = eye - a
    p = a
    n = 2
    while n < c:
        p = _dot3(p, p, _bmm)
        inv = inv + _dot3(inv, p, _bmm)
        n *= 2
    sol = _dot3(inv, jnp.concatenate([v * beta, k * (beta * jnp.exp(gc))], axis=-1), _bmm)
    u, w = sol[:, :, :DK_C], sol[:, :, DK_C:]
    qk = jnp.where(causal, _bmm_nt(q.astype(BF16), k16) * decay, 0.0)
    q_dec = (q * jnp.exp(gc)).astype(BF16)
    k_dec = k * jnp.exp(gc_last - gc)
    k_dec_t = jnp.stack([k_dec[i].T for i in range(k_dec.shape[0])]).astype(BF16)
    w16, qk16, g_last = w.astype(BF16), qk.astype(BF16), jnp.exp(gc_last)
    s = s_ref[...]
    for j in range(n_chunks):
        grp = slice(j * N_HEADS, (j + 1) * N_HEADS)
        s16 = s.astype(BF16)
        v16 = (u[grp] - _bmm(w16[grp], s16)).astype(BF16)
        o = _bmm(q_dec[grp], s16) + _bmm(qk16[grp], v16)
        for h in range(N_HEADS):
            o_ref[j * c:(j + 1) * c, h * DK_C:(h + 1) * DK_C] = o[h]
        s = s * g_last[grp] + _bmm(k_dec_t[grp], v16)
    s_ref[...] = s


def _gdn(q, k, v, col, row, s0, *, batch, chunk):
    n = q.shape[0]
    per_step = GDN_CHUNKS if (n // batch // chunk) % GDN_CHUNKS == 0 else 1
    nc = n // batch // chunk // per_step
    hd = N_HEADS * DK_C
    blk = lambda w: pl.BlockSpec((per_step * chunk, w), lambda b, i: (b * nc + i, 0))
    state = pl.BlockSpec((None, N_HEADS, DK_C, DK_C), lambda b, i: (b, 0, 0, 0))
    return pl.pallas_call(
        _gdn_kernel,
        out_shape=[jax.ShapeDtypeStruct((n, hd), F32), jax.ShapeDtypeStruct(s0.shape, F32)],
        grid=(batch, nc),
        in_specs=[blk(hd), blk(hd), blk(hd), blk(2 * N_HEADS),
                  pl.BlockSpec((per_step, N_HEADS, 1, chunk), lambda b, i: (b * nc + i, 0, 0, 0)), state],
        out_specs=[blk(hd), state],
        compiler_params=_params("arbitrary", "arbitrary"),
        name="gated_delta_rule",
    )(q, k, v, col, row, s0)


def _out_gdn_kernel(o_ref, gz_ref, nw_ref, w_ref, x_ref, gt_ref, gpost_ref, out_ref):
    parts = []
    for h in range(N_HEADS):
        cols = slice(h * DK_C, (h + 1) * DK_C)
        o = o_ref[:, cols]
        o = o * lax.rsqrt(jnp.mean(o * o, axis=-1, keepdims=True) + EPS) * nw_ref[...]
        parts.append((o * _silu(gz_ref[:, cols])).astype(BF16))
    y = _dot(jnp.concatenate(parts, axis=-1), w_ref[...])
    out_ref[...] = _residual(x_ref[...], y, gpost_ref[...], gt_ref[...], 1.0)


def _out_gdn(o, gz, norm_w, w, x, gt, gpost, tm=256):
    n = x.shape[0]
    tm = min(tm, n)
    row = pl.BlockSpec((tm, D_MODEL), lambda i: (i, 0))
    return pl.pallas_call(
        _out_gdn_kernel,
        out_shape=jax.ShapeDtypeStruct(x.shape, F32),
        grid=(n // tm,),
        in_specs=[row, row, pl.BlockSpec((1, DK_C), lambda i: (0, 0)), pl.BlockSpec(w.shape, lambda i: (0, 0)),
                  row, pl.BlockSpec((tm // GROUP, 1, D_MODEL), lambda i: (i, 0, 0)),
                  pl.BlockSpec((1, D_MODEL), lambda i: (0, 0))],
        out_specs=row,
        compiler_params=_params("arbitrary"),
        name="out_gdn",
    )(o, gz, norm_w, w, x, gt, gpost)


def _mixer_ab(x, mod, w, past, *, batch, t, bias):
    sh, sc, gt = mod
    (qa, qb, ka, va, kb, vb, ka16, va16, kb16, vb16, qi, misc, kcat) = _proj_ab(
        x, sh, sc, w["pre"], w["ab_main"], w["ab_idx_hi"], w["ab_idx_lo"], batch=batch)
    ki = misc[:, :IDX_DIM]
    fl = misc[:, IDX_DIM + N_HEADS:IDX_DIM + 2 * N_HEADS].reshape(batch, t, N_HEADS)
    t_kv = -(-t // LANES) * LANES
    pad_rows = lambda a: a if t_kv == t else jnp.pad(
        a.reshape(batch, t, -1), ((0, 0), (0, t_kv - t), (0, 0))).reshape(batch * t_kv, -1)
    fl_p = fl if t_kv == t else jnp.pad(fl, ((0, 0), (0, t_kv - t), (0, 0)))
    if past is None:
        logf_b, f_new = _logf(fl_p, w["ab_b_f"], None)
        cache_a = cache_b = None
    else:
        cak, cav, caik, cbk, cbv, cblf = past
        p = cak.shape[1]
        logf_b, f_new, f_cache = _logf(fl_p, w["ab_b_f"], cblf)
        cache_a = (caik, _feature_major(cak), _feature_major(cav))
        cache_b = (_feature_major(cbk), _feature_major(cbv), _unblocked_rows(f_cache))
    logf = _unblocked_rows(logf_b)[:, :t]
    f_new = _unblocked_rows(f_new)
    o_a = _dsa(pad_rows(qa), pad_rows(qi), pad_rows(misc), pad_rows(kcat), pad_rows(ka16), pad_rows(va16), bias, cache_a,
               batch=batch, t_q=t, tq=bias.shape[2])
    o_b = _fox(pad_rows(qb), f_new.reshape(batch * t_kv, N_HEADS), pad_rows(kb16), pad_rows(vb16), f_new, cache_b,
               batch=batch, t_q=t, tq=min(FOX_TQ, t), tk=LANES, tkc=LANES)
    x = _out_ab(o_a, o_b, w["ab_w_out"], x, gt, w["post"], batch=batch, t_q=t, tm=min(256, t))
    return x, (ka, va, ki.reshape(batch, t, IDX_DIM), kb, vb, logf)


def _mixer_gdn(x, mod, w, past, *, batch, t):
    sh, sc, gt = mod
    if past is None:
        conv_prev = jnp.zeros((batch, CONV_W - 1, C_QKV), F32)
        s0 = jnp.zeros((batch, N_HEADS, DK_C, DK_C), F32)
    else:
        conv_prev, s0 = past
    q, k, v, gz, misc, new_conv = _proj_gdn(x, sh, sc, w["pre"], w["gdn_w_in"], w["gdn_conv_w"], conv_prev,
                                            w["gdn_alog_row"], w["gdn_dtb_row"], batch=batch, tm=min(256, t))
    chunk = min(t, CHUNK)
    n = batch * t
    gc_rows = _chunk_cumsum(misc[:, N_HEADS:2 * N_HEADS].T, chunk)
    col = jnp.concatenate([misc[:, :N_HEADS], gc_rows.T], axis=-1)
    row = gc_rows.reshape(N_HEADS, n // chunk, chunk).transpose(1, 0, 2)[:, :, None, :]
    o, s_new = _gdn(q, k, v, col, row, s0, batch=batch, chunk=chunk)
    x = _out_gdn(o, gz, w["gdn_norm_w"], w["gdn_w_out"], x, gt, w["post"])
    return x, (new_conv, s_new)


def _forward(x, mod_all, w, past, *, batch, t):
    new_state = []
    groups = lambda v: jnp.repeat(v, t // GROUP, axis=0)[:, None, :]
    for l in range(mod_all.shape[0]):
        m = mod_all[l].reshape(batch, 3, 3, D_MODEL)
        mod = [[groups(m[:, s, j]) for j in range(3)] for s in range(3)]
        pre = lambda s: w["norm_pre"][l, s][None, :]
        post = lambda s: w["norm_post"][l, s][None, :]
        x = _ffn(x, mod[0], pre(0), post(0), w["ffn_w_gate"][l, 0], w["ffn_w_up"][l, 0], w["ffn_w_down"][l, 0])
        wl = dict(w, pre=pre(1), post=post(1))
        if l % 2 == 0:
            x, st = _mixer_ab(x, mod[1], wl, None if past is None else past[:6], batch=batch, t=t,
                              bias=w["bias_tiles"][min(DSA_TQ, t)])
        else:
            x, st = _mixer_gdn(x, mod[1], wl, None if past is None else past[6:], batch=batch, t=t)
        new_state.extend(st)
        x = _ffn(x, mod[2], pre(2), post(2), w["ffn_w_gate"][l, 1], w["ffn_w_up"][l, 1], w["ffn_w_down"][l, 1])
    return x, new_state


def _prepare_weights(rel_bias, ab_w_in, ab_b_f, ab_w_out, gdn_w_in, gdn_conv_w, gdn_A_log, gdn_dt_bias,
                     gdn_norm_w, gdn_w_out, norm_pre, norm_post, ffn_w_gate, ffn_w_up, ffn_w_down, tqs):
    o = [0]
    for s in (D_HEADS, D_HEADS, D_HEADS, D_HEADS, IDX_DIM, N_HEADS, D_HEADS, D_HEADS, D_HEADS, N_HEADS):
        o.append(o[-1] + s)
    cols = lambda i: ab_w_in[:, o[i]:o[i + 1]]
    qa, ka, va, qi, ki, wi, qb, kb, vb, fl = (cols(i) for i in range(10))
    idx = jnp.concatenate([qi, ki, wi, fl, jnp.zeros((D_MODEL, IDX_COLS - D_HEADS - IDX_DIM - 2 * N_HEADS), F32)], 1)
    idx_hi = idx.astype(BF16)
    gdn_in = jnp.concatenate([gdn_w_in, jnp.zeros((D_MODEL, LANES - 2 * N_HEADS), F32)], 1)
    lane_row = lambda v: jnp.concatenate([jnp.zeros((N_HEADS,), F32), v.astype(F32),
                                          jnp.zeros((LANES - 2 * N_HEADS,), F32)])[None, :]
    return dict(
        ab_main=jnp.concatenate([qa, qb, ka, va, kb, vb], 1).astype(BF16),
        ab_idx_hi=idx_hi, ab_idx_lo=(idx - idx_hi.astype(F32)).astype(BF16),
        ab_b_f=ab_b_f, ab_w_out=ab_w_out.astype(BF16),
        gdn_w_in=gdn_in.astype(BF16), gdn_conv_w=gdn_conv_w,
        gdn_alog_row=lane_row(gdn_A_log), gdn_dtb_row=lane_row(gdn_dt_bias),
        gdn_norm_w=gdn_norm_w[None, :], gdn_w_out=gdn_w_out.astype(BF16),
        norm_pre=norm_pre, norm_post=norm_post,
        ffn_w_gate=ffn_w_gate.astype(BF16), ffn_w_up=ffn_w_up.astype(BF16), ffn_w_down=ffn_w_down.astype(BF16),
        bias_tiles={tq: _bias_tiles(rel_bias, tq) for tq in tqs},
    )


def kernel(x_prompt, x_sample, cache_a_k, cache_a_v, cache_a_idx_k, cache_b_k, cache_b_v, cache_b_logf,
           state_c_conv, state_c_rec, c_prompt, c_sample, rel_bias, ab_w_in, ab_b_f, ab_w_out,
           gdn_w_in, gdn_conv_w, gdn_A_log, gdn_dt_bias, gdn_norm_w, gdn_w_out, norm_pre, norm_post,
           ada_w, ada_b, ffn_w_gate, ffn_w_up, ffn_w_down):
    bp, tp, _ = x_prompt.shape
    bs, ts, _ = x_sample.shape
    w = _prepare_weights(rel_bias, ab_w_in, ab_b_f, ab_w_out, gdn_w_in, gdn_conv_w, gdn_A_log, gdn_dt_bias,
                         gdn_norm_w, gdn_w_out, norm_pre, norm_post, ffn_w_gate, ffn_w_up, ffn_w_down,
                         tqs={min(DSA_TQ, tp), min(DSA_TQ, ts)})
    mod = _ada(jnp.concatenate([c_prompt, c_sample], 0), ada_w, ada_b)
    y_p, st_p = _forward(x_prompt.reshape(bp * tp, D_MODEL), mod[:, :bp], w, None, batch=bp, t=tp)
    past = (cache_a_k, cache_a_v, cache_a_idx_k, cache_b_k, cache_b_v, cache_b_logf, state_c_conv, state_c_rec)
    y_s, st_s = _forward(x_sample.reshape(bs * ts, D_MODEL), mod[:, bp:], w, past, batch=bs, t=ts)
    return (y_p.reshape(bp, tp, D_MODEL), y_s.reshape(bs, ts, D_MODEL), *st_p, *st_s)
```

```python
import functools
import math

import jax
import jax.numpy as jnp
from jax import lax
from jax.experimental import pallas as pl
from jax.experimental.pallas import tpu as pltpu

D_MODEL = 1024
D_FF = 4 * D_MODEL
CHUNK = 64
HEAD_DIM = 64
N_HEADS = 8
D_HEADS = N_HEADS * HEAD_DIM
IDX_DIM = 64
IDX_TOPK_MAX = 256
DK_C = 128
CONV_W = 4
C_QKV = 3 * N_HEADS * DK_C
REL_BUCKETS = 32
REL_MAX_DIST = 256
FFN_RES = 0.5
EPS = 1e-6

GROUP = 32
LANES = 128
NEG = -1e30
INT_MIN = -(2 ** 31)
VMEM_LIMIT = 56 * 1024 * 1024

F32 = jnp.float32
BF16 = jnp.bfloat16


def _params(*sem):
    return pltpu.CompilerParams(dimension_semantics=sem, vmem_limit_bytes=VMEM_LIMIT)


def _dot(a, b):
    return jnp.dot(a, b, preferred_element_type=F32)


def _dot_nt(a, b):
    return lax.dot_general(a, b, (((1,), (1,)), ((), ())), preferred_element_type=F32)


def _bmm(a, b):
    return lax.dot_general(a, b, (((2,), (1,)), ((0,), (0,))), preferred_element_type=F32)


def _bmm_nt(a, b):
    return lax.dot_general(a, b, (((2,), (2,)), ((0,), (0,))), preferred_element_type=F32)


def _split(x):
    hi = x.astype(BF16)
    lo = (x - hi.astype(F32)).astype(BF16)
    return hi, lo


def _dot3(a, b, mm=_dot):
    ah, al = _split(a)
    bh, bl = _split(b)
    return mm(ah, bh) + mm(al, bh) + mm(ah, bl)


def _silu(x):
    return x * jax.nn.sigmoid(x)


def _modulated(x, gpre, sh, sc):
    r = lax.rsqrt(jnp.mean(x * x, axis=-1, keepdims=True) + EPS)
    y = (x * r * gpre).reshape(sh.shape[0], GROUP, x.shape[-1])
    return (y * (1.0 + sc) + sh).reshape(x.shape)


def _residual(x, y, gpost, gate, weight):
    r = lax.rsqrt(jnp.mean(y * y, axis=-1, keepdims=True) + EPS)
    yn = (y * r * gpost).reshape(gate.shape[0], GROUP, x.shape[-1])
    return x + (weight * gate * yn).reshape(x.shape)


def _ada_kernel(c_ref, w_ref, b_ref, o_ref):
    o_ref[...] = _dot3(_silu(c_ref[...]), w_ref[...]) + b_ref[...]


def _ada(c_all, ada_w, ada_b):
    depth, _, n_out = ada_w.shape
    nb = c_all.shape[0]
    tn = 1024
    return pl.pallas_call(
        _ada_kernel,
        out_shape=jax.ShapeDtypeStruct((depth, nb, n_out), F32),
        grid=(depth, n_out // tn),
        in_specs=[pl.BlockSpec((nb, D_MODEL), lambda l, j: (0, 0)),
                  pl.BlockSpec((None, D_MODEL, tn), lambda l, j: (l, 0, j)),
                  pl.BlockSpec((None, 1, tn), lambda l, j: (l, 0, j))],
        out_specs=pl.BlockSpec((None, nb, tn), lambda l, j: (l, 0, j)),
        compiler_params=_params("arbitrary", "arbitrary"),
        name="ada_mod",
    )(c_all, ada_w, ada_b.reshape(depth, 1, n_out))


FFN_SPLIT = 4


def _ffn_kernel(x_ref, sh_ref, sc_ref, gt_ref, gpre_ref, gpost_ref, wg_ref, wu_ref, wd_ref, o_ref,
                h_scr, acc_scr):
    j = pl.program_id(1)

    @pl.when(j == 0)
    def _():
        h_scr[...] = _modulated(x_ref[...], gpre_ref[...], sh_ref[...], sc_ref[...]).astype(BF16)
        acc_scr[...] = jnp.zeros_like(acc_scr)

    h = h_scr[...]
    tf = wg_ref.shape[1]
    part = tf // FFN_SPLIT
    y = None
    for s in range(FFN_SPLIT):
        cols = slice(s * part, (s + 1) * part)
        a = _dot(h, wg_ref[:, cols])
        b = _dot(h, wu_ref[:, cols])
        d = _dot((_silu(a) * b).astype(BF16), wd_ref[cols, :])
        y = d if y is None else y + d
    acc_scr[...] += y

    @pl.when(j == pl.num_programs(1) - 1)
    def _():
        o_ref[...] = _residual(x_ref[...], acc_scr[...], gpost_ref[...], gt_ref[...], FFN_RES)


def _ffn(x, mod, gpre, gpost, wg, wu, wd, tm=512, tf=1024):
    n = x.shape[0]
    tm = min(tm, n)
    sh, sc, gt = mod
    row = pl.BlockSpec((tm, D_MODEL), lambda i, j: (i, 0))
    grp = pl.BlockSpec((tm // GROUP, 1, D_MODEL), lambda i, j: (i, 0, 0))
    vec = pl.BlockSpec((1, D_MODEL), lambda i, j: (0, 0))
    return pl.pallas_call(
        _ffn_kernel,
        out_shape=jax.ShapeDtypeStruct((n, D_MODEL), F32),
        grid=(n // tm, D_FF // tf),
        in_specs=[row, grp, grp, grp, vec, vec,
                  pl.BlockSpec((D_MODEL, tf), lambda i, j: (0, j)),
                  pl.BlockSpec((D_MODEL, tf), lambda i, j: (0, j)),
                  pl.BlockSpec((tf, D_MODEL), lambda i, j: (j, 0))],
        out_specs=row,
        scratch_shapes=[pltpu.VMEM((tm, D_MODEL), BF16), pltpu.VMEM((tm, D_MODEL), F32)],
        compiler_params=_params("arbitrary", "arbitrary"),
        name="ffn_sublayer",
    )(x, sh, sc, gt, gpre, gpost, wg, wu, wd)


AB_MAIN = ("qa", "qb", "ka", "va", "kb", "vb")
IDX_COLS = D_HEADS + LANES


def _proj_ab_kernel(x_ref, sh_ref, sc_ref, gpre_ref, wm_ref, wih_ref, wil_ref,
                    qa_ref, qb_ref, ka_ref, va_ref, kb_ref, vb_ref,
                    ka16_ref, va16_ref, kb16_ref, vb16_ref, qi_ref, misc_ref, kcat_ref, *, feature_major):
    h = _modulated(x_ref[...], gpre_ref[...], sh_ref[...], sc_ref[...])
    hi, lo = _split(h)
    for n, ref in enumerate((qa_ref, qb_ref)):
        ref[...] = _dot(hi, wm_ref[:, n * D_HEADS:(n + 1) * D_HEADS]).astype(BF16)
    for n, (ref, ref16) in enumerate(((ka_ref, ka16_ref), (va_ref, va16_ref),
                                      (kb_ref, kb16_ref), (vb_ref, vb16_ref))):
        z = _dot(hi, wm_ref[:, (n + 2) * D_HEADS:(n + 3) * D_HEADS])
        ref[...] = z.T.reshape(ref.shape) if feature_major else z
        ref16[...] = z.astype(BF16)
    z = _dot(hi, wih_ref[...]) + _dot(lo, wih_ref[...]) + _dot(hi, wil_ref[...])
    qi_ref[...] = z[:, :D_HEADS]
    misc_ref[...] = z[:, D_HEADS:]
    kcat_ref[...] = _split_keys(z[:, D_HEADS:D_HEADS + IDX_DIM])


def _proj_ab(x, sh, sc, gpre, wm, wih, wil, *, batch, tm=512):
    n = x.shape[0]
    t = n // batch
    tm = min(tm, n)
    feature_major = t % tm == 0 and tm % LANES == 0
    row = lambda w: pl.BlockSpec((tm, w), lambda i: (i, 0))
    full = lambda a: pl.BlockSpec(a.shape, lambda i: (0, 0))
    grp = pl.BlockSpec((tm // GROUP, 1, D_MODEL), lambda i: (i, 0, 0))
    sds = lambda w, dt: jax.ShapeDtypeStruct((n, w), dt)
    if feature_major:
        nt = t // tm
        kv_shape = jax.ShapeDtypeStruct((batch, N_HEADS, HEAD_DIM, t), F32)
        kv_spec = pl.BlockSpec((None, N_HEADS, HEAD_DIM, tm), lambda i: (i // nt, 0, 0, i % nt))
    else:
        kv_shape, kv_spec = sds(D_HEADS, F32), row(D_HEADS)
    out_shape = ([sds(D_HEADS, BF16)] * 2 + [kv_shape] * 4 + [sds(D_HEADS, BF16)] * 4
                 + [sds(D_HEADS, F32), sds(LANES, F32), sds(4 * IDX_DIM, BF16)])
    out_specs = [row(D_HEADS)] * 2 + [kv_spec] * 4 + [row(D_HEADS)] * 5 + [row(LANES), row(4 * IDX_DIM)]
    outs = pl.pallas_call(
        functools.partial(_proj_ab_kernel, feature_major=feature_major),
        out_shape=out_shape,
        grid=(n // tm,),
        in_specs=[row(D_MODEL), grp, grp, pl.BlockSpec((1, D_MODEL), lambda i: (0, 0)),
                  full(wm), full(wih), full(wil)],
        out_specs=out_specs,
        compiler_params=_params("arbitrary"),
        name="proj_ab",
    )(x, sh, sc, gpre, wm, wih, wil)
    to_cache = ((lambda a: jnp.transpose(a, (0, 3, 1, 2))) if feature_major
                else (lambda a: a.reshape(batch, t, N_HEADS, HEAD_DIM)))
    return (*outs[:2], *(to_cache(a) for a in outs[2:6]), *outs[6:])


def _lane_cumsum(x):
    lane = lax.broadcasted_iota(jnp.int32, x.shape, 1)
    s = 1
    while s < LANES:
        x = x + jnp.where(lane >= s, pltpu.roll(x, s, 1), 0.0)
        s *= 2
    return x


def _logf_kernel(*refs, n_cache, n_new):
    if n_cache:
        fl_ref, bf_ref, cache_ref, logf_ref, fnew_ref, fcache_ref = refs
    else:
        fl_ref, bf_ref, logf_ref, fnew_ref = refs
    carry = jnp.zeros((N_HEADS, 1), F32)
    for blk in range(n_cache):
        c = _lane_cumsum(cache_ref[blk]) + carry
        fcache_ref[blk] = c
        carry = c[:, LANES - 1:LANES]
    for blk in range(n_new):
        lf = jax.nn.log_sigmoid(fl_ref[blk] + bf_ref[...])
        logf_ref[blk] = lf
        c = _lane_cumsum(lf) + carry
        fnew_ref[blk] = c
        carry = c[:, LANES - 1:LANES]


def _blocked_rows(a):
    b, t, h = a.shape
    return a.reshape(b, t // LANES, LANES, h).transpose(0, 1, 3, 2)


def _unblocked_rows(a):
    b, nb, h, _ = a.shape
    return a.transpose(0, 1, 3, 2).reshape(b, nb * LANES, h)


def _logf(fl, b_f, cache_logf):
    b, t, _ = fl.shape
    n_new = t // LANES
    blk = lambda nb: pl.BlockSpec((None, nb, N_HEADS, LANES), lambda i: (i, 0, 0, 0))
    sds = lambda nb: jax.ShapeDtypeStruct((b, nb, N_HEADS, LANES), F32)
    ins = [_blocked_rows(fl), b_f.reshape(N_HEADS, 1)]
    in_specs = [blk(n_new), pl.BlockSpec((N_HEADS, 1), lambda i: (0, 0))]
    out_shape = [sds(n_new), sds(n_new)]
    out_specs = [blk(n_new), blk(n_new)]
    n_cache = 0
    if cache_logf is not None:
        n_cache = cache_logf.shape[1] // LANES
        ins.append(_blocked_rows(cache_logf))
        in_specs.append(blk(n_cache))
        out_shape.append(sds(n_cache))
        out_specs.append(blk(n_cache))
    outs = pl.pallas_call(
        functools.partial(_logf_kernel, n_cache=n_cache, n_new=n_new),
        out_shape=out_shape, grid=(b,), in_specs=in_specs, out_specs=out_specs,
        compiler_params=_params("arbitrary"), name="logf_cumsum",
    )(*ins)
    return outs


BIAS_FAR = 3


def _t5_bucket(rel):
    nb = REL_BUCKETS // 2
    max_exact = nb // 2
    base = jnp.where(rel > 0, nb, 0)
    n = jnp.abs(rel)
    nf = jnp.maximum(n, 1).astype(F32)
    large = max_exact + (jnp.log(nf / max_exact) / math.log(REL_MAX_DIST / max_exact)
                         * (nb - max_exact)).astype(jnp.int32)
    large = jnp.minimum(large, nb - 1)
    return base + jnp.where(n < max_exact, n, large)


def _bias_kernel(tab_ref, o_ref):
    tq = o_ref.shape[2]
    row = lax.broadcasted_iota(jnp.int32, (tq, LANES), 0)
    col = lax.broadcasted_iota(jnp.int32, (tq, LANES), 1)
    for t in range(o_ref.shape[0]):
        bucket = _t5_bucket(LANES * (t - BIAS_FAR) + col - row)
        for h in range(N_HEADS):
            acc = jnp.zeros((tq, LANES), F32)
            for b in range(REL_BUCKETS):
                acc = jnp.where(bucket == b, tab_ref[b, h], acc)
            o_ref[t, h] = acc


def _bias_tiles(rel_bias, tq):
    assert LANES * BIAS_FAR - (LANES - 1) >= REL_MAX_DIST
    n_tiles = BIAS_FAR + max(1, tq // LANES)
    return pl.pallas_call(
        _bias_kernel,
        out_shape=jax.ShapeDtypeStruct((n_tiles, N_HEADS, tq, LANES), F32),
        in_specs=[pl.BlockSpec(memory_space=pltpu.SMEM)],
        name="rel_bias_tiles",
    )(rel_bias)


def _sortable(x):
    b = pltpu.bitcast(x, jnp.int32)
    return b ^ ((b >> 31) & 0x7FFFFFFF)


def _split_keys(ki):
    hi = ki.astype(BF16).astype(F32)
    return jnp.concatenate([hi, ki - hi, hi, ki - hi], axis=-1).astype(BF16)


def _block_rows(blk, size):
    start = blk * size
    return pl.ds(start if isinstance(start, int) else pl.multiple_of(start, size), size)


def _pair_queries(q16):
    lane = lax.broadcasted_iota(jnp.int32, (q16.shape[0], LANES), 1)
    out = []
    for h in range(N_HEADS):
        pair = q16[:, LANES * (h // 2):LANES * (h // 2 + 1)].astype(F32) * HEAD_DIM ** -0.5
        own = (lane < HEAD_DIM) if h % 2 == 0 else (lane >= HEAD_DIM)
        out.append(jnp.where(own, pair, 0.0).astype(BF16))
    return jnp.stack(out)


def _pair_keys(x):
    x = x.astype(BF16)
    return jnp.stack([x[:, LANES * (h // 2):LANES * (h // 2 + 1)] for h in range(N_HEADS)])


def _pair_values(x):
    x = x.astype(BF16)
    lane = lax.broadcasted_iota(jnp.int32, (x.shape[0], LANES), 1)
    out = []
    for h in range(N_HEADS):
        pair = x[:, LANES * (h // 2):LANES * (h // 2 + 1)]
        own = (lane < HEAD_DIM) if h % 2 == 0 else (lane >= HEAD_DIM)
        out.append(jnp.where(own, pair, jnp.ones_like(pair)))
    return jnp.stack(out)


def _head_queries(q16):
    return jnp.stack([(q16[:, h * HEAD_DIM:(h + 1) * HEAD_DIM].astype(F32) * HEAD_DIM ** -0.5).astype(BF16)
                      for h in range(N_HEADS)])


def _cache_values(vt):
    vt = vt.astype(BF16)
    ones = jnp.ones_like(vt[0])
    return jnp.stack([jnp.concatenate([vt[h], ones] if h % 2 == 0 else [ones, vt[h]], axis=0) for h in range(N_HEADS)])


def _cache_spec(past):
    return pl.BlockSpec((None, N_HEADS, HEAD_DIM, past), lambda b, i: (b, 0, 0, 0))


def _feature_major(cache):
    return jnp.transpose(cache, (0, 2, 3, 1))


def _cache_group(tq, n_cache):
    return next(g for g in (8, 4, 2, 1) if g <= (8 if tq <= 64 else FOX_GROUP) and n_cache % g == 0)


def _block_cols(blk, size):
    return pl.ds(pl.multiple_of(blk * size, size), size)


def _unpair(o):
    lane = lax.broadcasted_iota(jnp.int32, o.shape[1:], 1)
    return jnp.concatenate([jnp.where(lane < HEAD_DIM, o[2 * p], o[2 * p + 1]) for p in range(N_HEADS // 2)], axis=-1)


def _softmax_init(m_scr, acc_scr):
    m_scr[...] = jnp.full(m_scr.shape, NEG, F32)
    acc_scr[...] = jnp.zeros(acc_scr.shape, F32)


def _softmax_step(tiles, m_scr, acc_scr):
    m_prev = m_scr[...]
    top = tiles[0][0]
    for tile in tiles[1:]:
        top = jnp.maximum(top, tile[0])
    m_new = jnp.maximum(m_prev, jnp.max(top, axis=-1, keepdims=True))
    acc = jnp.exp(m_prev - m_new) * acc_scr[...]
    for s, v, feature_major in tiles:
        acc = acc + (_bmm_nt if feature_major else _bmm)(jnp.exp(s - m_new).astype(BF16), v)
    acc_scr[...] = acc
    m_scr[...] = m_new


def _softmax_out(acc_scr):
    acc = acc_scr[...]
    den = jnp.stack([pltpu.roll(acc[h], HEAD_DIM, 1) for h in range(N_HEADS)])
    return _unpair(acc / den)


def _softmax_scratch(tq):
    return [pltpu.VMEM((N_HEADS, tq, LANES), F32), pltpu.VMEM((N_HEADS, tq, LANES), F32)]


def _dsa_kernel(*refs, tq, n_cache, past, t_new, top_k, dyn_new):
    if n_cache:
        (q_ref, qi_ref, mq_ref, cki_ref, cka_ref, cva_ref, nki_ref, nka_ref, nva_ref, bias_ref,
         o_ref, keys_ref, madd_ref, qcat_scr, wi_scr, cand_scr, m_scr, acc_scr) = refs
    else:
        (q_ref, qi_ref, mq_ref, nki_ref, nka_ref, nva_ref, bias_ref,
         o_ref, keys_ref, madd_ref, qcat_scr, wi_scr, cand_scr, m_scr, acc_scr) = refs
        cki_ref = cka_ref = cva_ref = None
    i = pl.program_id(1)
    n_new = (i + 1) * (tq // LANES) if dyn_new else 1
    n_blk = n_cache + n_new
    qpos0 = past + i * tq
    qpos = qpos0 + lax.broadcasted_iota(jnp.int32, (tq, 1), 0)
    col = lax.broadcasted_iota(jnp.int32, (tq, LANES), 1)
    n_keys = past + t_new

    qi = qi_ref[...]
    for h in range(N_HEADS):
        x = qi[:, h * IDX_DIM:(h + 1) * IDX_DIM]
        hi = x.astype(BF16).astype(F32)
        qcat_scr[h * tq:(h + 1) * tq, :] = jnp.concatenate([hi, hi, x - hi, jnp.zeros_like(x)], axis=-1).astype(BF16)
        wi_scr[h] = jnp.broadcast_to(mq_ref[:, IDX_DIM + h:IDX_DIM + h + 1], (tq, LANES))
    score_scale = IDX_DIM ** -0.5 * N_HEADS ** -0.5

    def score_blocks(kcat, kpos0, slot0):
        nb = kcat.shape[0] // LANES
        accs = [jnp.zeros((tq, LANES), F32)] * nb
        for h in range(N_HEADS):
            s = _dot_nt(qcat_scr[h * tq:(h + 1) * tq, :], kcat)
            for d in range(nb):
                accs[d] = accs[d] + jnp.maximum(s[:, d * LANES:(d + 1) * LANES], 0.0) * wi_scr[h]
        for d in range(nb):
            kpos = kpos0 + d * LANES + col
            adm = jnp.logical_and((kpos // CHUNK) <= (qpos // CHUNK), kpos < n_keys)
            keys_ref[slot0 + d] = jnp.where(adm, _sortable(accs[d] * score_scale), INT_MIN)

    if n_cache:
        def cache_scores(j, _):
            score_blocks(_split_keys(cki_ref[_block_rows(j, 2 * LANES), :]), 2 * j * LANES, 2 * j)
            return 0
        lax.fori_loop(0, n_cache // 2, cache_scores, 0)

    if dyn_new:
        def new_scores(j, _):
            score_blocks(nki_ref[_block_rows(j, 2 * LANES), :], past + 2 * j * LANES, n_cache + 2 * j)
            return 0
        lax.fori_loop(0, n_new // 2, new_scores, 0)

        @pl.when(n_new % 2 == 1)
        def _():
            score_blocks(nki_ref[_block_rows(n_new - 1, LANES), :], past + (n_new - 1) * LANES, n_blk - 1)
    else:
        score_blocks(nki_ref[_block_rows(0, LANES), :], past, n_cache)

    def count(cand, strict=False):
        cand_scr[...] = jnp.broadcast_to(cand, (tq, LANES))
        strips = [slice(r, min(r + COUNT_STRIP, tq)) for r in range(0, tq, COUNT_STRIP)]

        def part(first, nb, cs):
            for d in range(nb):
                out = []
                for c, rows in zip(cs, strips):
                    k = keys_ref[first + d, rows, :]
                    hit = (k > cand_scr[rows, :]) if strict else (k >= cand_scr[rows, :])
                    out.append(c + jnp.where(hit, 1.0, 0.0))
                cs = tuple(out)
            return cs

        cs = tuple(jnp.zeros((r.stop - r.start, LANES), F32) for r in strips)
        cs = lax.fori_loop(0, n_blk // COUNT_BLOCKS, lambda t, cs: part(COUNT_BLOCKS * t, COUNT_BLOCKS, cs), cs)
        done = n_blk // COUNT_BLOCKS * COUNT_BLOCKS
        nb = COUNT_BLOCKS // 2
        while nb:
            if nb == 1 and not isinstance(n_blk, int) and (tq // LANES) % 2 == 0 and n_cache % 2 == 0:
                break
            if isinstance(n_blk, int):
                if (n_blk - done) >= nb:
                    cs, done = part(done, nb, cs), done + nb
            else:
                take = (n_blk - done) >= nb
                cs = lax.cond(take, lambda cs, done=done, nb=nb: part(done, nb, cs), lambda cs: cs, cs)
                done = done + jnp.where(take, nb, 0)
            nb //= 2
        return jnp.sum(jnp.concatenate(cs, axis=0), axis=-1, keepdims=True)

    def search(b, ans):
        cand = ans ^ lax.shift_left(jnp.int32(1), jnp.int32(31) - jnp.asarray(b, jnp.int32))
        return jnp.where(count(cand) >= top_k, cand, ans)

    kth = lax.fori_loop(0, 32, search, jnp.full((tq, 1), INT_MIN, jnp.int32))
    kth = jnp.maximum(kth, INT_MIN + 1)
    kth_b = jnp.broadcast_to(kth, (tq, LANES))
    n_ge = count(kth)
    ties_overflow = jnp.max(n_ge) > top_k

    @pl.when(jnp.logical_not(ties_overflow))
    def _():
        def plain_mask(j, _):
            madd_ref[j] = jnp.where(keys_ref[j] >= kth_b, 0.0, NEG)
            return 0
        lax.fori_loop(0, n_blk, plain_mask, 0)

    @pl.when(ties_overflow)
    def _():
        n_gt = count(kth, strict=True)
        quota = top_k - n_gt
        tri = (lax.broadcasted_iota(jnp.int32, (LANES, LANES), 0)
               <= lax.broadcasted_iota(jnp.int32, (LANES, LANES), 1)).astype(BF16)

        def tie_mask(j, seen):
            k = keys_ref[j]
            eq = jnp.where(k == kth_b, 1.0, 0.0)
            rank = _dot(eq.astype(BF16), tri) + seen
            sel = jnp.where(k > kth_b, 1.0, jnp.where(rank <= quota, eq, 0.0))
            madd_ref[j] = (1.0 - sel) * NEG
            return seen + jnp.sum(eq, axis=-1, keepdims=True)
        lax.fori_loop(0, n_blk, tie_mask, jnp.zeros((tq, 1), F32))

    qm = _pair_queries(q_ref[...])
    _softmax_init(m_scr, acc_scr)

    def step(k_ref, v_ref, blocks, slot0, kblk0, cached=False):
        tiles = []
        for blk in blocks:
            tile = jnp.clip(kblk0 + blk - qpos0 // LANES + BIAS_FAR, 0, bias_ref.shape[0] - 1)
            if cached:
                cols = _block_cols(blk, LANES)
                qk, v = _bmm(qh, k_ref[:, :, cols].astype(BF16)), _cache_values(v_ref[:, :, cols])
            else:
                rows = _block_rows(blk, LANES)
                qk, v = _bmm_nt(qm, _pair_keys(k_ref[rows, :])), _pair_values(v_ref[rows, :])
            tiles.append((qk + bias_ref[tile] + madd_ref[slot0 + blk][None], v, cached))
        _softmax_step(tiles, m_scr, acc_scr)

    if n_cache:
        qh = _head_queries(q_ref[...])

        group = _cache_group(tq, n_cache)

        def cache_step(j, _):
            step(cka_ref, cva_ref, [group * j + d for d in range(group)], 0, 0, cached=True)
            return 0
        lax.fori_loop(0, n_cache // group, cache_step, 0)

    if dyn_new:
        @pl.when(n_new % 2 == 1)
        def _():
            madd_ref[n_blk] = jnp.full((tq, LANES), NEG, F32)

        def new_step(j, _):
            step(nka_ref, nva_ref, (2 * j, 2 * j + 1), n_cache, past // LANES)
            return 0
        lax.fori_loop(0, (n_new + 1) // 2, new_step, 0)
    else:
        step(nka_ref, nva_ref, (0,), n_cache, past // LANES)
    o_ref[...] = _softmax_out(acc_scr).astype(BF16)


def _dsa(q, qi, misc, kcat, ka16, va16, bias, cache, *, batch, t_q, tq):
    t_kv = ka16.shape[0] // batch
    nq = t_q // tq
    past = 0 if cache is None else cache[0].shape[1]
    n_cache = past // LANES
    n_keys = past + t_q
    top_k = min(IDX_TOPK_MAX, n_keys // 4)
    assert tq % LANES == 0 or nq == 1
    assert n_cache % 2 == 0 and (cache is not None or (t_kv // LANES) % 2 == 0)
    qrow = lambda w: pl.BlockSpec((tq, w), lambda b, i: (b * (t_kv // tq) + i, 0))
    krow = lambda w: pl.BlockSpec((t_kv, w), lambda b, i: (b, 0))
    crow = lambda w: pl.BlockSpec((None, past, w), lambda b, i: (b, 0, 0))
    ins = [q, qi, misc]
    in_specs = [qrow(D_HEADS), qrow(D_HEADS), qrow(LANES)]
    if cache is not None:
        ins += list(cache)
        in_specs += [crow(IDX_DIM), _cache_spec(past), _cache_spec(past)]
    ins += [kcat, ka16, va16, bias]
    in_specs += [krow(4 * IDX_DIM), krow(D_HEADS), krow(D_HEADS),
                 pl.BlockSpec(bias.shape, lambda b, i: (0, 0, 0, 0))]
    n_blk_max = n_cache + t_kv // LANES
    return pl.pallas_call(
        functools.partial(_dsa_kernel, tq=tq, n_cache=n_cache, past=past, t_new=t_q, top_k=top_k,
                          dyn_new=cache is None),
        out_shape=jax.ShapeDtypeStruct((batch * t_q, D_HEADS), BF16),
        grid=(batch, nq),
        in_specs=in_specs,
        out_specs=pl.BlockSpec((tq, D_HEADS), lambda b, i: (b * nq + i, 0)),
        scratch_shapes=[pltpu.VMEM((n_blk_max, tq, LANES), jnp.int32),
                        pltpu.VMEM((n_blk_max, tq, LANES), F32),
                        pltpu.VMEM((N_HEADS * tq, 4 * IDX_DIM), BF16),
                        pltpu.VMEM((N_HEADS, tq, LANES), F32),
                        pltpu.VMEM((tq, LANES), jnp.int32)] + _softmax_scratch(tq),
        compiler_params=_params("arbitrary", "arbitrary"),
        name="dsa_attention",
    )(*ins)


FOX_GROUP = 4
DSA_TQ = 2 * LANES
COUNT_BLOCKS = 4
COUNT_STRIP = 64
FOX_TQ = 4 * LANES


def _fox_kernel(*refs, tq, tk, tkc, n_cache, past, dyn_new):
    if n_cache:
        q_ref, fq_ref, ck_ref, cv_ref, cf_ref, nk_ref, nv_ref, nf_ref, o_ref, m_scr, acc_scr = refs
    else:
        q_ref, fq_ref, nk_ref, nv_ref, nf_ref, o_ref, m_scr, acc_scr = refs
        ck_ref = cv_ref = cf_ref = None
    i = pl.program_id(1)
    group = tq // tk if dyn_new else 1
    qpos = past + i * tq + lax.broadcasted_iota(jnp.int32, (1, tq, 1), 1)
    col = lax.broadcasted_iota(jnp.int32, (1, 1, LANES), 2)

    qm = _pair_queries(q_ref[...])
    fq = jnp.stack([fq_ref[:, h:h + 1] for h in range(N_HEADS)])
    _softmax_init(m_scr, acc_scr)

    def step(k_ref, v_ref, f_ref, blocks, size, kpos_base=None, cached=False):
        tiles = []
        for blk in blocks:
            if cached:
                cols = _block_cols(blk, size)
                qk, v = _bmm(qh, k_ref[:, :, cols].astype(BF16)), _cache_values(v_ref[:, :, cols])
            else:
                rows = _block_rows(blk, size)
                qk, v = _bmm_nt(qm, _pair_keys(k_ref[rows, :])), _pair_values(v_ref[rows, :])
            s = qk + fq - f_ref[blk]
            if kpos_base is not None:
                s = jnp.where(kpos_base + blk * size + col <= qpos, s, NEG)
            tiles.append((s, v, cached))
        _softmax_step(tiles, m_scr, acc_scr)

    if n_cache:
        qh = _head_queries(q_ref[...])

        group_c = _cache_group(tq, n_cache)

        def cache_step(j, _):
            step(ck_ref, cv_ref, cf_ref, [group_c * j + d for d in range(group_c)], tkc, cached=True)
            return 0
        lax.fori_loop(0, n_cache // group_c, cache_step, 0)

    if dyn_new:
        def new_step(j, _):
            step(nk_ref, nv_ref, nf_ref, [group * j + d for d in range(group)], tk)
            return 0
        lax.fori_loop(0, i, new_step, 0)
        step(nk_ref, nv_ref, nf_ref, [group * i + d for d in range(group)], tk, past)
    else:
        step(nk_ref, nv_ref, nf_ref, [0], tk, past)
    o_ref[...] = _softmax_out(acc_scr).astype(BF16)


def _key_rows(f, tk):
    b, t, h = f.shape
    return f.reshape(b, t // tk, tk, h).transpose(0, 1, 3, 2)[:, :, :, None, :]


def _fox(q, fq, kb16, vb16, f_new, cache, *, batch, t_q, tq, tk, tkc):
    t_kv = kb16.shape[0] // batch
    nq = t_q // tq
    past = 0 if cache is None else cache[2].shape[1]
    n_cache = past // tkc
    assert tk == LANES and tkc == LANES
    qrow = lambda w: pl.BlockSpec((tq, w), lambda b, i: (b * (t_kv // tq) + i, 0))
    krow = lambda w: pl.BlockSpec((t_kv, w), lambda b, i: (b, 0))
    crow = lambda w: pl.BlockSpec((None, past, w), lambda b, i: (b, 0, 0))
    fblk = lambda nb, w: pl.BlockSpec((None, nb, N_HEADS, 1, w), lambda b, i: (b, 0, 0, 0, 0))
    ins = [q, fq]
    in_specs = [qrow(D_HEADS), qrow(N_HEADS)]
    if cache is not None:
        ins += [cache[0], cache[1], _key_rows(cache[2], tkc)]
        in_specs += [_cache_spec(past), _cache_spec(past), fblk(n_cache, tkc)]
    ins += [kb16, vb16, _key_rows(f_new, tk)]
    in_specs += [krow(D_HEADS), krow(D_HEADS), fblk(t_kv // tk, tk)]
    return pl.pallas_call(
        functools.partial(_fox_kernel, tq=tq, tk=tk, tkc=tkc, n_cache=n_cache, past=past, dyn_new=cache is None),
        out_shape=jax.ShapeDtypeStruct((batch * t_q, D_HEADS), BF16),
        grid=(batch, nq),
        in_specs=in_specs,
        out_specs=pl.BlockSpec((tq, D_HEADS), lambda b, i: (b * nq + i, 0)),
        scratch_shapes=_softmax_scratch(tq),
        compiler_params=_params("arbitrary", "arbitrary"),
        name="fox_attention",
    )(*ins)


def _out_ab_kernel(oa_ref, ob_ref, w_ref, x_ref, gt_ref, gpost_ref, o_ref):
    y = _dot(oa_ref[...], w_ref[:D_HEADS, :]) + _dot(ob_ref[...], w_ref[D_HEADS:, :])
    o_ref[...] = _residual(x_ref[...], y, gpost_ref[...], gt_ref[...], 1.0)


def _out_ab(oa, ob, w, x, gt, gpost, *, batch, t_q, tm):
    t_kv = oa.shape[0] // batch
    nt = t_q // tm
    arow = pl.BlockSpec((tm, D_HEADS), lambda b, i: (b * (t_kv // tm) + i, 0))
    xrow = pl.BlockSpec((tm, D_MODEL), lambda b, i: (b * nt + i, 0))
    return pl.pallas_call(
        _out_ab_kernel,
        out_shape=jax.ShapeDtypeStruct(x.shape, F32),
        grid=(batch, nt),
        in_specs=[arow, arow, pl.BlockSpec(w.shape, lambda b, i: (0, 0)), xrow,
                  pl.BlockSpec((tm // GROUP, 1, D_MODEL), lambda b, i: (b * nt + i, 0, 0)),
                  pl.BlockSpec((1, D_MODEL), lambda b, i: (0, 0))],
        out_specs=xrow,
        compiler_params=_params("arbitrary", "arbitrary"),
        name="out_ab",
    )(oa, ob, w, x, gt, gpost)


GDN_MISC0 = C_QKV + N_HEADS * DK_C


def _proj_gdn_kernel(x_ref, sh_ref, sc_ref, gpre_ref, w_ref, cw_ref, prev_ref, alog_ref, dtb_ref,
                     q_ref, k_ref, v_ref, gz_ref, misc_ref, tail_ref, zbuf):
    tm = x_ref.shape[0]
    halo = CONV_W - 1
    base = 8

    @pl.when(pl.program_id(1) == 0)
    def _():
        zbuf[base - halo:base, :] = prev_ref[...]

    h = _modulated(x_ref[...], gpre_ref[...], sh_ref[...], sc_ref[...]).astype(BF16)
    zbuf[base:base + tm, :] = _dot(h, w_ref[:, :C_QKV])
    gz_ref[...] = _dot(h, w_ref[:, C_QKV:GDN_MISC0])
    zm = _dot(h, w_ref[:, GDN_MISC0:])
    lane = lax.broadcasted_iota(jnp.int32, zm.shape, 1)
    beta = jax.nn.sigmoid(zm)
    g = -jnp.exp(alog_ref[...]) * jax.nn.softplus(zm + dtb_ref[...])
    misc_ref[...] = jnp.where(lane < N_HEADS, beta, g)

    for part, ref in enumerate((q_ref, k_ref, v_ref)):
        for hd in range(N_HEADS):
            c0 = part * N_HEADS * DK_C + hd * DK_C
            cols = slice(c0, c0 + DK_C)
            conv = zbuf[base:base + tm, cols] * cw_ref[halo:halo + 1, cols]
            for j in range(halo):
                conv = conv + zbuf[base - halo + j:base - halo + j + tm, cols] * cw_ref[j:j + 1, cols]
            a = _silu(conv)
            if part < 2:
                a = a * lax.rsqrt(jnp.sum(a * a, axis=-1, keepdims=True) + EPS)
                if part == 0:
                    a = a * DK_C ** -0.5
            ref[:, hd * DK_C:(hd + 1) * DK_C] = a
    tail = zbuf[base + tm - halo:base + tm, :]
    tail_ref[...] = tail
    zbuf[base - halo:base, :] = tail


def _proj_gdn(x, sh, sc, gpre, w, conv_w, conv_prev, alog_row, dtb_row, *, batch, tm):
    n = x.shape[0]
    t = n // batch
    nt = t // tm
    row = lambda wd: pl.BlockSpec((tm, wd), lambda b, i: (b * nt + i, 0))
    full = lambda a: pl.BlockSpec(a.shape, lambda b, i: (0,) * a.ndim)
    sds = lambda wd: jax.ShapeDtypeStruct((n, wd), F32)
    hd = N_HEADS * DK_C
    tail = pl.BlockSpec((None, CONV_W - 1, C_QKV), lambda b, i: (b, 0, 0))
    return pl.pallas_call(
        _proj_gdn_kernel,
        out_shape=[sds(hd), sds(hd), sds(hd), sds(hd), sds(LANES),
                   jax.ShapeDtypeStruct((batch, CONV_W - 1, C_QKV), F32)],
        grid=(batch, nt),
        in_specs=[row(D_MODEL),
                  pl.BlockSpec((tm // GROUP, 1, D_MODEL), lambda b, i: (b * nt + i, 0, 0)),
                  pl.BlockSpec((tm // GROUP, 1, D_MODEL), lambda b, i: (b * nt + i, 0, 0)),
                  pl.BlockSpec((1, D_MODEL), lambda b, i: (0, 0)),
                  full(w), full(conv_w), tail, full(alog_row), full(dtb_row)],
        out_specs=[row(hd), row(hd), row(hd), row(hd), row(LANES), tail],
        scratch_shapes=[pltpu.VMEM((8 + tm, C_QKV), F32)],
        compiler_params=_params("arbitrary", "arbitrary"),
        name="proj_gdn",
    )(x, sh, sc, gpre, w, conv_w, conv_prev, alog_row, dtb_row)


def _chunk_cumsum_kernel(g_ref, o_ref, *, chunk):
    x = g_ref[...]
    lane = lax.broadcasted_iota(jnp.int32, x.shape, 1) % chunk
    s = 1
    while s < chunk:
        x = x + jnp.where(lane >= s, pltpu.roll(x, s, 1), 0.0)
        s *= 2
    o_ref[...] = x


def _chunk_cumsum(g_rows, chunk):
    assert LANES % chunk == 0
    tiles = g_rows.reshape(-1, LANES)
    return pl.pallas_call(
        functools.partial(_chunk_cumsum_kernel, chunk=chunk),
        out_shape=jax.ShapeDtypeStruct(tiles.shape, F32),
        name="gdn_chunk_cumsum",
    )(tiles).reshape(g_rows.shape)


GDN_CHUNKS = 4


def _gdn_kernel(q_ref, k_ref, v_ref, col_ref, row_ref, s0_ref, o_ref, s_ref):
    n_chunks, _, _, c = row_ref.shape

    @pl.when(pl.program_id(1) == 0)
    def _():
        s_ref[...] = s0_ref[...]

    def stacked(ref, col0, width):
        return jnp.stack([ref[j * c:(j + 1) * c, col0 + h * width:col0 + (h + 1) * width]
                          for j in range(n_chunks) for h in range(N_HEADS)])
    q, k, v = stacked(q_ref, 0, DK_C), stacked(k_ref, 0, DK_C), stacked(v_ref, 0, DK_C)
    beta = stacked(col_ref, 0, 1)
    gc = stacked(col_ref, N_HEADS, 1)
    gc_row = jnp.concatenate([row_ref[j] for j in range(n_chunks)], axis=0)
    gc_last = gc[:, c - 1:c, :]
    ri = lax.broadcasted_iota(jnp.int32, (1, c, c), 1)
    ci = lax.broadcasted_iota(jnp.int32, (1, c, c), 2)
    causal = ri >= ci
    strict = ri > ci
    eye = jnp.where(ri == ci, 1.0, 0.0)
    decay = jnp.exp(jnp.where(causal, gc - gc_row, NEG))
    k16 = k.astype(BF16)
    a = jnp.where(strict, beta * _bmm_nt(k16, k16) * decay, 0.0)
    inv = eye - a
    p = a
    n = 2
    while n < c:
        p = _dot3(p, p, _bmm)
        inv = inv + _dot3(inv, p, _bmm)
        n *= 2
    sol = _dot3(inv, jnp.concatenate([v * beta, k * (beta * jnp.exp(gc))], axis=-1), _bmm)
    u, w = sol[:, :, :DK_C], sol[:, :, DK_C:]
    qk = jnp.where(causal, _bmm_nt(q.astype(BF16), k16) * decay, 0.0)
    q_dec = (q * jnp.exp(gc)).astype(BF16)
    k_dec = k * jnp.exp(gc_last - gc)
    k_dec_t = jnp.stack([k_dec[i].T for i in range(k_dec.shape[0])]).astype(BF16)
    w16, qk16, g_last = w.astype(BF16), qk.astype(BF16), jnp.exp(gc_last)
    s = s_ref[...]
    for j in range(n_chunks):
        grp = slice(j * N_HEADS, (j + 1) * N_HEADS)
        s16 = s.astype(BF16)
        v16 = (u[grp] - _bmm(w16[grp], s16)).astype(BF16)
        o = _bmm(q_dec[grp], s16) + _bmm(qk16[grp], v16)
        for h in range(N_HEADS):
            o_ref[j * c:(j + 1) * c, h * DK_C:(h + 1) * DK_C] = o[h]
        s = s * g_last[grp] + _bmm(k_dec_t[grp], v16)
    s_ref[...] = s


def _gdn(q, k, v, col, row, s0, *, batch, chunk):
    n = q.shape[0]
    per_step = GDN_CHUNKS if (n // batch // chunk) % GDN_CHUNKS == 0 else 1
    nc = n // batch // chunk // per_step
    hd = N_HEADS * DK_C
    blk = lambda w: pl.BlockSpec((per_step * chunk, w), lambda b, i: (b * nc + i, 0))
    state = pl.BlockSpec((None, N_HEADS, DK_C, DK_C), lambda b, i: (b, 0, 0, 0))
    return pl.pallas_call(
        _gdn_kernel,
        out_shape=[jax.ShapeDtypeStruct((n, hd), F32), jax.ShapeDtypeStruct(s0.shape, F32)],
        grid=(batch, nc),
        in_specs=[blk(hd), blk(hd), blk(hd), blk(2 * N_HEADS),
                  pl.BlockSpec((per_step, N_HEADS, 1, chunk), lambda b, i: (b * nc + i, 0, 0, 0)), state],
        out_specs=[blk(hd), state],
        compiler_params=_params("arbitrary", "arbitrary"),
        name="gated_delta_rule",
    )(q, k, v, col, row, s0)


def _out_gdn_kernel(o_ref, gz_ref, nw_ref, w_ref, x_ref, gt_ref, gpost_ref, out_ref):
    parts = []
    for h in range(N_HEADS):
        cols = slice(h * DK_C, (h + 1) * DK_C)
        o = o_ref[:, cols]
        o = o * lax.rsqrt(jnp.mean(o * o, axis=-1, keepdims=True) + EPS) * nw_ref[...]
        parts.append((o * _silu(gz_ref[:, cols])).astype(BF16))
    y = _dot(jnp.concatenate(parts, axis=-1), w_ref[...])
    out_ref[...] = _residual(x_ref[...], y, gpost_ref[...], gt_ref[...], 1.0)


def _out_gdn(o, gz, norm_w, w, x, gt, gpost, tm=512):
    n = x.shape[0]
    tm = min(tm, n)
    row = pl.BlockSpec((tm, D_MODEL), lambda i: (i, 0))
    return pl.pallas_call(
        _out_gdn_kernel,
        out_shape=jax.ShapeDtypeStruct(x.shape, F32),
        grid=(n // tm,),
        in_specs=[row, row, pl.BlockSpec((1, DK_C), lambda i: (0, 0)), pl.BlockSpec(w.shape, lambda i: (0, 0)),
                  row, pl.BlockSpec((tm // GROUP, 1, D_MODEL), lambda i: (i, 0, 0)),
                  pl.BlockSpec((1, D_MODEL), lambda i: (0, 0))],
        out_specs=row,
        compiler_params=_params("arbitrary"),
        name="out_gdn",
    )(o, gz, norm_w, w, x, gt, gpost)


def _mixer_ab(x, mod, w, past, *, batch, t, bias):
    sh, sc, gt = mod
    (qa, qb, ka, va, kb, vb, ka16, va16, kb16, vb16, qi, misc, kcat) = _proj_ab(
        x, sh, sc, w["pre"], w["ab_main"], w["ab_idx_hi"], w["ab_idx_lo"], batch=batch)
    ki = misc[:, :IDX_DIM]
    fl = misc[:, IDX_DIM + N_HEADS:IDX_DIM + 2 * N_HEADS].reshape(batch, t, N_HEADS)
    t_kv = -(-t // LANES) * LANES
    pad_rows = lambda a: a if t_kv == t else jnp.pad(
        a.reshape(batch, t, -1), ((0, 0), (0, t_kv - t), (0, 0))).reshape(batch * t_kv, -1)
    fl_p = fl if t_kv == t else jnp.pad(fl, ((0, 0), (0, t_kv - t), (0, 0)))
    if past is None:
        logf_b, f_new = _logf(fl_p, w["ab_b_f"], None)
        cache_a = cache_b = None
    else:
        cak, cav, caik, cbk, cbv, cblf = past
        p = cak.shape[1]
        logf_b, f_new, f_cache = _logf(fl_p, w["ab_b_f"], cblf)
        cache_a = (caik, _feature_major(cak), _feature_major(cav))
        cache_b = (_feature_major(cbk), _feature_major(cbv), _unblocked_rows(f_cache))
    logf = _unblocked_rows(logf_b)[:, :t]
    f_new = _unblocked_rows(f_new)
    o_a = _dsa(pad_rows(qa), pad_rows(qi), pad_rows(misc), pad_rows(kcat), pad_rows(ka16), pad_rows(va16), bias, cache_a,
               batch=batch, t_q=t, tq=bias.shape[2])
    o_b = _fox(pad_rows(qb), f_new.reshape(batch * t_kv, N_HEADS), pad_rows(kb16), pad_rows(vb16), f_new, cache_b,
               batch=batch, t_q=t, tq=min(FOX_TQ, t), tk=LANES, tkc=LANES)
    x = _out_ab(o_a, o_b, w["ab_w_out"], x, gt, w["post"], batch=batch, t_q=t, tm=min(512, t))
    return x, (ka, va, ki.reshape(batch, t, IDX_DIM), kb, vb, logf)


def _mixer_gdn(x, mod, w, past, *, batch, t):
    sh, sc, gt = mod
    if past is None:
        conv_prev = jnp.zeros((batch, CONV_W - 1, C_QKV), F32)
        s0 = jnp.zeros((batch, N_HEADS, DK_C, DK_C), F32)
    else:
        conv_prev, s0 = past
    q, k, v, gz, misc, new_conv = _proj_gdn(x, sh, sc, w["pre"], w["gdn_w_in"], w["gdn_conv_w"], conv_prev,
                                            w["gdn_alog_row"], w["gdn_dtb_row"], batch=batch, tm=min(512, t))
    chunk = min(t, CHUNK)
    n = batch * t
    gc_rows = _chunk_cumsum(misc[:, N_HEADS:2 * N_HEADS].T, chunk)
    col = jnp.concatenate([misc[:, :N_HEADS], gc_rows.T], axis=-1)
    row = gc_rows.reshape(N_HEADS, n // chunk, chunk).transpose(1, 0, 2)[:, :, None, :]
    o, s_new = _gdn(q, k, v, col, row, s0, batch=batch, chunk=chunk)
    x = _out_gdn(o, gz, w["gdn_norm_w"], w["gdn_w_out"], x, gt, w["post"])
    return x, (new_conv, s_new)


def _forward(x, mod_all, w, past, *, batch, t):
    new_state = []
    groups = lambda v: jnp.repeat(v, t // GROUP, axis=0)[:, None, :]
    for l in range(mod_all.shape[0]):
        m = mod_all[l].reshape(batch, 3, 3, D_MODEL)
        mod = [[groups(m[:, s, j]) for j in range(3)] for s in range(3)]
        pre = lambda s: w["norm_pre"][l, s][None, :]
        post = lambda s: w["norm_post"][l, s][None, :]
        x = _ffn(x, mod[0], pre(0), post(0), w["ffn_w_gate"][l, 0], w["ffn_w_up"][l, 0], w["ffn_w_down"][l, 0])
        wl = dict(w, pre=pre(1), post=post(1))
        if l % 2 == 0:
            x, st = _mixer_ab(x, mod[1], wl, None if past is None else past[:6], batch=batch, t=t,
                              bias=w["bias_tiles"][min(DSA_TQ, t)])
        else:
            x, st = _mixer_gdn(x, mod[1], wl, None if past is None else past[6:], batch=batch, t=t)
        new_state.extend(st)
        x = _ffn(x, mod[2], pre(2), post(2), w["ffn_w_gate"][l, 1], w["ffn_w_up"][l, 1], w["ffn_w_down"][l, 1])
    return x, new_state


def _prepare_weights(rel_bias, ab_w_in, ab_b_f, ab_w_out, gdn_w_in, gdn_conv_w, gdn_A_log, gdn_dt_bias,
                     gdn_norm_w, gdn_w_out, norm_pre, norm_post, ffn_w_gate, ffn_w_up, ffn_w_down, tqs):
    o = [0]
    for s in (D_HEADS, D_HEADS, D_HEADS, D_HEADS, IDX_DIM, N_HEADS, D_HEADS, D_HEADS, D_HEADS, N_HEADS):
        o.append(o[-1] + s)
    cols = lambda i: ab_w_in[:, o[i]:o[i + 1]]
    qa, ka, va, qi, ki, wi, qb, kb, vb, fl = (cols(i) for i in range(10))
    idx = jnp.concatenate([qi, ki, wi, fl, jnp.zeros((D_MODEL, IDX_COLS - D_HEADS - IDX_DIM - 2 * N_HEADS), F32)], 1)
    idx_hi = idx.astype(BF16)
    gdn_in = jnp.concatenate([gdn_w_in, jnp.zeros((D_MODEL, LANES - 2 * N_HEADS), F32)], 1)
    lane_row = lambda v: jnp.concatenate([jnp.zeros((N_HEADS,), F32), v.astype(F32),
                                          jnp.zeros((LANES - 2 * N_HEADS,), F32)])[None, :]
    return dict(
        ab_main=jnp.concatenate([qa, qb, ka, va, kb, vb], 1).astype(BF16),
        ab_idx_hi=idx_hi, ab_idx_lo=(idx - idx_hi.astype(F32)).astype(BF16),
        ab_b_f=ab_b_f, ab_w_out=ab_w_out.astype(BF16),
        gdn_w_in=gdn_in.astype(BF16), gdn_conv_w=gdn_conv_w,
        gdn_alog_row=lane_row(gdn_A_log), gdn_dtb_row=lane_row(gdn_dt_bias),
        gdn_norm_w=gdn_norm_w[None, :], gdn_w_out=gdn_w_out.astype(BF16),
        norm_pre=norm_pre, norm_post=norm_post,
        ffn_w_gate=ffn_w_gate.astype(BF16), ffn_w_up=ffn_w_up.astype(BF16), ffn_w_down=ffn_w_down.astype(BF16),
        bias_tiles={tq: _bias_tiles(rel_bias, tq) for tq in tqs},
    )


def kernel(x_prompt, x_sample, cache_a_k, cache_a_v, cache_a_idx_k, cache_b_k, cache_b_v, cache_b_logf,
           state_c_conv, state_c_rec, c_prompt, c_sample, rel_bias, ab_w_in, ab_b_f, ab_w_out,
           gdn_w_in, gdn_conv_w, gdn_A_log, gdn_dt_bias, gdn_norm_w, gdn_w_out, norm_pre, norm_post,
           ada_w, ada_b, ffn_w_gate, ffn_w_up, ffn_w_down):
    bp, tp, _ = x_prompt.shape
    bs, ts, _ = x_sample.shape
    w = _prepare_weights(rel_bias, ab_w_in, ab_b_f, ab_w_out, gdn_w_in, gdn_conv_w, gdn_A_log, gdn_dt_bias,
                         gdn_norm_w, gdn_w_out, norm_pre, norm_post, ffn_w_gate, ffn_w_up, ffn_w_down,
                         tqs={min(DSA_TQ, tp), min(DSA_TQ, ts)})
    mod = _ada(jnp.concatenate([c_prompt, c_sample], 0), ada_w, ada_b)
    y_p, st_p = _forward(x_prompt.reshape(bp * tp, D_MODEL), mod[:, :bp], w, None, batch=bp, t=tp)
    past = (cache_a_k, cache_a_v, cache_a_idx_k, cache_b_k, cache_b_v, cache_b_logf, state_c_conv, state_c_rec)
    y_s, st_s = _forward(x_sample.reshape(bs * ts, D_MODEL), mod[:, bp:], w, past, batch=bs, t=ts)
    return (y_p.reshape(bp, tp, D_MODEL), y_s.reshape(bs, ts, D_MODEL), *st_p, *st_s)
```

```python
import functools
import math

import jax
import jax.numpy as jnp
from jax import lax
from jax.experimental import pallas as pl
from jax.experimental.pallas import tpu as pltpu

D_MODEL = 1024
D_FF = 4 * D_MODEL
CHUNK = 64
HEAD_DIM = 64
N_HEADS = 8
D_HEADS = N_HEADS * HEAD_DIM
IDX_DIM = 64
IDX_TOPK_MAX = 256
DK_C = 128
CONV_W = 4
C_QKV = 3 * N_HEADS * DK_C
REL_BUCKETS = 32
REL_MAX_DIST = 256
FFN_RES = 0.5
EPS = 1e-6

GROUP = 32
LANES = 128
NEG = -1e30
INT_MIN = -(2 ** 31)
VMEM_LIMIT = 56 * 1024 * 1024

F32 = jnp.float32
BF16 = jnp.bfloat16


def _params(*sem):
    return pltpu.CompilerParams(dimension_semantics=sem, vmem_limit_bytes=VMEM_LIMIT)


def _dot(a, b):
    return jnp.dot(a, b, preferred_element_type=F32)


def _dot_nt(a, b):
    return lax.dot_general(a, b, (((1,), (1,)), ((), ())), preferred_element_type=F32)


def _bmm(a, b):
    return lax.dot_general(a, b, (((2,), (1,)), ((0,), (0,))), preferred_element_type=F32)


def _bmm_nt(a, b):
    return lax.dot_general(a, b, (((2,), (2,)), ((0,), (0,))), preferred_element_type=F32)


def _split(x):
    hi = x.astype(BF16)
    lo = (x - hi.astype(F32)).astype(BF16)
    return hi, lo


def _dot3(a, b, mm=_dot):
    ah, al = _split(a)
    bh, bl = _split(b)
    return mm(ah, bh) + mm(al, bh) + mm(ah, bl)


def _silu(x):
    return x * jax.nn.sigmoid(x)


def _modulated(x, gpre, sh, sc):
    r = lax.rsqrt(jnp.mean(x * x, axis=-1, keepdims=True) + EPS)
    y = (x * r * gpre).reshape(sh.shape[0], GROUP, x.shape[-1])
    return (y * (1.0 + sc) + sh).reshape(x.shape)


def _residual(x, y, gpost, gate, weight):
    r = lax.rsqrt(jnp.mean(y * y, axis=-1, keepdims=True) + EPS)
    yn = (y * r * gpost).reshape(gate.shape[0], GROUP, x.shape[-1])
    return x + (weight * gate * yn).reshape(x.shape)


def _ada_kernel(c_ref, w_ref, b_ref, o_ref):
    o_ref[...] = _dot3(_silu(c_ref[...]), w_ref[...]) + b_ref[...]


def _ada(c_all, ada_w, ada_b):
    depth, _, n_out = ada_w.shape
    nb = c_all.shape[0]
    tn = 1024
    return pl.pallas_call(
        _ada_kernel,
        out_shape=jax.ShapeDtypeStruct((depth, nb, n_out), F32),
        grid=(depth, n_out // tn),
        in_specs=[pl.BlockSpec((nb, D_MODEL), lambda l, j: (0, 0)),
                  pl.BlockSpec((None, D_MODEL, tn), lambda l, j: (l, 0, j)),
                  pl.BlockSpec((None, 1, tn), lambda l, j: (l, 0, j))],
        out_specs=pl.BlockSpec((None, nb, tn), lambda l, j: (l, 0, j)),
        compiler_params=_params("arbitrary", "arbitrary"),
        name="ada_mod",
    )(c_all, ada_w, ada_b.reshape(depth, 1, n_out))


FFN_SPLIT = 8


def _ffn_kernel(x_ref, sh_ref, sc_ref, gt_ref, gpre_ref, gpost_ref, wg_ref, wu_ref, wd_ref, o_ref,
                h_scr, acc_scr):
    j = pl.program_id(1)

    @pl.when(j == 0)
    def _():
        h_scr[...] = _modulated(x_ref[...], gpre_ref[...], sh_ref[...], sc_ref[...]).astype(BF16)
        acc_scr[...] = jnp.zeros_like(acc_scr)

    h = h_scr[...]
    tf = wg_ref.shape[1]
    part = tf // FFN_SPLIT
    y = None
    for s in range(FFN_SPLIT):
        cols = slice(s * part, (s + 1) * part)
        a = _dot(h, wg_ref[:, cols])
        b = _dot(h, wu_ref[:, cols])
        d = _dot((_silu(a) * b).astype(BF16), wd_ref[cols, :])
        y = d if y is None else y + d
    acc_scr[...] += y

    @pl.when(j == pl.num_programs(1) - 1)
    def _():
        o_ref[...] = _residual(x_ref[...], acc_scr[...], gpost_ref[...], gt_ref[...], FFN_RES)


def _ffn(x, mod, gpre, gpost, wg, wu, wd, tm=512, tf=2048):
    n = x.shape[0]
    tm = min(tm, n)
    sh, sc, gt = mod
    row = pl.BlockSpec((tm, D_MODEL), lambda i, j: (i, 0))
    grp = pl.BlockSpec((tm // GROUP, 1, D_MODEL), lambda i, j: (i, 0, 0))
    vec = pl.BlockSpec((1, D_MODEL), lambda i, j: (0, 0))
    return pl.pallas_call(
        _ffn_kernel,
        out_shape=jax.ShapeDtypeStruct((n, D_MODEL), F32),
        grid=(n // tm, D_FF // tf),
        in_specs=[row, grp, grp, grp, vec, vec,
                  pl.BlockSpec((D_MODEL, tf), lambda i, j: (0, j)),
                  pl.BlockSpec((D_MODEL, tf), lambda i, j: (0, j)),
                  pl.BlockSpec((tf, D_MODEL), lambda i, j: (j, 0))],
        out_specs=row,
        scratch_shapes=[pltpu.VMEM((tm, D_MODEL), BF16), pltpu.VMEM((tm, D_MODEL), F32)],
        compiler_params=_params("arbitrary", "arbitrary"),
        name="ffn_sublayer",
    )(x, sh, sc, gt, gpre, gpost, wg, wu, wd)


AB_MAIN = ("qa", "qb", "ka", "va", "kb", "vb")
IDX_COLS = D_HEADS + LANES


def _proj_ab_kernel(x_ref, sh_ref, sc_ref, gpre_ref, wm_ref, wih_ref, wil_ref,
                    qa_ref, qb_ref, ka_ref, va_ref, kb_ref, vb_ref,
                    ka16_ref, va16_ref, kb16_ref, vb16_ref, qi_ref, misc_ref, kcat_ref, *, feature_major):
    h = _modulated(x_ref[...], gpre_ref[...], sh_ref[...], sc_ref[...])
    hi, lo = _split(h)
    for n, ref in enumerate((qa_ref, qb_ref)):
        ref[...] = _dot(hi, wm_ref[:, n * D_HEADS:(n + 1) * D_HEADS]).astype(BF16)
    for n, (ref, ref16) in enumerate(((ka_ref, ka16_ref), (va_ref, va16_ref),
                                      (kb_ref, kb16_ref), (vb_ref, vb16_ref))):
        z = _dot(hi, wm_ref[:, (n + 2) * D_HEADS:(n + 3) * D_HEADS])
        ref[...] = z.T.reshape(ref.shape) if feature_major else z
        ref16[...] = z.astype(BF16)
    z = _dot(hi, wih_ref[...]) + _dot(lo, wih_ref[...]) + _dot(hi, wil_ref[...])
    qi_ref[...] = z[:, :D_HEADS]
    misc_ref[...] = z[:, D_HEADS:]
    kcat_ref[...] = _split_keys(z[:, D_HEADS:D_HEADS + IDX_DIM])


def _proj_ab(x, sh, sc, gpre, wm, wih, wil, *, batch, tm=512):
    n = x.shape[0]
    t = n // batch
    tm = min(tm, n)
    feature_major = t % tm == 0 and tm % LANES == 0
    row = lambda w: pl.BlockSpec((tm, w), lambda i: (i, 0))
    full = lambda a: pl.BlockSpec(a.shape, lambda i: (0, 0))
    grp = pl.BlockSpec((tm // GROUP, 1, D_MODEL), lambda i: (i, 0, 0))
    sds = lambda w, dt: jax.ShapeDtypeStruct((n, w), dt)
    if feature_major:
        nt = t // tm
        kv_shape = jax.ShapeDtypeStruct((batch, N_HEADS, HEAD_DIM, t), F32)
        kv_spec = pl.BlockSpec((None, N_HEADS, HEAD_DIM, tm), lambda i: (i // nt, 0, 0, i % nt))
    else:
        kv_shape, kv_spec = sds(D_HEADS, F32), row(D_HEADS)
    out_shape = ([sds(D_HEADS, BF16)] * 2 + [kv_shape] * 4 + [sds(D_HEADS, BF16)] * 4
                 + [sds(D_HEADS, F32), sds(LANES, F32), sds(4 * IDX_DIM, BF16)])
    out_specs = [row(D_HEADS)] * 2 + [kv_spec] * 4 + [row(D_HEADS)] * 5 + [row(LANES), row(4 * IDX_DIM)]
    outs = pl.pallas_call(
        functools.partial(_proj_ab_kernel, feature_major=feature_major),
        out_shape=out_shape,
        grid=(n // tm,),
        in_specs=[row(D_MODEL), grp, grp, pl.BlockSpec((1, D_MODEL), lambda i: (0, 0)),
                  full(wm), full(wih), full(wil)],
        out_specs=out_specs,
        compiler_params=_params("arbitrary"),
        name="proj_ab",
    )(x, sh, sc, gpre, wm, wih, wil)
    to_cache = ((lambda a: jnp.transpose(a, (0, 3, 1, 2))) if feature_major
                else (lambda a: a.reshape(batch, t, N_HEADS, HEAD_DIM)))
    return (*outs[:2], *(to_cache(a) for a in outs[2:6]), *outs[6:])


def _lane_cumsum(x):
    lane = lax.broadcasted_iota(jnp.int32, x.shape, 1)
    s = 1
    while s < LANES:
        x = x + jnp.where(lane >= s, pltpu.roll(x, s, 1), 0.0)
        s *= 2
    return x


def _logf_kernel(*refs, n_cache, n_new):
    if n_cache:
        fl_ref, bf_ref, cache_ref, logf_ref, fnew_ref, fcache_ref = refs
    else:
        fl_ref, bf_ref, logf_ref, fnew_ref = refs
    carry = jnp.zeros((N_HEADS, 1), F32)
    for blk in range(n_cache):
        c = _lane_cumsum(cache_ref[blk]) + carry
        fcache_ref[blk] = c
        carry = c[:, LANES - 1:LANES]
    for blk in range(n_new):
        lf = jax.nn.log_sigmoid(fl_ref[blk] + bf_ref[...])
        logf_ref[blk] = lf
        c = _lane_cumsum(lf) + carry
        fnew_ref[blk] = c
        carry = c[:, LANES - 1:LANES]


def _blocked_rows(a):
    b, t, h = a.shape
    return a.reshape(b, t // LANES, LANES, h).transpose(0, 1, 3, 2)


def _unblocked_rows(a):
    b, nb, h, _ = a.shape
    return a.transpose(0, 1, 3, 2).reshape(b, nb * LANES, h)


def _logf(fl, b_f, cache_logf):
    b, t, _ = fl.shape
    n_new = t // LANES
    blk = lambda nb: pl.BlockSpec((None, nb, N_HEADS, LANES), lambda i: (i, 0, 0, 0))
    sds = lambda nb: jax.ShapeDtypeStruct((b, nb, N_HEADS, LANES), F32)
    ins = [_blocked_rows(fl), b_f.reshape(N_HEADS, 1)]
    in_specs = [blk(n_new), pl.BlockSpec((N_HEADS, 1), lambda i: (0, 0))]
    out_shape = [sds(n_new), sds(n_new)]
    out_specs = [blk(n_new), blk(n_new)]
    n_cache = 0
    if cache_logf is not None:
        n_cache = cache_logf.shape[1] // LANES
        ins.append(_blocked_rows(cache_logf))
        in_specs.append(blk(n_cache))
        out_shape.append(sds(n_cache))
        out_specs.append(blk(n_cache))
    outs = pl.pallas_call(
        functools.partial(_logf_kernel, n_cache=n_cache, n_new=n_new),
        out_shape=out_shape, grid=(b,), in_specs=in_specs, out_specs=out_specs,
        compiler_params=_params("arbitrary"), name="logf_cumsum",
    )(*ins)
    return outs


BIAS_FAR = 3


def _t5_bucket(rel):
    nb = REL_BUCKETS // 2
    max_exact = nb // 2
    base = jnp.where(rel > 0, nb, 0)
    n = jnp.abs(rel)
    nf = jnp.maximum(n, 1).astype(F32)
    large = max_exact + (jnp.log(nf / max_exact) / math.log(REL_MAX_DIST / max_exact)
                         * (nb - max_exact)).astype(jnp.int32)
    large = jnp.minimum(large, nb - 1)
    return base + jnp.where(n < max_exact, n, large)


def _bias_kernel(tab_ref, o_ref):
    tq = o_ref.shape[2]
    row = lax.broadcasted_iota(jnp.int32, (tq, LANES), 0)
    col = lax.broadcasted_iota(jnp.int32, (tq, LANES), 1)
    for t in range(o_ref.shape[0]):
        bucket = _t5_bucket(LANES * (t - BIAS_FAR) + col - row)
        for h in range(N_HEADS):
            acc = jnp.zeros((tq, LANES), F32)
            for b in range(REL_BUCKETS):
                acc = jnp.where(bucket == b, tab_ref[b, h], acc)
            o_ref[t, h] = acc


def _bias_tiles(rel_bias, tq):
    assert LANES * BIAS_FAR - (LANES - 1) >= REL_MAX_DIST
    n_tiles = BIAS_FAR + max(1, tq // LANES)
    return pl.pallas_call(
        _bias_kernel,
        out_shape=jax.ShapeDtypeStruct((n_tiles, N_HEADS, tq, LANES), F32),
        in_specs=[pl.BlockSpec(memory_space=pltpu.SMEM)],
        name="rel_bias_tiles",
    )(rel_bias)


def _sortable(x):
    b = pltpu.bitcast(x, jnp.int32)
    return b ^ ((b >> 31) & 0x7FFFFFFF)


def _split_keys(ki):
    hi = ki.astype(BF16).astype(F32)
    return jnp.concatenate([hi, ki - hi, hi, ki - hi], axis=-1).astype(BF16)


def _block_rows(blk, size):
    start = blk * size
    return pl.ds(start if isinstance(start, int) else pl.multiple_of(start, size), size)


def _pair_queries(q16):
    lane = lax.broadcasted_iota(jnp.int32, (q16.shape[0], LANES), 1)
    out = []
    for h in range(N_HEADS):
        pair = q16[:, LANES * (h // 2):LANES * (h // 2 + 1)].astype(F32) * HEAD_DIM ** -0.5
        own = (lane < HEAD_DIM) if h % 2 == 0 else (lane >= HEAD_DIM)
        out.append(jnp.where(own, pair, 0.0).astype(BF16))
    return jnp.stack(out)


def _pair_keys(x):
    x = x.astype(BF16)
    return jnp.stack([x[:, LANES * (h // 2):LANES * (h // 2 + 1)] for h in range(N_HEADS)])


def _pair_values(x):
    x = x.astype(BF16)
    lane = lax.broadcasted_iota(jnp.int32, (x.shape[0], LANES), 1)
    out = []
    for h in range(N_HEADS):
        pair = x[:, LANES * (h // 2):LANES * (h // 2 + 1)]
        own = (lane < HEAD_DIM) if h % 2 == 0 else (lane >= HEAD_DIM)
        out.append(jnp.where(own, pair, jnp.ones_like(pair)))
    return jnp.stack(out)


def _head_queries(q16):
    return jnp.stack([(q16[:, h * HEAD_DIM:(h + 1) * HEAD_DIM].astype(F32) * HEAD_DIM ** -0.5).astype(BF16)
                      for h in range(N_HEADS)])


def _cache_values(vt):
    vt = vt.astype(BF16)
    ones = jnp.ones_like(vt[0])
    return jnp.stack([jnp.concatenate([vt[h], ones] if h % 2 == 0 else [ones, vt[h]], axis=0) for h in range(N_HEADS)])


def _cache_spec(past):
    return pl.BlockSpec((None, N_HEADS, HEAD_DIM, past), lambda b, i: (b, 0, 0, 0))


def _feature_major(cache):
    return jnp.transpose(cache, (0, 2, 3, 1))


def _cache_group(tq, n_cache):
    return next(g for g in (8, 4, 2, 1) if g <= (8 if tq <= 64 else FOX_GROUP) and n_cache % g == 0)


def _block_cols(blk, size):
    return pl.ds(pl.multiple_of(blk * size, size), size)


def _unpair(o):
    lane = lax.broadcasted_iota(jnp.int32, o.shape[1:], 1)
    return jnp.concatenate([jnp.where(lane < HEAD_DIM, o[2 * p], o[2 * p + 1]) for p in range(N_HEADS // 2)], axis=-1)


def _softmax_init(m_scr, acc_scr):
    m_scr[...] = jnp.full(m_scr.shape, NEG, F32)
    acc_scr[...] = jnp.zeros(acc_scr.shape, F32)


def _softmax_step(tiles, m_scr, acc_scr):
    m_prev = m_scr[...]
    top = tiles[0][0]
    for tile in tiles[1:]:
        top = jnp.maximum(top, tile[0])
    m_new = jnp.maximum(m_prev, jnp.max(top, axis=-1, keepdims=True))
    acc = jnp.exp(m_prev - m_new) * acc_scr[...]
    for s, v, feature_major in tiles:
        acc = acc + (_bmm_nt if feature_major else _bmm)(jnp.exp(s - m_new).astype(BF16), v)
    acc_scr[...] = acc
    m_scr[...] = m_new


def _softmax_out(acc_scr):
    acc = acc_scr[...]
    den = jnp.stack([pltpu.roll(acc[h], HEAD_DIM, 1) for h in range(N_HEADS)])
    return _unpair(acc / den)


def _softmax_scratch(tq):
    return [pltpu.VMEM((N_HEADS, tq, LANES), F32), pltpu.VMEM((N_HEADS, tq, LANES), F32)]


def _dsa_kernel(*refs, tq, n_cache, past, t_new, top_k, dyn_new):
    if n_cache:
        (q_ref, qi_ref, mq_ref, cki_ref, cka_ref, cva_ref, nki_ref, nka_ref, nva_ref, bias_ref,
         o_ref, keys_ref, madd_ref, qcat_scr, wi_scr, cand_scr, m_scr, acc_scr) = refs
    else:
        (q_ref, qi_ref, mq_ref, nki_ref, nka_ref, nva_ref, bias_ref,
         o_ref, keys_ref, madd_ref, qcat_scr, wi_scr, cand_scr, m_scr, acc_scr) = refs
        cki_ref = cka_ref = cva_ref = None
    i = pl.program_id(1)
    n_new = (i + 1) * (tq // LANES) if dyn_new else 1
    n_blk = n_cache + n_new
    qpos0 = past + i * tq
    qpos = qpos0 + lax.broadcasted_iota(jnp.int32, (tq, 1), 0)
    col = lax.broadcasted_iota(jnp.int32, (tq, LANES), 1)
    n_keys = past + t_new

    qi = qi_ref[...]
    for h in range(N_HEADS):
        x = qi[:, h * IDX_DIM:(h + 1) * IDX_DIM]
        hi = x.astype(BF16).astype(F32)
        qcat_scr[h * tq:(h + 1) * tq, :] = jnp.concatenate([hi, hi, x - hi, jnp.zeros_like(x)], axis=-1).astype(BF16)
        wi_scr[h] = jnp.broadcast_to(mq_ref[:, IDX_DIM + h:IDX_DIM + h + 1], (tq, LANES))
    score_scale = IDX_DIM ** -0.5 * N_HEADS ** -0.5

    def score_blocks(kcat, kpos0, slot0):
        nb = kcat.shape[0] // LANES
        accs = [jnp.zeros((tq, LANES), F32)] * nb
        for h in range(N_HEADS):
            s = _dot_nt(qcat_scr[h * tq:(h + 1) * tq, :], kcat)
            for d in range(nb):
                accs[d] = accs[d] + jnp.maximum(s[:, d * LANES:(d + 1) * LANES], 0.0) * wi_scr[h]
        for d in range(nb):
            kpos = kpos0 + d * LANES + col
            adm = jnp.logical_and((kpos // CHUNK) <= (qpos // CHUNK), kpos < n_keys)
            keys_ref[slot0 + d] = jnp.where(adm, _sortable(accs[d] * score_scale), INT_MIN)

    if n_cache:
        def cache_scores(j, _):
            score_blocks(_split_keys(cki_ref[_block_rows(j, 2 * LANES), :]), 2 * j * LANES, 2 * j)
            return 0
        lax.fori_loop(0, n_cache // 2, cache_scores, 0)

    if dyn_new:
        def new_scores(j, _):
            score_blocks(nki_ref[_block_rows(j, 2 * LANES), :], past + 2 * j * LANES, n_cache + 2 * j)
            return 0
        lax.fori_loop(0, n_new // 2, new_scores, 0)

        @pl.when(n_new % 2 == 1)
        def _():
            score_blocks(nki_ref[_block_rows(n_new - 1, LANES), :], past + (n_new - 1) * LANES, n_blk - 1)
    else:
        score_blocks(nki_ref[_block_rows(0, LANES), :], past, n_cache)

    def count(cand, strict=False):
        cand_scr[...] = jnp.broadcast_to(cand, (tq, LANES))
        strips = [slice(r, min(r + COUNT_STRIP, tq)) for r in range(0, tq, COUNT_STRIP)]

        def part(first, nb, cs):
            for d in range(nb):
                out = []
                for c, rows in zip(cs, strips):
                    k = keys_ref[first + d, rows, :]
                    hit = (k > cand_scr[rows, :]) if strict else (k >= cand_scr[rows, :])
                    out.append(c + jnp.where(hit, 1.0, 0.0))
                cs = tuple(out)
            return cs

        cs = tuple(jnp.zeros((r.stop - r.start, LANES), F32) for r in strips)
        cs = lax.fori_loop(0, n_blk // COUNT_BLOCKS, lambda t, cs: part(COUNT_BLOCKS * t, COUNT_BLOCKS, cs), cs)
        done = n_blk // COUNT_BLOCKS * COUNT_BLOCKS
        nb = COUNT_BLOCKS // 2
        while nb:
            if nb == 1 and not isinstance(n_blk, int) and (tq // LANES) % 2 == 0 and n_cache % 2 == 0:
                break
            if isinstance(n_blk, int):
                if (n_blk - done) >= nb:
                    cs, done = part(done, nb, cs), done + nb
            else:
                take = (n_blk - done) >= nb
                cs = lax.cond(take, lambda cs, done=done, nb=nb: part(done, nb, cs), lambda cs: cs, cs)
                done = done + jnp.where(take, nb, 0)
            nb //= 2
        return jnp.sum(jnp.concatenate(cs, axis=0), axis=-1, keepdims=True)

    def search(b, ans):
        cand = ans ^ lax.shift_left(jnp.int32(1), jnp.int32(31) - jnp.asarray(b, jnp.int32))
        return jnp.where(count(cand) >= top_k, cand, ans)

    kth = lax.fori_loop(0, 32, search, jnp.full((tq, 1), INT_MIN, jnp.int32))
    kth = jnp.maximum(kth, INT_MIN + 1)
    kth_b = jnp.broadcast_to(kth, (tq, LANES))
    n_ge = count(kth)
    ties_overflow = jnp.max(n_ge) > top_k

    @pl.when(jnp.logical_not(ties_overflow))
    def _():
        def plain_mask(j, _):
            madd_ref[j] = jnp.where(keys_ref[j] >= kth_b, 0.0, NEG)
            return 0
        lax.fori_loop(0, n_blk, plain_mask, 0)

    @pl.when(ties_overflow)
    def _():
        n_gt = count(kth, strict=True)
        quota = top_k - n_gt
        tri = (lax.broadcasted_iota(jnp.int32, (LANES, LANES), 0)
               <= lax.broadcasted_iota(jnp.int32, (LANES, LANES), 1)).astype(BF16)

        def tie_mask(j, seen):
            k = keys_ref[j]
            eq = jnp.where(k == kth_b, 1.0, 0.0)
            rank = _dot(eq.astype(BF16), tri) + seen
            sel = jnp.where(k > kth_b, 1.0, jnp.where(rank <= quota, eq, 0.0))
            madd_ref[j] = (1.0 - sel) * NEG
            return seen + jnp.sum(eq, axis=-1, keepdims=True)
        lax.fori_loop(0, n_blk, tie_mask, jnp.zeros((tq, 1), F32))

    qm = _pair_queries(q_ref[...])
    _softmax_init(m_scr, acc_scr)

    def step(k_ref, v_ref, blocks, slot0, kblk0, cached=False):
        tiles = []
        for blk in blocks:
            tile = jnp.clip(kblk0 + blk - qpos0 // LANES + BIAS_FAR, 0, bias_ref.shape[0] - 1)
            if cached:
                cols = _block_cols(blk, LANES)
                qk, v = _bmm(qh, k_ref[:, :, cols].astype(BF16)), _cache_values(v_ref[:, :, cols])
            else:
                rows = _block_rows(blk, LANES)
                qk, v = _bmm_nt(qm, _pair_keys(k_ref[rows, :])), _pair_values(v_ref[rows, :])
            tiles.append((qk + bias_ref[tile] + madd_ref[slot0 + blk][None], v, cached))
        _softmax_step(tiles, m_scr, acc_scr)

    if n_cache:
        qh = _head_queries(q_ref[...])

        group = _cache_group(tq, n_cache)

        def cache_step(j, _):
            step(cka_ref, cva_ref, [group * j + d for d in range(group)], 0, 0, cached=True)
            return 0
        lax.fori_loop(0, n_cache // group, cache_step, 0)

    if dyn_new:
        @pl.when(n_new % 2 == 1)
        def _():
            madd_ref[n_blk] = jnp.full((tq, LANES), NEG, F32)

        def new_step(j, _):
            step(nka_ref, nva_ref, (2 * j, 2 * j + 1), n_cache, past // LANES)
            return 0
        lax.fori_loop(0, (n_new + 1) // 2, new_step, 0)
    else:
        step(nka_ref, nva_ref, (0,), n_cache, past // LANES)
    o_ref[...] = _softmax_out(acc_scr).astype(BF16)


def _dsa(q, qi, misc, kcat, ka16, va16, bias, cache, *, batch, t_q, tq):
    t_kv = ka16.shape[0] // batch
    nq = t_q // tq
    past = 0 if cache is None else cache[0].shape[1]
    n_cache = past // LANES
    n_keys = past + t_q
    top_k = min(IDX_TOPK_MAX, n_keys // 4)
    assert tq % LANES == 0 or nq == 1
    assert n_cache % 2 == 0 and (cache is not None or (t_kv // LANES) % 2 == 0)
    qrow = lambda w: pl.BlockSpec((tq, w), lambda b, i: (b * (t_kv // tq) + i, 0))
    krow = lambda w: pl.BlockSpec((t_kv, w), lambda b, i: (b, 0))
    crow = lambda w: pl.BlockSpec((None, past, w), lambda b, i: (b, 0, 0))
    ins = [q, qi, misc]
    in_specs = [qrow(D_HEADS), qrow(D_HEADS), qrow(LANES)]
    if cache is not None:
        ins += list(cache)
        in_specs += [crow(IDX_DIM), _cache_spec(past), _cache_spec(past)]
    ins += [kcat, ka16, va16, bias]
    in_specs += [krow(4 * IDX_DIM), krow(D_HEADS), krow(D_HEADS),
                 pl.BlockSpec(bias.shape, lambda b, i: (0, 0, 0, 0))]
    n_blk_max = n_cache + t_kv // LANES
    return pl.pallas_call(
        functools.partial(_dsa_kernel, tq=tq, n_cache=n_cache, past=past, t_new=t_q, top_k=top_k,
                          dyn_new=cache is None),
        out_shape=jax.ShapeDtypeStruct((batch * t_q, D_HEADS), BF16),
        grid=(batch, nq),
        in_specs=in_specs,
        out_specs=pl.BlockSpec((tq, D_HEADS), lambda b, i: (b * nq + i, 0)),
        scratch_shapes=[pltpu.VMEM((n_blk_max, tq, LANES), jnp.int32),
                        pltpu.VMEM((n_blk_max, tq, LANES), F32),
                        pltpu.VMEM((N_HEADS * tq, 4 * IDX_DIM), BF16),
                        pltpu.VMEM((N_HEADS, tq, LANES), F32),
                        pltpu.VMEM((tq, LANES), jnp.int32)] + _softmax_scratch(tq),
        compiler_params=_params("arbitrary", "arbitrary"),
        name="dsa_attention",
    )(*ins)


FOX_GROUP = 4
DSA_TQ = 2 * LANES
COUNT_BLOCKS = 4
COUNT_STRIP = 64
FOX_TQ = 4 * LANES


def _fox_kernel(*refs, tq, tk, tkc, n_cache, past, dyn_new):
    if n_cache:
        q_ref, fq_ref, ck_ref, cv_ref, cf_ref, nk_ref, nv_ref, nf_ref, o_ref, m_scr, acc_scr = refs
    else:
        q_ref, fq_ref, nk_ref, nv_ref, nf_ref, o_ref, m_scr, acc_scr = refs
        ck_ref = cv_ref = cf_ref = None
    i = pl.program_id(1)
    group = tq // tk if dyn_new else 1
    qpos = past + i * tq + lax.broadcasted_iota(jnp.int32, (1, tq, 1), 1)
    col = lax.broadcasted_iota(jnp.int32, (1, 1, LANES), 2)

    qm = _pair_queries(q_ref[...])
    fq = jnp.stack([fq_ref[:, h:h + 1] for h in range(N_HEADS)])
    _softmax_init(m_scr, acc_scr)

    def step(k_ref, v_ref, f_ref, blocks, size, kpos_base=None, cached=False):
        tiles = []
        for blk in blocks:
            if cached:
                cols = _block_cols(blk, size)
                qk, v = _bmm(qh, k_ref[:, :, cols].astype(BF16)), _cache_values(v_ref[:, :, cols])
            else:
                rows = _block_rows(blk, size)
                qk, v = _bmm_nt(qm, _pair_keys(k_ref[rows, :])), _pair_values(v_ref[rows, :])
            s = qk + fq - f_ref[blk]
            if kpos_base is not None:
                s = jnp.where(kpos_base + blk * size + col <= qpos, s, NEG)
            tiles.append((s, v, cached))
        _softmax_step(tiles, m_scr, acc_scr)

    if n_cache:
        qh = _head_queries(q_ref[...])

        group_c = _cache_group(tq, n_cache)

        def cache_step(j, _):
            step(ck_ref, cv_ref, cf_ref, [group_c * j + d for d in range(group_c)], tkc, cached=True)
            return 0
        lax.fori_loop(0, n_cache // group_c, cache_step, 0)

    if dyn_new:
        def new_step(j, _):
            step(nk_ref, nv_ref, nf_ref, [group * j + d for d in range(group)], tk)
            return 0
        lax.fori_loop(0, i, new_step, 0)
        step(nk_ref, nv_ref, nf_ref, [group * i + d for d in range(group)], tk, past)
    else:
        step(nk_ref, nv_ref, nf_ref, [0], tk, past)
    o_ref[...] = _softmax_out(acc_scr).astype(BF16)


def _key_rows(f, tk):
    b, t, h = f.shape
    return f.reshape(b, t // tk, tk, h).transpose(0, 1, 3, 2)[:, :, :, None, :]


def _fox(q, fq, kb16, vb16, f_new, cache, *, batch, t_q, tq, tk, tkc):
    t_kv = kb16.shape[0] // batch
    nq = t_q // tq
    past = 0 if cache is None else cache[2].shape[1]
    n_cache = past // tkc
    assert tk == LANES and tkc == LANES
    qrow = lambda w: pl.BlockSpec((tq, w), lambda b, i: (b * (t_kv // tq) + i, 0))
    krow = lambda w: pl.BlockSpec((t_kv, w), lambda b, i: (b, 0))
    crow = lambda w: pl.BlockSpec((None, past, w), lambda b, i: (b, 0, 0))
    fblk = lambda nb, w: pl.BlockSpec((None, nb, N_HEADS, 1, w), lambda b, i: (b, 0, 0, 0, 0))
    ins = [q, fq]
    in_specs = [qrow(D_HEADS), qrow(N_HEADS)]
    if cache is not None:
        ins += [cache[0], cache[1], _key_rows(cache[2], tkc)]
        in_specs += [_cache_spec(past), _cache_spec(past), fblk(n_cache, tkc)]
    ins += [kb16, vb16, _key_rows(f_new, tk)]
    in_specs += [krow(D_HEADS), krow(D_HEADS), fblk(t_kv // tk, tk)]
    return pl.pallas_call(
        functools.partial(_fox_kernel, tq=tq, tk=tk, tkc=tkc, n_cache=n_cache, past=past, dyn_new=cache is None),
        out_shape=jax.ShapeDtypeStruct((batch * t_q, D_HEADS), BF16),
        grid=(batch, nq),
        in_specs=in_specs,
        out_specs=pl.BlockSpec((tq, D_HEADS), lambda b, i: (b * nq + i, 0)),
        scratch_shapes=_softmax_scratch(tq),
        compiler_params=_params("arbitrary", "arbitrary"),
        name="fox_attention",
    )(*ins)


def _out_ab_kernel(oa_ref, ob_ref, w_ref, x_ref, gt_ref, gpost_ref, o_ref):
    y = _dot(oa_ref[...], w_ref[:D_HEADS, :]) + _dot(ob_ref[...], w_ref[D_HEADS:, :])
    o_ref[...] = _residual(x_ref[...], y, gpost_ref[...], gt_ref[...], 1.0)


def _out_ab(oa, ob, w, x, gt, gpost, *, batch, t_q, tm):
    t_kv = oa.shape[0] // batch
    nt = t_q // tm
    arow = pl.BlockSpec((tm, D_HEADS), lambda b, i: (b * (t_kv // tm) + i, 0))
    xrow = pl.BlockSpec((tm, D_MODEL), lambda b, i: (b * nt + i, 0))
    return pl.pallas_call(
        _out_ab_kernel,
        out_shape=jax.ShapeDtypeStruct(x.shape, F32),
        grid=(batch, nt),
        in_specs=[arow, arow, pl.BlockSpec(w.shape, lambda b, i: (0, 0)), xrow,
                  pl.BlockSpec((tm // GROUP, 1, D_MODEL), lambda b, i: (b * nt + i, 0, 0)),
                  pl.BlockSpec((1, D_MODEL), lambda b, i: (0, 0))],
        out_specs=xrow,
        compiler_params=_params("arbitrary", "arbitrary"),
        name="out_ab",
    )(oa, ob, w, x, gt, gpost)


GDN_MISC0 = C_QKV + N_HEADS * DK_C


def _proj_gdn_kernel(x_ref, sh_ref, sc_ref, gpre_ref, w_ref, cw_ref, prev_ref, alog_ref, dtb_ref,
                     q_ref, k_ref, v_ref, gz_ref, misc_ref, tail_ref, zbuf):
    tm = x_ref.shape[0]
    halo = CONV_W - 1
    base = 8

    @pl.when(pl.program_id(1) == 0)
    def _():
        zbuf[base - halo:base, :] = prev_ref[...]

    h = _modulated(x_ref[...], gpre_ref[...], sh_ref[...], sc_ref[...]).astype(BF16)
    zbuf[base:base + tm, :] = _dot(h, w_ref[:, :C_QKV])
    gz_ref[...] = _dot(h, w_ref[:, C_QKV:GDN_MISC0])
    zm = _dot(h, w_ref[:, GDN_MISC0:])
    lane = lax.broadcasted_iota(jnp.int32, zm.shape, 1)
    beta = jax.nn.sigmoid(zm)
    g = -jnp.exp(alog_ref[...]) * jax.nn.softplus(zm + dtb_ref[...])
    misc_ref[...] = jnp.where(lane < N_HEADS, beta, g)

    for part, ref in enumerate((q_ref, k_ref, v_ref)):
        for hd in range(N_HEADS):
            c0 = part * N_HEADS * DK_C + hd * DK_C
            cols = slice(c0, c0 + DK_C)
            conv = zbuf[base:base + tm, cols] * cw_ref[halo:halo + 1, cols]
            for j in range(halo):
                conv = conv + zbuf[base - halo + j:base - halo + j + tm, cols] * cw_ref[j:j + 1, cols]
            a = _silu(conv)
            if part < 2:
                a = a * lax.rsqrt(jnp.sum(a * a, axis=-1, keepdims=True) + EPS)
                if part == 0:
                    a = a * DK_C ** -0.5
            ref[:, hd * DK_C:(hd + 1) * DK_C] = a
    tail = zbuf[base + tm - halo:base + tm, :]
    tail_ref[...] = tail
    zbuf[base - halo:base, :] = tail


def _proj_gdn(x, sh, sc, gpre, w, conv_w, conv_prev, alog_row, dtb_row, *, batch, tm):
    n = x.shape[0]
    t = n // batch
    nt = t // tm
    row = lambda wd: pl.BlockSpec((tm, wd), lambda b, i: (b * nt + i, 0))
    full = lambda a: pl.BlockSpec(a.shape, lambda b, i: (0,) * a.ndim)
    sds = lambda wd: jax.ShapeDtypeStruct((n, wd), F32)
    hd = N_HEADS * DK_C
    tail = pl.BlockSpec((None, CONV_W - 1, C_QKV), lambda b, i: (b, 0, 0))
    return pl.pallas_call(
        _proj_gdn_kernel,
        out_shape=[sds(hd), sds(hd), sds(hd), sds(hd), sds(LANES),
                   jax.ShapeDtypeStruct((batch, CONV_W - 1, C_QKV), F32)],
        grid=(batch, nt),
        in_specs=[row(D_MODEL),
                  pl.BlockSpec((tm // GROUP, 1, D_MODEL), lambda b, i: (b * nt + i, 0, 0)),
                  pl.BlockSpec((tm // GROUP, 1, D_MODEL), lambda b, i: (b * nt + i, 0, 0)),
                  pl.BlockSpec((1, D_MODEL), lambda b, i: (0, 0)),
                  full(w), full(conv_w), tail, full(alog_row), full(dtb_row)],
        out_specs=[row(hd), row(hd), row(hd), row(hd), row(LANES), tail],
        scratch_shapes=[pltpu.VMEM((8 + tm, C_QKV), F32)],
        compiler_params=_params("arbitrary", "arbitrary"),
        name="proj_gdn",
    )(x, sh, sc, gpre, w, conv_w, conv_prev, alog_row, dtb_row)


def _chunk_cumsum_kernel(g_ref, o_ref, *, chunk):
    x = g_ref[...]
    lane = lax.broadcasted_iota(jnp.int32, x.shape, 1) % chunk
    s = 1
    while s < chunk:
        x = x + jnp.where(lane >= s, pltpu.roll(x, s, 1), 0.0)
        s *= 2
    o_ref[...] = x


def _chunk_cumsum(g_rows, chunk):
    assert LANES % chunk == 0
    tiles = g_rows.reshape(-1, LANES)
    return pl.pallas_call(
        functools.partial(_chunk_cumsum_kernel, chunk=chunk),
        out_shape=jax.ShapeDtypeStruct(tiles.shape, F32),
        name="gdn_chunk_cumsum",
    )(tiles).reshape(g_rows.shape)


GDN_CHUNKS = 4


def _gdn_kernel(q_ref, k_ref, v_ref, col_ref, row_ref, s0_ref, o_ref, s_ref):
    n_chunks, _, _, c = row_ref.shape

    @pl.when(pl.program_id(1) == 0)
    def _():
        s_ref[...] = s0_ref[...]

    def stacked(ref, col0, width):
        return jnp.stack([ref[j * c:(j + 1) * c, col0 + h * width:col0 + (h + 1) * width]
                          for j in range(n_chunks) for h in range(N_HEADS)])
    q, k, v = stacked(q_ref, 0, DK_C), stacked(k_ref, 0, DK_C), stacked(v_ref, 0, DK_C)
    beta = stacked(col_ref, 0, 1)
    gc = stacked(col_ref, N_HEADS, 1)
    gc_row = jnp.concatenate([row_ref[j] for j in range(n_chunks)], axis=0)
    gc_last = gc[:, c - 1:c, :]
    ri = lax.broadcasted_iota(jnp.int32, (1, c, c), 1)
    ci = lax.broadcasted_iota(jnp.int32, (1, c, c), 2)
    causal = ri >= ci
    strict = ri > ci
    eye = jnp.where(ri == ci, 1.0, 0.0)
    decay = jnp.exp(jnp.where(causal, gc - gc_row, NEG))
    k16 = k.astype(BF16)
    a = jnp.where(strict, beta * _bmm_nt(k16, k16) * decay, 0.0)
    inv = eye - a
    p = a
    n = 2
    while n < c:
        p = _dot3(p, p, _bmm)
        inv = inv + _dot3(inv, p, _bmm)
        n *= 2
    sol = _dot3(inv, jnp.concatenate([v * beta, k * (beta * jnp.exp(gc))], axis=-1), _bmm)
    u, w = sol[:, :, :DK_C], sol[:, :, DK_C:]
    qk = jnp.where(causal, _bmm_nt(q.astype(BF16), k16) * decay, 0.0)
    q_dec = (q * jnp.exp(gc)).astype(BF16)
    k_dec = k * jnp.exp(gc_last - gc)
    k_dec_t = jnp.stack([k_dec[i].T for i in range(k_dec.shape[0])]).astype(BF16)
    w16, qk16, g_last = w.astype(BF16), qk.astype(BF16), jnp.exp(gc_last)
    s = s_ref[...]
    for j in range(n_chunks):
        grp = slice(j * N_HEADS, (j + 1) * N_HEADS)
        s16 = s.astype(BF16)
        v16 = (u[grp] - _bmm(w16[grp], s16)).astype(BF16)
        o = _bmm(q_dec[grp], s16) + _bmm(qk16[grp], v16)
        for h in range(N_HEADS):
            o_ref[j * c:(j + 1) * c, h * DK_C:(h + 1) * DK_C] = o[h]
        s = s * g_last[grp] + _bmm(k_dec_t[grp], v16)
    s_ref[...] = s


def _gdn(q, k, v, col, row, s0, *, batch, chunk):
    n = q.shape[0]
    per_step = GDN_CHUNKS if (n // batch // chunk) % GDN_CHUNKS == 0 else 1
    nc = n // batch // chunk // per_step
    hd = N_HEADS * DK_C
    blk = lambda w: pl.BlockSpec((per_step * chunk, w), lambda b, i: (b * nc + i, 0))
    state = pl.BlockSpec((None, N_HEADS, DK_C, DK_C), lambda b, i: (b, 0, 0, 0))
    return pl.pallas_call(
        _gdn_kernel,
        out_shape=[jax.ShapeDtypeStruct((n, hd), F32), jax.ShapeDtypeStruct(s0.shape, F32)],
        grid=(batch, nc),
        in_specs=[blk(hd), blk(hd), blk(hd), blk(2 * N_HEADS),
                  pl.BlockSpec((per_step, N_HEADS, 1, chunk), lambda b, i: (b * nc + i, 0, 0, 0)), state],
        out_specs=[blk(hd), state],
        compiler_params=_params("arbitrary", "arbitrary"),
        name="gated_delta_rule",
    )(q, k, v, col, row, s0)


def _out_gdn_kernel(o_ref, gz_ref, nw_ref, w_ref, x_ref, gt_ref, gpost_ref, out_ref):
    parts = []
    for h in range(N_HEADS):
        cols = slice(h * DK_C, (h + 1) * DK_C)
        o = o_ref[:, cols]
        o = o * lax.rsqrt(jnp.mean(o * o, axis=-1, keepdims=True) + EPS) * nw_ref[...]
        parts.append((o * _silu(gz_ref[:, cols])).astype(BF16))
    y = _dot(jnp.concatenate(parts, axis=-1), w_ref[...])
    out_ref[...] = _residual(x_ref[...], y, gpost_ref[...], gt_ref[...], 1.0)


def _out_gdn(o, gz, norm_w, w, x, gt, gpost, tm=512):
    n = x.shape[0]
    tm = min(tm, n)
    row = pl.BlockSpec((tm, D_MODEL), lambda i: (i, 0))
    return pl.pallas_call(
        _out_gdn_kernel,
        out_shape=jax.ShapeDtypeStruct(x.shape, F32),
        grid=(n // tm,),
        in_specs=[row, row, pl.BlockSpec((1, DK_C), lambda i: (0, 0)), pl.BlockSpec(w.shape, lambda i: (0, 0)),
                  row, pl.BlockSpec((tm // GROUP, 1, D_MODEL), lambda i: (i, 0, 0)),
                  pl.BlockSpec((1, D_MODEL), lambda i: (0, 0))],
        out_specs=row,
        compiler_params=_params("arbitrary"),
        name="out_gdn",
    )(o, gz, norm_w, w, x, gt, gpost)


def _mixer_ab(x, mod, w, past, *, batch, t, bias):
    sh, sc, gt = mod
    (qa, qb, ka, va, kb, vb, ka16, va16, kb16, vb16, qi, misc, kcat) = _proj_ab(
        x, sh, sc, w["pre"], w["ab_main"], w["ab_idx_hi"], w["ab_idx_lo"], batch=batch)
    ki = misc[:, :IDX_DIM]
    fl = misc[:, IDX_DIM + N_HEADS:IDX_DIM + 2 * N_HEADS].reshape(batch, t, N_HEADS)
    t_kv = -(-t // LANES) * LANES
    pad_rows = lambda a: a if t_kv == t else jnp.pad(
        a.reshape(batch, t, -1), ((0, 0), (0, t_kv - t), (0, 0))).reshape(batch * t_kv, -1)
    fl_p = fl if t_kv == t else jnp.pad(fl, ((0, 0), (0, t_kv - t), (0, 0)))
    if past is None:
        logf_b, f_new = _logf(fl_p, w["ab_b_f"], None)
        cache_a = cache_b = None
    else:
        cak, cav, caik, cbk, cbv, cblf = past
        p = cak.shape[1]
        logf_b, f_new, f_cache = _logf(fl_p, w["ab_b_f"], cblf)
        cache_a = (caik, _feature_major(cak), _feature_major(cav))
        cache_b = (_feature_major(cbk), _feature_major(cbv), _unblocked_rows(f_cache))
    logf = _unblocked_rows(logf_b)[:, :t]
    f_new = _unblocked_rows(f_new)
    o_a = _dsa(pad_rows(qa), pad_rows(qi), pad_rows(misc), pad_rows(kcat), pad_rows(ka16), pad_rows(va16), bias, cache_a,
               batch=batch, t_q=t, tq=bias.shape[2])
    o_b = _fox(pad_rows(qb), f_new.reshape(batch * t_kv, N_HEADS), pad_rows(kb16), pad_rows(vb16), f_new, cache_b,
               batch=batch, t_q=t, tq=min(FOX_TQ, t), tk=LANES, tkc=LANES)
    x = _out_ab(o_a, o_b, w["ab_w_out"], x, gt, w["post"], batch=batch, t_q=t, tm=min(512, t))
    return x, (ka, va, ki.reshape(batch, t, IDX_DIM), kb, vb, logf)


def _mixer_gdn(x, mod, w, past, *, batch, t):
    sh, sc, gt = mod
    if past is None:
        conv_prev = jnp.zeros((batch, CONV_W - 1, C_QKV), F32)
        s0 = jnp.zeros((batch, N_HEADS, DK_C, DK_C), F32)
    else:
        conv_prev, s0 = past
    q, k, v, gz, misc, new_conv = _proj_gdn(x, sh, sc, w["pre"], w["gdn_w_in"], w["gdn_conv_w"], conv_prev,
                                            w["gdn_alog_row"], w["gdn_dtb_row"], batch=batch, tm=min(512, t))
    chunk = min(t, CHUNK)
    n = batch * t
    gc_rows = _chunk_cumsum(misc[:, N_HEADS:2 * N_HEADS].T, chunk)
    col = jnp.concatenate([misc[:, :N_HEADS], gc_rows.T], axis=-1)
    row = gc_rows.reshape(N_HEADS, n // chunk, chunk).transpose(1, 0, 2)[:, :, None, :]
    o, s_new = _gdn(q, k, v, col, row, s0, batch=batch, chunk=chunk)
    x = _out_gdn(o, gz, w["gdn_norm_w"], w["gdn_w_out"], x, gt, w["post"])
    return x, (new_conv, s_new)


def _forward(x, mod_all, w, past, *, batch, t):
    new_state = []
    groups = lambda v: jnp.repeat(v, t // GROUP, axis=0)[:, None, :]
    for l in range(mod_all.shape[0]):
        m = mod_all[l].reshape(batch, 3, 3, D_MODEL)
        mod = [[groups(m[:, s, j]) for j in range(3)] for s in range(3)]
        pre = lambda s: w["norm_pre"][l, s][None, :]
        post = lambda s: w["norm_post"][l, s][None, :]
        x = _ffn(x, mod[0], pre(0), post(0), w["ffn_w_gate"][l, 0], w["ffn_w_up"][l, 0], w["ffn_w_down"][l, 0])
        wl = dict(w, pre=pre(1), post=post(1))
        if l % 2 == 0:
            x, st = _mixer_ab(x, mod[1], wl, None if past is None else past[:6], batch=batch, t=t,
                              bias=w["bias_tiles"][min(DSA_TQ, t)])
        else:
            x, st = _mixer_gdn(x, mod[1], wl, None if past is None else past[6:], batch=batch, t=t)
        new_state.extend(st)
        x = _ffn(x, mod[2], pre(2), post(2), w["ffn_w_gate"][l, 1], w["ffn_w_up"][l, 1], w["ffn_w_down"][l, 1])
    return x, new_state


def _prepare_weights(rel_bias, ab_w_in, ab_b_f, ab_w_out, gdn_w_in, gdn_conv_w, gdn_A_log, gdn_dt_bias,
                     gdn_norm_w, gdn_w_out, norm_pre, norm_post, ffn_w_gate, ffn_w_up, ffn_w_down, tqs):
    o = [0]
    for s in (D_HEADS, D_HEADS, D_HEADS, D_HEADS, IDX_DIM, N_HEADS, D_HEADS, D_HEADS, D_HEADS, N_HEADS):
        o.append(o[-1] + s)
    cols = lambda i: ab_w_in[:, o[i]:o[i + 1]]
    qa, ka, va, qi, ki, wi, qb, kb, vb, fl = (cols(i) for i in range(10))
    idx = jnp.concatenate([qi, ki, wi, fl, jnp.zeros((D_MODEL, IDX_COLS - D_HEADS - IDX_DIM - 2 * N_HEADS), F32)], 1)
    idx_hi = idx.astype(BF16)
    gdn_in = jnp.concatenate([gdn_w_in, jnp.zeros((D_MODEL, LANES - 2 * N_HEADS), F32)], 1)
    lane_row = lambda v: jnp.concatenate([jnp.zeros((N_HEADS,), F32), v.astype(F32),
                                          jnp.zeros((LANES - 2 * N_HEADS,), F32)])[None, :]
    return dict(
        ab_main=jnp.concatenate([qa, qb, ka, va, kb, vb], 1).astype(BF16),
        ab_idx_hi=idx_hi, ab_idx_lo=(idx - idx_hi.astype(F32)).astype(BF16),
        ab_b_f=ab_b_f, ab_w_out=ab_w_out.astype(BF16),
        gdn_w_in=gdn_in.astype(BF16), gdn_conv_w=gdn_conv_w,
        gdn_alog_row=lane_row(gdn_A_log), gdn_dtb_row=lane_row(gdn_dt_bias),
        gdn_norm_w=gdn_norm_w[None, :], gdn_w_out=gdn_w_out.astype(BF16),
        norm_pre=norm_pre, norm_post=norm_post,
        ffn_w_gate=ffn_w_gate.astype(BF16), ffn_w_up=ffn_w_up.astype(BF16), ffn_w_down=ffn_w_down.astype(BF16),
        bias_tiles={tq: _bias_tiles(rel_bias, tq) for tq in tqs},
    )


def kernel(x_prompt, x_sample, cache_a_k, cache_a_v, cache_a_idx_k, cache_b_k, cache_b_v, cache_b_logf,
           state_c_conv, state_c_rec, c_prompt, c_sample, rel_bias, ab_w_in, ab_b_f, ab_w_out,
           gdn_w_in, gdn_conv_w, gdn_A_log, gdn_dt_bias, gdn_norm_w, gdn_w_out, norm_pre, norm_post,
           ada_w, ada_b, ffn_w_gate, ffn_w_up, ffn_w_down):
    bp, tp, _ = x_prompt.shape
    bs, ts, _ = x_sample.shape
    w = _prepare_weights(rel_bias, ab_w_in, ab_b_f, ab_w_out, gdn_w_in, gdn_conv_w, gdn_A_log, gdn_dt_bias,
                         gdn_norm_w, gdn_w_out, norm_pre, norm_post, ffn_w_gate, ffn_w_up, ffn_w_down,
                         tqs={min(DSA_TQ, tp), min(DSA_TQ, ts)})
    mod = _ada(jnp.concatenate([c_prompt, c_sample], 0), ada_w, ada_b)
    y_p, st_p = _forward(x_prompt.reshape(bp * tp, D_MODEL), mod[:, :bp], w, None, batch=bp, t=tp)
    past = (cache_a_k, cache_a_v, cache_a_idx_k, cache_b_k, cache_b_v, cache_b_logf, state_c_conv, state_c_rec)
    y_s, st_s = _forward(x_sample.reshape(bs * ts, D_MODEL), mod[:, bp:], w, past, batch=bs, t=ts)
    return (y_p.reshape(bp, tp, D_MODEL), y_s.reshape(bs, ts, D_MODEL), *st_p, *st_s)
```
